```python
import jax, jax.numpy as jnp
from jax import lax
import numpy as np

D_MODEL = 4096
BATCH = 2
SEQ = 8192
DEPTH = 2

GRID_W = 64
CTX_LEN = 256
HEAD_DIM = 128
GM_GROUPS = 8
GM_WIDTH = GM_GROUPS * HEAD_DIM
GM_CHUNK = 128
NA_HEADS = 8
NA_WIDTH = NA_HEADS * HEAD_DIM
NA_ROWS = 8
NA_COLS = 16
RET_HEADS = 8
RET_DK = 128
RET_DV = 256
RET_QK_WIDTH = RET_HEADS * RET_DK
RET_V_WIDTH = RET_HEADS * RET_DV
RET_CHUNK = 128
RET_DECAY_BASE = 5.0
ROPE_BASE = 10000.0
MOE_GROUPS = 4
MOE_EXPERTS_PER_GROUP = 8
MOE_EXPERTS = MOE_GROUPS * MOE_EXPERTS_PER_GROUP
MOE_TOPK = 2
MOE_HIDDEN = 512
MOE_BLOCK = 128
EPS = 1e-6
NEG_INF = -1e30

IN_COLS = (
    ("gate_a", D_MODEL), ("gate_b", D_MODEL), ("gate_c", D_MODEL),
    ("gm_u", GM_WIDTH), ("gm_v", GM_WIDTH),
    ("na_q", NA_WIDTH), ("na_k", NA_WIDTH), ("na_v", NA_WIDTH),
    ("ret_q", RET_QK_WIDTH), ("ret_k", RET_QK_WIDTH), ("ret_v", RET_V_WIDTH), ("ret_g", RET_V_WIDTH),
)
IN_TOTAL = 3 * D_MODEL + 2 * GM_WIDTH + 3 * NA_WIDTH + 2 * RET_QK_WIDTH + 2 * RET_V_WIDTH
CTX_KV_NAMES = ("na_k", "na_v", "ret_k", "ret_v")

kernel_name = "hybrid_gmlp_natten_retention_hmoe_dit"


def _col_ranges():
    ranges, off = {}, 0
    for name, width in IN_COLS:
        ranges[name] = (off, off + width)
        off += width
    return ranges


def rms_norm(x, g):
    x32 = x.astype(jnp.float32)
    y = x32 * lax.rsqrt(jnp.mean(x32 * x32, axis=-1, keepdims=True) + EPS)
    return (y * g.astype(jnp.float32)).astype(x.dtype)


def modulate(h, shift, scale):
    return h * (1 + scale) + shift


def to_heads(t, n_heads):
    b, n, w = t.shape
    return t.reshape(b, n, n_heads, w // n_heads).transpose(0, 2, 1, 3)


def from_heads(t):
    b, h, n, d = t.shape
    return t.transpose(0, 2, 1, 3).reshape(b, n, h * d)


def chunk_gmlp(u, v, norm_g, ws, bs):
    b, n, _ = u.shape
    u = jax.nn.gelu(u)
    v32 = jax.nn.gelu(v).astype(jnp.float32)
    mu = jnp.mean(v32, axis=-1, keepdims=True)
    var = jnp.mean(jnp.square(v32 - mu), axis=-1, keepdims=True)
    vn = ((v32 - mu) * lax.rsqrt(var + EPS) * norm_g.astype(jnp.float32)).astype(u.dtype)
    vn = vn.reshape(b, n // GM_CHUNK, GM_CHUNK, GM_GROUPS, HEAD_DIM)
    mixed = jnp.einsum('gts,bnsgd->bntgd', ws, vn) + bs.T[None, None, :, :, None]
    return u * mixed.reshape(b, n, GM_WIDTH)


def neighbourhood_attention(q, k, v, k_ctx, v_ctx, rpb):
    b, h, s, hd = q.shape
    rows = s // GRID_W
    kr = min(NA_ROWS, rows)
    grid = lambda t: t.reshape(b, h, rows, GRID_W, hd)
    qg = grid(q * (hd ** -0.5))
    r = jnp.arange(rows)
    r_start = jnp.clip(r - kr // 2, 0, rows - kr)
    ridx = r_start[:, None] + jnp.arange(kr)
    cq = jnp.arange(GRID_W)
    c_start = jnp.clip(cq - NA_COLS // 2, 0, GRID_W - NA_COLS)
    col_ok = (cq[None, :] >= c_start[:, None]) & (cq[None, :] < c_start[:, None] + NA_COLS)
    dr = ridx - r[:, None] + NA_ROWS - 1
    dc = jnp.clip(cq[None, :] - cq[:, None] + NA_COLS - 1, 0, 2 * NA_COLS - 2)
    bias = rpb[:, dr[:, None, :, None], dc[None, :, None, :]].astype(jnp.float32)
    kg = grid(k)[:, :, ridx]
    vg = grid(v)[:, :, ridx]
    s_loc = jnp.einsum('bhrcd,bhrjwd->bhrcjw', qg, kg).astype(jnp.float32) + bias
    s_loc = jnp.where(col_ok[:, None, :], s_loc, NEG_INF)
    s_ctx = jnp.einsum('bhrcd,bhld->bhrcl', qg, k_ctx).astype(jnp.float32)
    n_loc = kr * GRID_W
    p = jax.nn.softmax(jnp.concatenate([s_loc.reshape(b, h, rows, GRID_W, n_loc), s_ctx], axis=-1), axis=-1)
    p = p.astype(v.dtype)
    p_loc = p[..., :n_loc].reshape(b, h, rows, GRID_W, kr, GRID_W)
    p_ctx = p[..., n_loc:]
    o = jnp.einsum('bhrcjw,bhrjwd->bhrcd', p_loc, vg) + jnp.einsum('bhrcl,bhld->bhrcd', p_ctx, v_ctx)
    return o.reshape(b, h, s, hd)


def context_attention(q, k, v):
    hd = q.shape[-1]
    sc = jnp.einsum('bhid,bhjd->bhij', q * (hd ** -0.5), k).astype(jnp.float32)
    p = jax.nn.softmax(sc, axis=-1).astype(v.dtype)
    return jnp.einsum('bhij,bhjd->bhid', p, v)


def axial_rope(t):
    n, hd = t.shape[2], t.shape[-1]
    half, nf = hd // 2, hd // 4
    pos = jnp.arange(n)
    p_row = (pos // GRID_W).astype(jnp.float32)
    p_col = (pos % GRID_W).astype(jnp.float32)
    inv = ROPE_BASE ** (-jnp.arange(nf, dtype=jnp.float32) / nf)

    def rot(a, p):
        ang = p[:, None] * inv[None, :]
        cos, sin = jnp.cos(ang), jnp.sin(ang)
        a1, a2 = a[..., :nf], a[..., nf:]
        return jnp.concatenate([a1 * cos - a2 * sin, a2 * cos + a1 * sin], axis=-1)

    t32 = t.astype(jnp.float32)
    out = jnp.concatenate([rot(t32[..., :half], p_row), rot(t32[..., half:], p_col)], axis=-1)
    return out.astype(t.dtype)


def retention_chunked(q, k, v, log_g, s0, include_diag):
    q, k, v = (t.astype(jnp.float32) for t in (q, k, v))
    b, h, n, dk = q.shape
    dv = v.shape[-1]
    nc = n // RET_CHUNK
    idx = jnp.arange(RET_CHUNK, dtype=jnp.float32)
    rel = idx[:, None] - idx[None, :]
    keep = rel >= 0 if include_diag else rel > 0
    dmat = jnp.where(keep, jnp.exp(log_g[:, None, None] * jnp.maximum(rel, 0.0)), 0.0)
    q_dec = jnp.exp(log_g[:, None] * (idx + 1.0))[None, :, :, None]
    k_dec = jnp.exp(log_g[:, None] * (RET_CHUNK - 1.0 - idx))[None, :, :, None]
    c_dec = jnp.exp(log_g * RET_CHUNK)[None, :, None, None]
    chunks = lambda t: t.reshape(b, h, nc, RET_CHUNK, t.shape[-1]).transpose(2, 0, 1, 3, 4)

    def step(s, qkv):
        qc, kc, vc = qkv
        inner = jnp.einsum('bhij,bhjv->bhiv', jnp.einsum('bhid,bhjd->bhij', qc, kc) * dmat, vc)
        cross = jnp.einsum('bhid,bhdv->bhiv', qc * q_dec, s)
        s = s * c_dec + jnp.einsum('bhjd,bhjv->bhdv', kc * k_dec, vc)
        return s, inner + cross

    _, o = lax.scan(step, s0.astype(jnp.float32), (chunks(q), chunks(k), chunks(v)))
    return o.transpose(1, 2, 0, 3, 4).reshape(b, h, n, dv)


def context_state(k, v, log_g, reverse):
    n = k.shape[2]
    pos = jnp.arange(n, dtype=jnp.float32)
    steps_after = pos if reverse else (n - 1.0) - pos
    w = jnp.exp(log_g[:, None] * steps_after)
    return jnp.einsum('bhld,bhlv->bhdv', k.astype(jnp.float32) * w[None, :, :, None], v.astype(jnp.float32))


def head_norm(o):
    mu = jnp.mean(o, axis=-1, keepdims=True)
    var = jnp.mean(jnp.square(o - mu), axis=-1, keepdims=True)
    return (o - mu) * lax.rsqrt(var + EPS)


def retention_branch(q, k, v, g, s_f, s_b, log_gf, log_gb):
    o_f = retention_chunked(q, k, v, log_gf, s_f, True)
    o_b = jnp.flip(retention_chunked(jnp.flip(q, 2), jnp.flip(k, 2), jnp.flip(v, 2), log_gb, s_b, False), 2)
    return from_heads(head_norm(o_f + o_b)).astype(g.dtype) * jax.nn.silu(g)


def merge_branches(parts, a, bb, r, w_a, w_b, w_c, w_o):
    y = (jax.nn.sigmoid(parts["gate_a"]) * (a @ w_a)
         + jax.nn.sigmoid(parts["gate_b"]) * (bb @ w_b)
         + jax.nn.sigmoid(parts["gate_c"]) * (r @ w_c))
    return y @ w_o


def token_mixer(h, hc, w_in, gm_norm_g, gm_ws, gm_bs, na_rpb, dec_f, dec_b, w_a, w_b, w_c, w_o, need_ctx):
    cols = _col_ranges()
    z_full = h @ w_in
    z = {name: z_full[..., lo:hi] for name, (lo, hi) in cols.items()}
    if need_ctx:
        zc_full = hc @ w_in
        zc = {name: zc_full[..., lo:hi] for name, (lo, hi) in cols.items()}
    else:
        zc = {name: hc @ w_in[:, cols[name][0]:cols[name][1]] for name in CTX_KV_NAMES}

    a = chunk_gmlp(z["gm_u"], z["gm_v"], gm_norm_g, gm_ws, gm_bs)

    nk_c, nv_c = to_heads(zc["na_k"], NA_HEADS), to_heads(zc["na_v"], NA_HEADS)
    bb = from_heads(neighbourhood_attention(to_heads(z["na_q"], NA_HEADS), to_heads(z["na_k"], NA_HEADS),
                                            to_heads(z["na_v"], NA_HEADS), nk_c, nv_c, na_rpb))

    k_scale = RET_DK ** -0.5
    log_gf = jnp.log1p(-jnp.exp2(dec_f.astype(jnp.float32)))
    log_gb = jnp.log1p(-jnp.exp2(dec_b.astype(jnp.float32)))
    rk_c = to_heads(zc["ret_k"], RET_HEADS) * k_scale
    rv_c = to_heads(zc["ret_v"], RET_HEADS)
    s_f = context_state(rk_c, rv_c, log_gf, reverse=False)
    s_b = context_state(rk_c, rv_c, log_gb, reverse=True)
    rq = axial_rope(to_heads(z["ret_q"], RET_HEADS))
    rk = axial_rope(to_heads(z["ret_k"], RET_HEADS)) * k_scale
    r = retention_branch(rq, rk, to_heads(z["ret_v"], RET_HEADS), z["ret_g"], s_f, s_b, log_gf, log_gb)

    out = merge_branches(z, a, bb, r, w_a, w_b, w_c, w_o)
    if not need_ctx:
        return out, None

    a_c = chunk_gmlp(zc["gm_u"], zc["gm_v"], gm_norm_g, gm_ws, gm_bs)
    b_c = from_heads(context_attention(to_heads(zc["na_q"], NA_HEADS), nk_c, nv_c))
    zeros = jnp.zeros_like(s_f)
    r_c = retention_branch(to_heads(zc["ret_q"], RET_HEADS), rk_c, rv_c, zc["ret_g"], zeros, zeros, log_gf, log_gb)
    out_c = merge_branches(zc, a_c, b_c, r_c, w_a, w_b, w_c, w_o)
    return out, out_c


def hierarchical_moe(hf, w_group, w_expert, w1, w3, w2):
    t = hf.shape[0]
    p_grp = jax.nn.softmax((hf @ w_group).astype(jnp.float32), axis=-1)
    grp = jnp.argmax(p_grp, axis=-1)
    p_sel = jnp.take_along_axis(p_grp, grp[:, None], axis=-1)
    le = (hf @ w_expert).astype(jnp.float32).reshape(t, MOE_GROUPS, MOE_EXPERTS_PER_GROUP)
    le = jnp.take_along_axis(le, grp[:, None, None], axis=1)[:, 0]
    top_w, top_i = lax.top_k(jax.nn.softmax(le, axis=-1), MOE_TOPK)
    weight = p_sel * top_w / jnp.sum(top_w, axis=-1, keepdims=True)
    eid = grp[:, None] * MOE_EXPERTS_PER_GROUP + top_i

    n_assign = t * MOE_TOPK
    flat_e = eid.reshape(-1).astype(jnp.int32)
    flat_t = jnp.repeat(jnp.arange(t, dtype=jnp.int32), MOE_TOPK)
    flat_w = weight.reshape(-1).astype(hf.dtype)
    order = jnp.argsort(flat_e)
    se, st, sw = flat_e[order], flat_t[order], flat_w[order]
    counts = jax.ops.segment_sum(jnp.ones_like(flat_e), flat_e, num_segments=MOE_EXPERTS)
    start = jnp.cumsum(counts) - counts
    padded = (counts + MOE_BLOCK - 1) // MOE_BLOCK * MOE_BLOCK
    p_end = jnp.cumsum(padded)
    p_start = p_end - padded
    dest = p_start[se] + jnp.arange(n_assign, dtype=jnp.int32) - start[se]
    n_blocks = (n_assign + MOE_EXPERTS * (MOE_BLOCK - 1)) // MOE_BLOCK
    n_rows = n_blocks * MOE_BLOCK
    row_tok = jnp.zeros((n_rows,), jnp.int32).at[dest].set(st)
    row_w = jnp.zeros((n_rows,), hf.dtype).at[dest].set(sw)
    block_e = jnp.clip(jnp.searchsorted(p_end, jnp.arange(n_blocks) * MOE_BLOCK, side='right'), 0, MOE_EXPERTS - 1)

    def run_block(blk):
        e, toks, wts = blk
        xb = hf[toks]
        y = (jax.nn.silu(xb @ w1[e]) * (xb @ w3[e])) @ w2[e]
        return y * wts[:, None]

    ys = lax.map(run_block, (block_e, row_tok.reshape(n_blocks, MOE_BLOCK), row_w.reshape(n_blocks, MOE_BLOCK)))
    return jnp.zeros_like(hf).at[row_tok].add(ys.reshape(n_rows, -1))


def setup_inputs(seed: int = 0) -> dict:
    key = jax.random.key(seed)
    ks = jax.random.split(key, 32)
    f32 = jnp.float32
    nrm = lambda k, shape, scale: jax.random.normal(k, shape, f32) * scale
    d = D_MODEL
    decay_base = -(RET_DECAY_BASE + jnp.arange(RET_HEADS, dtype=f32))
    return {
        "x": nrm(ks[0], (BATCH, SEQ, d), 1.0),
        "c": nrm(ks[1], (BATCH, d), 1.0),
        "ctx": nrm(ks[2], (BATCH, CTX_LEN, d), 1.0),
        "c_ctx": nrm(ks[3], (d,), 1.0),
        "ada_w": nrm(ks[4], (DEPTH, d, 6 * d), 0.5 * d ** -0.5),
        "ada_b": nrm(ks[5], (DEPTH, 6 * d), 0.02),
        "norm1_g": 1.0 + nrm(ks[6], (DEPTH, d), 0.02),
        "w_in": nrm(ks[7], (DEPTH, d, IN_TOTAL), d ** -0.5),
        "gm_norm_g": 1.0 + nrm(ks[8], (DEPTH, GM_WIDTH), 0.02),
        "gm_ws": nrm(ks[9], (DEPTH, GM_GROUPS, GM_CHUNK, GM_CHUNK), GM_CHUNK ** -0.5),
        "gm_bs": 1.0 + nrm(ks[10], (DEPTH, GM_GROUPS, GM_CHUNK), 0.02),
        "na_rpb": nrm(ks[11], (DEPTH, NA_HEADS, 2 * NA_ROWS - 1, 2 * NA_COLS - 1), 0.1),
        "ret_decay_fwd": decay_base + nrm(ks[12], (DEPTH, RET_HEADS), 0.1),
        "ret_decay_bwd": decay_base + nrm(ks[13], (DEPTH, RET_HEADS), 0.1),
        "w_branch_a": nrm(ks[14], (DEPTH, GM_WIDTH, d), GM_WIDTH ** -0.5),
        "w_branch_b": nrm(ks[15], (DEPTH, NA_WIDTH, d), NA_WIDTH ** -0.5),
        "w_branch_c": nrm(ks[16], (DEPTH, RET_V_WIDTH, d), RET_V_WIDTH ** -0.5),
        "w_out": nrm(ks[17], (DEPTH, d, d), d ** -0.5),
        "norm2_g": 1.0 + nrm(ks[18], (DEPTH, d), 0.02),
        "moe_w_group": nrm(ks[19], (DEPTH, d, MOE_GROUPS), d ** -0.5),
        "moe_w_expert": nrm(ks[20], (DEPTH, d, MOE_EXPERTS), d ** -0.5),
        "moe_w1": nrm(ks[21], (DEPTH, MOE_EXPERTS, d, MOE_HIDDEN), d ** -0.5),
        "moe_w3": nrm(ks[22], (DEPTH, MOE_EXPERTS, d, MOE_HIDDEN), d ** -0.5),
        "moe_w2": nrm(ks[23], (DEPTH, MOE_EXPERTS, MOE_HIDDEN, d), MOE_HIDDEN ** -0.5),
        "final_norm_g": 1.0 + nrm(ks[24], (d,), 0.02),
    }


def reference(x, c, ctx, c_ctx, ada_w, ada_b, norm1_g, w_in, gm_norm_g, gm_ws, gm_bs, na_rpb,
              ret_decay_fwd, ret_decay_bwd, w_branch_a, w_branch_b, w_branch_c, w_out, norm2_g,
              moe_w_group, moe_w_expert, moe_w1, moe_w3, moe_w2, final_norm_g):
    b, s, d = x.shape
    n_ctx = ctx.shape[1]
    xc = ctx
    for l in range(DEPTH):
        need_ctx = l < DEPTH - 1
        mod = jax.nn.silu(c) @ ada_w[l] + ada_b[l]
        mod_c = jax.nn.silu(c_ctx) @ ada_w[l] + ada_b[l]
        sh1, sc1, g1, sh2, sc2, g2 = jnp.split(mod[:, None, :], 6, axis=-1)
        sh1c, sc1c, g1c, sh2c, sc2c, g2c = jnp.split(mod_c, 6)

        h = modulate(rms_norm(x, norm1_g[l]), sh1, sc1)
        hc = modulate(rms_norm(xc, norm1_g[l]), sh1c, sc1c)
        out, out_c = token_mixer(h, hc, w_in[l], gm_norm_g[l], gm_ws[l], gm_bs[l], na_rpb[l],
                                 ret_decay_fwd[l], ret_decay_bwd[l], w_branch_a[l], w_branch_b[l],
                                 w_branch_c[l], w_out[l], need_ctx)
        x = x + g1 * out
        h2 = modulate(rms_norm(x, norm2_g[l]), sh2, sc2).reshape(b * s, d)
        if need_ctx:
            xc = xc + g1c * out_c
            h2c = modulate(rms_norm(xc, norm2_g[l]), sh2c, sc2c).reshape(b * n_ctx, d)
            f = hierarchical_moe(jnp.concatenate([h2, h2c], axis=0), moe_w_group[l], moe_w_expert[l],
                                 moe_w1[l], moe_w3[l], moe_w2[l])
            x = x + g2 * f[:b * s].reshape(b, s, d)
            xc = xc + g2c * f[b * s:].reshape(b, n_ctx, d)
        else:
            f = hierarchical_moe(h2, moe_w_group[l], moe_w_expert[l], moe_w1[l], moe_w3[l], moe_w2[l])
            x = x + g2 * f.reshape(b, s, d)
    return rms_norm(x, final_norm_g)
```

```python
import functools

import jax
import jax.numpy as jnp
import numpy as np
from jax import lax
from jax.experimental import pallas as pl
from jax.experimental.pallas import tpu as pltpu

F32 = jnp.float32
BF16 = jnp.bfloat16
SDS = jax.ShapeDtypeStruct
BS = pl.BlockSpec

EPS = 1e-6
NEG_INF = -1e30
HEAD_DIM = 128
GRID_W = 64
NA_ROWS = 8
NA_COLS = 16
NA_QROWS = 8
NA_KROWS = NA_QROWS + NA_ROWS
RET_DK = 128
RET_DV = 256
CHUNK = 128
ROPE_BASE = 10000.0
MOE_GROUPS = 4
MOE_EPG = 8
MOE_EXPERTS = MOE_GROUPS * MOE_EPG
MOE_TOPK = 2
ROUTER_LANES = 128
N_MIX_HEADS = 8
GM_WIDTH = N_MIX_HEADS * HEAD_DIM
NA_WIDTH = N_MIX_HEADS * HEAD_DIM
RET_QK_WIDTH = N_MIX_HEADS * RET_DK
RET_V_WIDTH = N_MIX_HEADS * RET_DV

V7X_VMEM_BYTES = 64 * 1024 * 1024
VMEM_LIMIT = V7X_VMEM_BYTES - 8 * 1024 * 1024

ROW_TILE = 512
COL_TILE = 1024
MOE_ROWS = 128
COMBINE_ROWS = 128


def _params(*sem):
    return pltpu.CompilerParams(dimension_semantics=sem, vmem_limit_bytes=VMEM_LIMIT)


def _col_offsets(d):
    off, out = 0, {}
    for name, width in (("gate_a", d), ("gate_b", d), ("gate_c", d), ("gm_u", GM_WIDTH), ("gm_v", GM_WIDTH),
                        ("na_q", NA_WIDTH), ("na_k", NA_WIDTH), ("na_v", NA_WIDTH),
                        ("ret_q", RET_QK_WIDTH), ("ret_k", RET_QK_WIDTH), ("ret_v", RET_V_WIDTH),
                        ("ret_g", RET_V_WIDTH)):
        out[name] = off
        off += width
    return out, off


def _dot(a, b):
    return jnp.dot(a, b, preferred_element_type=F32)


def _dot_nt(a, b):
    return lax.dot_general(a, b, (((1,), (1,)), ((), ())), preferred_element_type=F32)


def _norm_mod(x_ref, g_ref, sh_ref, sc_ref):
    x = x_ref[...]
    ms = jnp.mean(x * x, axis=-1, keepdims=True)
    y = x * lax.rsqrt(ms + EPS) * g_ref[...]
    return y * (1.0 + sc_ref[...]) + sh_ref[...]


def _norm_kernel(x_ref, g_ref, sh_ref, sc_ref, o_ref):
    o_ref[...] = _norm_mod(x_ref, g_ref, sh_ref, sc_ref).astype(o_ref.dtype)


def _norm_router_kernel(x_ref, g_ref, sh_ref, sc_ref, wr_ref, h_ref, lg_ref):
    h = _norm_mod(x_ref, g_ref, sh_ref, sc_ref)
    h_ref[...] = h
    lg_ref[...] = _dot(h.astype(BF16), wr_ref[...])


def _seg_map(seg_rows, tile, nseg):
    per = seg_rows // tile
    return lambda i: jnp.minimum(i // per, nseg - 1)


def norm_mod(x, g, shift, scale, rows, seg_rows, out_dtype):
    d = x.shape[1]
    seg = _seg_map(seg_rows, ROW_TILE // 2, shift.shape[0])
    tr = ROW_TILE // 2
    return pl.pallas_call(
        _norm_kernel, out_shape=SDS((rows, d), out_dtype), grid=(rows // tr,),
        in_specs=[BS((tr, d), lambda i: (i, 0)), BS((1, d), lambda i: (0, 0)),
                  BS((None, 1, d), lambda i: (seg(i), 0, 0)), BS((None, 1, d), lambda i: (seg(i), 0, 0))],
        out_specs=BS((tr, d), lambda i: (i, 0)),
        compiler_params=_params("parallel"), name="norm_mod")(x, g, shift, scale)


def norm_router(x, g, shift, scale, w_router, rows, seg_rows):
    d = x.shape[1]
    tr = ROW_TILE // 2
    seg = _seg_map(seg_rows, tr, shift.shape[0])
    return pl.pallas_call(
        _norm_router_kernel,
        out_shape=(SDS((rows, d), F32), SDS((rows, ROUTER_LANES), F32)), grid=(rows // tr,),
        in_specs=[BS((tr, d), lambda i: (i, 0)), BS((1, d), lambda i: (0, 0)),
                  BS((None, 1, d), lambda i: (seg(i), 0, 0)), BS((None, 1, d), lambda i: (seg(i), 0, 0)),
                  BS((d, ROUTER_LANES), lambda i: (0, 0))],
        out_specs=(BS((tr, d), lambda i: (i, 0)), BS((tr, ROUTER_LANES), lambda i: (i, 0))),
        compiler_params=_params("parallel"), name="norm_router")(x, g, shift, scale, w_router)


def _ada_kernel(a_ref, w_ref, b_ref, o_ref):
    o_ref[...] = _dot(a_ref[...], w_ref[...].astype(BF16)) + b_ref[...]


def ada_modulation(act, ada_w, ada_b):
    depth, d, n = ada_w.shape
    tn = ROW_TILE
    return pl.pallas_call(
        _ada_kernel, out_shape=SDS((depth, act.shape[0], n), F32), grid=(depth, n // tn),
        in_specs=[BS(act.shape, lambda l, j: (0, 0)), BS((None, d, tn), lambda l, j: (l, 0, j)),
                  BS((None, 1, tn), lambda l, j: (l, 0, j))],
        out_specs=BS((None, act.shape[0], tn), lambda l, j: (l, 0, j)),
        compiler_params=_params("parallel", "parallel"), name="ada_modulation")(
            act, ada_w, ada_b.reshape(depth, 1, n))


def _mm_kernel(a_ref, b_ref, o_ref):
    o_ref[...] = _dot(a_ref[...], b_ref[...]).astype(o_ref.dtype)


def matmul(a, b, rows, out_dtype):
    k, n = b.shape
    tm, tn = ROW_TILE, COL_TILE
    return pl.pallas_call(
        _mm_kernel, out_shape=SDS((rows, n), out_dtype), grid=(n // tn, rows // tm),
        in_specs=[BS((tm, k), lambda j, i: (i, 0)), BS((k, tn), lambda j, i: (0, j))],
        out_specs=BS((tm, tn), lambda j, i: (i, j)),
        compiler_params=_params("parallel", "parallel"), name="in_proj")(a, b)


def _mm_res_kernel(a_ref, b_ref, x_ref, g_ref, o_ref):
    o_ref[...] = x_ref[...] + g_ref[...] * _dot(a_ref[...], b_ref[...])


def matmul_residual(a, b, x, gate, rows, seg_rows):
    k, n = b.shape
    tm, tn = ROW_TILE, COL_TILE
    seg = _seg_map(seg_rows, tm, gate.shape[0])
    return pl.pallas_call(
        _mm_res_kernel, out_shape=SDS((rows, n), F32), grid=(n // tn, rows // tm),
        in_specs=[BS((tm, k), lambda j, i: (i, 0)), BS((k, tn), lambda j, i: (0, j)),
                  BS((tm, tn), lambda j, i: (i, j)), BS((None, 1, tn), lambda j, i: (seg(i), 0, j))],
        out_specs=BS((tm, tn), lambda j, i: (i, j)),
        compiler_params=_params("parallel", "parallel"), name="out_proj")(a, b, x, gate)


def _merge_kernel(a_ref, b_ref, r_ref, ga_ref, gb_ref, gc_ref, wa_ref, wb_ref, wc_ref, o_ref):
    def branch(x_ref, w_ref, g_ref):
        return jax.nn.sigmoid(g_ref[...].astype(F32)) * _dot(x_ref[...], w_ref[...])
    y = branch(a_ref, wa_ref, ga_ref) + branch(b_ref, wb_ref, gb_ref) + branch(r_ref, wc_ref, gc_ref)
    o_ref[...] = y.astype(o_ref.dtype)


def merge_branches(z, a, bb, r, w_a, w_b, w_c, rows):
    d = w_a.shape[1]
    tm, tn = ROW_TILE, COL_TILE
    nb = d // tn
    row_spec = lambda arr: BS((tm, arr.shape[1]), lambda j, i: (i, 0))
    w_spec = lambda arr: BS((arr.shape[0], tn), lambda j, i: (0, j))
    gate_spec = lambda g: BS((tm, tn), lambda j, i: (i, g * nb + j))
    return pl.pallas_call(
        _merge_kernel, out_shape=SDS((rows, d), BF16), grid=(nb, rows // tm),
        in_specs=[row_spec(a), row_spec(bb), row_spec(r), gate_spec(0), gate_spec(1), gate_spec(2),
                  w_spec(w_a), w_spec(w_b), w_spec(w_c)],
        out_specs=BS((tm, tn), lambda j, i: (i, j)),
        compiler_params=_params("parallel", "parallel"), name="merge_branches")(
            a, bb, r, z, z, z, w_a, w_b, w_c)


GM_CHUNKS_PER_STEP = 4


def _gmlp_kernel(u_ref, v_ref, ng_ref, ws_ref, bs_ref, o_ref):
    for c in range(GM_CHUNKS_PER_STEP):
        rows = slice(c * CHUNK, (c + 1) * CHUNK)
        v = jax.nn.gelu(v_ref[rows, :].astype(F32))
        mu = jnp.mean(v, axis=-1, keepdims=True)
        var = jnp.mean(jnp.square(v - mu), axis=-1, keepdims=True)
        vn = ((v - mu) * lax.rsqrt(var + EPS) * ng_ref[...]).astype(BF16)
        for g in range(N_MIX_HEADS):
            cols = slice(g * HEAD_DIM, (g + 1) * HEAD_DIM)
            mixed = _dot(ws_ref[g], vn[:, cols]) + bs_ref[g]
            u = jax.nn.gelu(u_ref[rows, cols].astype(F32))
            o_ref[rows, cols] = (u * mixed).astype(o_ref.dtype)


def chunk_gmlp(z, cols, norm_g, ws, bs, rows):
    tr = GM_CHUNKS_PER_STEP * CHUNK
    ub, vb = cols["gm_u"] // GM_WIDTH, cols["gm_v"] // GM_WIDTH
    bs_b = jnp.broadcast_to(bs[:, :, None], bs.shape + (HEAD_DIM,)).astype(F32)
    return pl.pallas_call(
        _gmlp_kernel, out_shape=SDS((rows, GM_WIDTH), BF16), grid=(rows // tr,),
        in_specs=[BS((tr, GM_WIDTH), lambda i: (i, ub)), BS((tr, GM_WIDTH), lambda i: (i, vb)),
                  BS((1, GM_WIDTH), lambda i: (0, 0)), BS(ws.shape, lambda i: (0, 0, 0)),
                  BS(bs_b.shape, lambda i: (0, 0, 0))],
        out_specs=BS((tr, GM_WIDTH), lambda i: (i, 0)),
        compiler_params=_params("parallel"), name="chunk_gmlp")(
            z, z, norm_g.reshape(1, GM_WIDTH), ws.astype(BF16), bs_b)


def _na_bias_tables(rpb, grid_rows):
    nq = grid_rows // NA_QROWS
    tables = []
    i = np.arange(NA_QROWS)[:, None, None, None]
    c = np.arange(GRID_W)[None, :, None, None]
    m = np.arange(NA_KROWS)[None, None, :, None]
    w = np.arange(GRID_W)[None, None, None, :]
    for j in (0, 1, nq - 1):
        base = int(np.clip(NA_QROWS * j - NA_ROWS // 2, 0, grid_rows - NA_KROWS))
        r = NA_QROWS * j + i
        kr = base + m
        r_start = np.clip(r - NA_ROWS // 2, 0, grid_rows - NA_ROWS)
        row_ok = (kr >= r_start) & (kr < r_start + NA_ROWS)
        c_start = np.clip(c - NA_COLS // 2, 0, GRID_W - NA_COLS)
        col_ok = (w >= c_start) & (w < c_start + NA_COLS)
        ok = np.broadcast_to(row_ok & col_ok, (NA_QROWS, GRID_W, NA_KROWS, GRID_W))
        dr = np.broadcast_to(np.clip(kr - r + NA_ROWS - 1, 0, 2 * NA_ROWS - 2), ok.shape)
        dc = np.broadcast_to(np.clip(w - c + NA_COLS - 1, 0, 2 * NA_COLS - 2), ok.shape)
        shape2 = (NA_QROWS * GRID_W, NA_KROWS * GRID_W)
        bias = rpb[:, dr.reshape(shape2), dc.reshape(shape2)].astype(F32)
        tables.append(jnp.where(ok.reshape(shape2)[None], bias, NEG_INF))
    return jnp.stack(tables)


def _na_kernel(q_ref, k_ref, v_ref, kc_ref, vc_ref, bias_ref, o_ref, *, grid_rows):
    j = pl.program_id(2)
    nq = grid_rows // NA_QROWS
    kind = jnp.where(j == 0, 0, jnp.where(j == nq - 1, 2, 1))
    base_row = jnp.clip(NA_QROWS * j - NA_ROWS // 2, 0, grid_rows - NA_KROWS)
    base = pl.multiple_of(base_row * GRID_W, (NA_ROWS // 2) * GRID_W)
    nk = NA_KROWS * GRID_W
    q = (q_ref[...].astype(F32) * (HEAD_DIM ** -0.5)).astype(BF16)
    s_loc = _dot_nt(q, k_ref[pl.ds(base, nk), :]) + bias_ref[kind]
    s_ctx = _dot_nt(q, kc_ref[...])
    mx = jnp.maximum(jnp.max(s_loc, axis=-1, keepdims=True), jnp.max(s_ctx, axis=-1, keepdims=True))
    e_loc = jnp.exp(s_loc - mx)
    e_ctx = jnp.exp(s_ctx - mx)
    den = jnp.sum(e_loc, axis=-1, keepdims=True) + jnp.sum(e_ctx, axis=-1, keepdims=True)
    o = _dot(e_loc.astype(BF16), v_ref[pl.ds(base, nk), :]) + _dot(e_ctx.astype(BF16), vc_ref[...])
    o_ref[...] = (o / den).astype(o_ref.dtype)


def neighbourhood_attention(z, cols, rpb, batch, seq, n_ctx):
    grid_rows = seq // GRID_W
    assert grid_rows % NA_QROWS == 0 and grid_rows >= 2 * NA_KROWS
    nq = grid_rows // NA_QROWS
    qb = NA_QROWS * GRID_W
    qc, kc, vc = (cols[n] // HEAD_DIM for n in ("na_q", "na_k", "na_v"))
    ctx_blk0 = batch * seq // n_ctx
    bias = _na_bias_tables(rpb, grid_rows)
    return pl.pallas_call(
        functools.partial(_na_kernel, grid_rows=grid_rows),
        out_shape=SDS((batch * seq, NA_WIDTH), BF16), grid=(N_MIX_HEADS, batch, nq),
        in_specs=[BS((qb, HEAD_DIM), lambda h, b, j: (b * nq + j, qc + h)),
                  BS((seq, HEAD_DIM), lambda h, b, j: (b, kc + h)),
                  BS((seq, HEAD_DIM), lambda h, b, j: (b, vc + h)),
                  BS((n_ctx, HEAD_DIM), lambda h, b, j: (ctx_blk0 + b, kc + h)),
                  BS((n_ctx, HEAD_DIM), lambda h, b, j: (ctx_blk0 + b, vc + h)),
                  BS((3, None, qb, NA_KROWS * GRID_W), lambda h, b, j: (0, h, 0, 0))],
        out_specs=BS((qb, HEAD_DIM), lambda h, b, j: (b * nq + j, h)),
        compiler_params=_params("parallel", "parallel", "parallel"), name="neighbourhood_attention")(
            z, z, z, z, z, bias)


def _ctx_attn_kernel(q_ref, k_ref, v_ref, o_ref):
    q = (q_ref[...].astype(F32) * (HEAD_DIM ** -0.5)).astype(BF16)
    s = _dot_nt(q, k_ref[...])
    e = jnp.exp(s - jnp.max(s, axis=-1, keepdims=True))
    o = _dot(e.astype(BF16), v_ref[...]) / jnp.sum(e, axis=-1, keepdims=True)
    o_ref[...] = o.astype(o_ref.dtype)


def context_attention(z, cols, batch, seq, n_ctx):
    qc, kc, vc = (cols[n] // HEAD_DIM for n in ("na_q", "na_k", "na_v"))
    blk0 = batch * seq // n_ctx
    spec = lambda col: BS((n_ctx, HEAD_DIM), lambda b, h: (blk0 + b, col + h))
    return pl.pallas_call(
        _ctx_attn_kernel, out_shape=SDS((batch * n_ctx, NA_WIDTH), BF16), grid=(batch, N_MIX_HEADS),
        in_specs=[spec(qc), spec(kc), spec(vc)],
        out_specs=BS((n_ctx, HEAD_DIM), lambda b, h: (b, h)),
        compiler_params=_params("parallel", "parallel"), name="context_attention")(z, z, z)


def _rope(t_ref, cos_ref, sin_ref):
    t = t_ref[...].astype(F32)
    lane = lax.broadcasted_iota(jnp.int32, t.shape, 1)
    quarter = HEAD_DIM // 4
    partner = jnp.where((lane % (2 * quarter)) < quarter,
                        pltpu.roll(t, HEAD_DIM - quarter, 1), pltpu.roll(t, quarter, 1))
    return t * cos_ref[...] + partner * sin_ref[...]


def _rope_tables(n, identity):
    if identity:
        return jnp.ones((n, HEAD_DIM), F32), jnp.zeros((n, HEAD_DIM), F32)
    nf = HEAD_DIM // 4
    pos = jnp.arange(n)
    p_row = (pos // GRID_W).astype(F32)
    p_col = (pos % GRID_W).astype(F32)
    inv = ROPE_BASE ** (-jnp.arange(nf, dtype=F32) / nf)
    a_row = p_row[:, None] * inv[None, :]
    a_col = p_col[:, None] * inv[None, :]
    cos = jnp.concatenate([jnp.cos(a_row), jnp.cos(a_row), jnp.cos(a_col), jnp.cos(a_col)], axis=-1)
    sin = jnp.concatenate([-jnp.sin(a_row), jnp.sin(a_row), -jnp.sin(a_col), jnp.sin(a_col)], axis=-1)
    return cos, sin


def _decay_tables(dec_f, dec_b):
    log_gf = jnp.log1p(-jnp.exp2(dec_f.astype(F32)))
    log_gb = jnp.log1p(-jnp.exp2(dec_b.astype(F32)))
    idx = jnp.arange(CHUNK, dtype=F32)
    rel = idx[:, None] - idx[None, :]
    d_f = jnp.where(rel >= 0, jnp.exp(log_gf[:, None, None] * jnp.maximum(rel, 0.0)), 0.0)
    d_b = jnp.where(rel < 0, jnp.exp(log_gb[:, None, None] * jnp.maximum(-rel, 0.0)), 0.0)
    bcast = lambda v: jnp.broadcast_to(v[:, :, None], v.shape + (RET_DK,))
    return dict(
        log_gf=log_gf, log_gb=log_gb,
        dfb=d_f + d_b,
        qdf=bcast(jnp.exp(log_gf[:, None] * (idx + 1.0))),
        qdb=bcast(jnp.exp(log_gb[:, None] * (CHUNK - idx))),
        kdf=bcast(jnp.exp(log_gf[:, None] * (CHUNK - 1.0 - idx))),
        kdb=bcast(jnp.exp(log_gb[:, None] * idx)),
        cdf=jnp.broadcast_to(jnp.exp(log_gf * CHUNK)[:, None, None], (log_gf.shape[0], 1, RET_DV)),
        cdb=jnp.broadcast_to(jnp.exp(log_gb * CHUNK)[:, None, None], (log_gb.shape[0], 1, RET_DV)),
    )


def _ctx_state_kernel(k_ref, v_ref, wf_ref, wb_ref, sf_ref, sb_ref):
    k = k_ref[...].astype(F32) * (RET_DK ** -0.5)
    v = v_ref[...]
    sf_ref[...] = _dot((k * wf_ref[...]).T.astype(BF16), v)
    sb_ref[...] = _dot((k * wb_ref[...]).T.astype(BF16), v)


def context_state(z, cols, tabs, batch, seq, n_ctx):
    pos = jnp.arange(n_ctx, dtype=F32)
    bcast = lambda v: jnp.broadcast_to(v[:, :, None], v.shape + (RET_DK,))
    w_f = bcast(jnp.exp(tabs["log_gf"][:, None] * ((n_ctx - 1.0) - pos)))
    w_b = bcast(jnp.exp(tabs["log_gb"][:, None] * pos))
    kc, vc = cols["ret_k"] // RET_DK, cols["ret_v"] // RET_DV
    blk0 = batch * seq // n_ctx
    out = SDS((batch, N_MIX_HEADS, RET_DK, RET_DV), F32)
    tab_spec = BS((None, n_ctx, RET_DK), lambda b, h: (h, 0, 0))
    out_spec = BS((None, None, RET_DK, RET_DV), lambda b, h: (b, h, 0, 0))
    return pl.pallas_call(
        _ctx_state_kernel, out_shape=(out, out), grid=(batch, N_MIX_HEADS),
        in_specs=[BS((n_ctx, RET_DK), lambda b, h: (blk0 + b, kc + h)),
                  BS((n_ctx, RET_DV), lambda b, h: (blk0 + b, vc + h)), tab_spec, tab_spec],
        out_specs=(out_spec, out_spec),
        compiler_params=_params("parallel", "parallel"), name="context_state")(z, z, w_f, w_b)


def _ret_state_kernel(kf_ref, vf_ref, cosf_ref, sinf_ref, kb_ref, vb_ref, cosb_ref, sinb_ref,
                      kdf_ref, kdb_ref, cdf_ref, cdb_ref, s0f_ref, s0b_ref, sf_ref, sb_ref, st_ref, *, cpg):
    @pl.when(pl.program_id(2) == 0)
    def _():
        st_ref[0] = s0f_ref[...]
        st_ref[1] = s0b_ref[...]

    def scan(k_ref, v_ref, cos_ref, sin_ref, kd_ref, cd_ref, out_ref, slot, order):
        kr = _rope(k_ref, cos_ref, sin_ref) * (RET_DK ** -0.5)
        s = st_ref[slot]
        for c in order:
            rows = slice(c * CHUNK, (c + 1) * CHUNK)
            out_ref[c] = s.astype(out_ref.dtype)
            kc = (kr[rows, :] * kd_ref[...]).T.astype(BF16)
            s = s * cd_ref[...] + _dot(kc, v_ref[rows, :])
        st_ref[slot] = s

    scan(kf_ref, vf_ref, cosf_ref, sinf_ref, kdf_ref, cdf_ref, sf_ref, 0, range(cpg))
    scan(kb_ref, vb_ref, cosb_ref, sinb_ref, kdb_ref, cdb_ref, sb_ref, 1, reversed(range(cpg)))


def _ret_out_kernel(q_ref, k_ref, v_ref, g_ref, cos_ref, sin_ref, sf_ref, sb_ref, dfb_ref, qdf_ref, qdb_ref,
                    o_ref, *, cpg):
    qr = _rope(q_ref, cos_ref, sin_ref)
    kr = (_rope(k_ref, cos_ref, sin_ref) * (RET_DK ** -0.5)).astype(BF16)
    for c in range(cpg):
        rows = slice(c * CHUNK, (c + 1) * CHUNK)
        qc = qr[rows, :]
        p = (_dot_nt(qc.astype(BF16), kr[rows, :]) * dfb_ref[...]).astype(BF16)
        o = (_dot(p, v_ref[rows, :]) + _dot((qc * qdf_ref[...]).astype(BF16), sf_ref[c])
             + _dot((qc * qdb_ref[...]).astype(BF16), sb_ref[c]))
        mu = jnp.mean(o, axis=-1, keepdims=True)
        var = jnp.mean(jnp.square(o - mu), axis=-1, keepdims=True)
        on = (o - mu) * lax.rsqrt(var + EPS)
        g = g_ref[rows, :].astype(F32)
        o_ref[rows, :] = (on * (g * jax.nn.sigmoid(g))).astype(o_ref.dtype)


def retention(z, cols, tabs, s0f, s0b, row0, n, batch, rope_identity):
    nc = n // CHUNK
    cpg = min(8, nc)
    ng = nc // cpg
    gr = cpg * CHUNK
    blk0 = row0 // gr
    qc, kc = cols["ret_q"] // RET_DK, cols["ret_k"] // RET_DK
    vc, gc = cols["ret_v"] // RET_DV, cols["ret_g"] // RET_DV
    cos, sin = _rope_tables(n, rope_identity)
    fwd = lambda b, h, g: blk0 + b * ng + g
    bwd = lambda b, h, g: blk0 + b * ng + (ng - 1 - g)
    head_tab = lambda w: BS((None, CHUNK, w), lambda b, h, g: (h, 0, 0))
    c_tab = BS((None, 1, RET_DV), lambda b, h, g: (h, 0, 0))
    s0_spec = BS((None, None, RET_DK, RET_DV), lambda b, h, g: (b, h, 0, 0))
    st_shape = SDS((batch, N_MIX_HEADS, nc, RET_DK, RET_DV), BF16)
    st_spec = lambda rev: BS((None, None, cpg, RET_DK, RET_DV),
                             lambda b, h, g: (b, h, (ng - 1 - g) if rev else g, 0, 0))
    sf, sb = pl.pallas_call(
        functools.partial(_ret_state_kernel, cpg=cpg), out_shape=(st_shape, st_shape),
        grid=(batch, N_MIX_HEADS, ng),
        in_specs=[BS((gr, RET_DK), lambda b, h, g: (fwd(b, h, g), kc + h)),
                  BS((gr, RET_DV), lambda b, h, g: (fwd(b, h, g), vc + h)),
                  BS((gr, RET_DK), lambda b, h, g: (g, 0)), BS((gr, RET_DK), lambda b, h, g: (g, 0)),
                  BS((gr, RET_DK), lambda b, h, g: (bwd(b, h, g), kc + h)),
                  BS((gr, RET_DV), lambda b, h, g: (bwd(b, h, g), vc + h)),
                  BS((gr, RET_DK), lambda b, h, g: (ng - 1 - g, 0)),
                  BS((gr, RET_DK), lambda b, h, g: (ng - 1 - g, 0)),
                  head_tab(RET_DK), head_tab(RET_DK), c_tab, c_tab, s0_spec, s0_spec],
        out_specs=(st_spec(False), st_spec(True)),
        scratch_shapes=[pltpu.VMEM((2, RET_DK, RET_DV), F32)],
        compiler_params=_params("parallel", "parallel", "arbitrary"), name="retention_state")(
            z, z, cos, sin, z, z, cos, sin, tabs["kdf"], tabs["kdb"], tabs["cdf"], tabs["cdb"], s0f, s0b)
    return pl.pallas_call(
        functools.partial(_ret_out_kernel, cpg=cpg), out_shape=SDS((batch * n, RET_V_WIDTH), BF16),
        grid=(batch, N_MIX_HEADS, ng),
        in_specs=[BS((gr, RET_DK), lambda b, h, g: (fwd(b, h, g), qc + h)),
                  BS((gr, RET_DK), lambda b, h, g: (fwd(b, h, g), kc + h)),
                  BS((gr, RET_DV), lambda b, h, g: (fwd(b, h, g), vc + h)),
                  BS((gr, RET_DV), lambda b, h, g: (fwd(b, h, g), gc + h)),
                  BS((gr, RET_DK), lambda b, h, g: (g, 0)), BS((gr, RET_DK), lambda b, h, g: (g, 0)),
                  st_spec(False), st_spec(False),
                  head_tab(CHUNK), head_tab(RET_DK), head_tab(RET_DK)],
        out_specs=BS((gr, RET_DV), lambda b, h, g: (b * ng + g, h)),
        compiler_params=_params("parallel", "parallel", "parallel"), name="retention_out")(
            z, z, z, z, cos, sin, sf, sb, tabs["dfb"], tabs["qdf"], tabs["qdb"])


def _route(logits):
    t = logits.shape[0]
    p_grp = jax.nn.softmax(logits[:, :MOE_GROUPS], axis=-1)
    grp = jnp.argmax(p_grp, axis=-1)
    p_sel = jnp.take_along_axis(p_grp, grp[:, None], axis=-1)
    le = logits[:, MOE_GROUPS:MOE_GROUPS + MOE_EXPERTS].reshape(t, MOE_GROUPS, MOE_EPG)
    le = jnp.take_along_axis(le, grp[:, None, None], axis=1)[:, 0]
    top_w, top_i = lax.top_k(jax.nn.softmax(le, axis=-1), MOE_TOPK)
    weight = p_sel * top_w / jnp.sum(top_w, axis=-1, keepdims=True)
    eid = grp[:, None] * MOE_EPG + top_i
    return eid.astype(jnp.int32), weight


def _dispatch_tables(eid, weight):
    t = eid.shape[0]
    n_assign = t * MOE_TOPK
    flat_e = eid.reshape(-1)
    flat_t = jnp.repeat(jnp.arange(t, dtype=jnp.int32), MOE_TOPK)
    onehot = (flat_e[:, None] == jnp.arange(MOE_EXPERTS, dtype=jnp.int32)[None, :]).astype(jnp.int32)
    csum = jnp.cumsum(onehot, axis=0)
    rank = jnp.take_along_axis(csum, flat_e[:, None], axis=1)[:, 0] - 1
    counts = csum[-1]
    padded = (counts + MOE_ROWS - 1) // MOE_ROWS * MOE_ROWS
    p_end = jnp.cumsum(padded)
    dest = (p_end - padded)[flat_e] + rank
    n_blocks = (n_assign + MOE_EXPERTS * (MOE_ROWS - 1)) // MOE_ROWS
    n_rows = n_blocks * MOE_ROWS
    row_tok = jnp.zeros((n_rows,), jnp.int32).at[dest].set(flat_t)
    row_w = jnp.zeros((n_rows,), F32).at[dest].set(weight.reshape(-1))
    block_e = jnp.clip(jnp.searchsorted(p_end, jnp.arange(n_blocks) * MOE_ROWS, side="right"),
                       0, MOE_EXPERTS - 1).astype(jnp.int32)
    n_used = (p_end[-1] // MOE_ROWS).astype(jnp.int32).reshape(1)
    return (block_e, n_used, row_tok.reshape(n_blocks, 1, MOE_ROWS), row_w.reshape(n_rows, 1),
            dest.reshape(t, MOE_TOPK).astype(jnp.int32))


def _moe_kernel(be_ref, nu_ref, tok_ref, h_hbm, w1_ref, w3_ref, w2_ref, rw_ref, y_ref, xb, sem):
    i = pl.program_id(0)

    @pl.when(i < nu_ref[0])
    def _():
        def issue(r, carry):
            pltpu.make_async_copy(h_hbm.at[pl.ds(tok_ref[0, r], 1)], xb.at[pl.ds(r, 1)], sem).start()
            return carry
        lax.fori_loop(0, MOE_ROWS, issue, 0)
        pltpu.make_async_copy(h_hbm.at[pl.ds(0, MOE_ROWS)], xb, sem).wait()
        x = xb[...].astype(BF16)
        h1 = _dot(x, w1_ref[...])
        h3 = _dot(x, w3_ref[...])
        hm = (h1 * jax.nn.sigmoid(h1) * h3).astype(BF16)
        y_ref[...] = _dot(hm, w2_ref[...]) * rw_ref[...]

    @pl.when(i >= nu_ref[0])
    def _():
        y_ref[...] = jnp.zeros_like(y_ref)


def moe_experts(h2, tables, w1, w3, w2):
    block_e, n_used, row_tok, row_w, _ = tables
    n_blocks = row_tok.shape[0]
    d, hid = w1.shape[1], w1.shape[2]
    grid_spec = pltpu.PrefetchScalarGridSpec(
        num_scalar_prefetch=2, grid=(n_blocks,),
        in_specs=[BS((None, 1, MOE_ROWS), lambda i, be, nu: (i, 0, 0), memory_space=pltpu.SMEM),
                  BS(memory_space=pl.ANY),
                  BS((None, d, hid), lambda i, be, nu: (be[i], 0, 0)),
                  BS((None, d, hid), lambda i, be, nu: (be[i], 0, 0)),
                  BS((None, hid, d), lambda i, be, nu: (be[i], 0, 0)),
                  BS((MOE_ROWS, 1), lambda i, be, nu: (i, 0))],
        out_specs=BS((MOE_ROWS, d), lambda i, be, nu: (i, 0)),
        scratch_shapes=[pltpu.VMEM((MOE_ROWS, d), F32), pltpu.SemaphoreType.DMA(())])
    return pl.pallas_call(
        _moe_kernel, out_shape=SDS((n_blocks * MOE_ROWS, d), F32), grid_spec=grid_spec,
        compiler_params=_params("arbitrary"), name="moe_experts")(
            block_e, n_used, row_tok, h2, w1, w3, w2, row_w)


def _combine_kernel(pos_ref, y_hbm, x_ref, g_ref, o_ref, ya, yb, sem):
    def issue(r, carry):
        pltpu.make_async_copy(y_hbm.at[pl.ds(pos_ref[0, 2 * r], 1)], ya.at[pl.ds(r, 1)], sem).start()
        pltpu.make_async_copy(y_hbm.at[pl.ds(pos_ref[0, 2 * r + 1], 1)], yb.at[pl.ds(r, 1)], sem).start()
        return carry
    lax.fori_loop(0, COMBINE_ROWS, issue, 0)
    pltpu.make_async_copy(y_hbm.at[pl.ds(0, COMBINE_ROWS)], ya, sem).wait()
    pltpu.make_async_copy(y_hbm.at[pl.ds(0, COMBINE_ROWS)], yb, sem).wait()
    o_ref[...] = x_ref[...] + g_ref[...] * (ya[...] + yb[...])


def moe_combine(x1, y, pos, gate, rows, seg_rows):
    d = x1.shape[1]
    tt = COMBINE_ROWS
    seg = _seg_map(seg_rows, tt, gate.shape[0])
    pos3 = pos.reshape(rows // tt, 1, MOE_TOPK * tt)
    return pl.pallas_call(
        _combine_kernel, out_shape=SDS((rows, d), F32), grid=(rows // tt,),
        in_specs=[BS((None, 1, MOE_TOPK * tt), lambda i: (i, 0, 0), memory_space=pltpu.SMEM),
                  BS(memory_space=pl.ANY), BS((tt, d), lambda i: (i, 0)),
                  BS((None, 1, d), lambda i: (seg(i), 0, 0))],
        out_specs=BS((tt, d), lambda i: (i, 0)),
        scratch_shapes=[pltpu.VMEM((tt, d), F32), pltpu.VMEM((tt, d), F32), pltpu.SemaphoreType.DMA(())],
        compiler_params=_params("arbitrary"), name="moe_combine")(pos3, y, x1, gate)


def kernel(x, c, ctx, c_ctx, ada_w, ada_b, norm1_g, w_in, gm_norm_g, gm_ws, gm_bs, na_rpb, ret_decay_fwd,
           ret_decay_bwd, w_branch_a, w_branch_b, w_branch_c, w_out, norm2_g, moe_w_group, moe_w_expert,
           moe_w1, moe_w3, moe_w2, final_norm_g):
    batch, seq, d = x.shape
    n_ctx = ctx.shape[1]
    depth = ada_w.shape[0]
    t_lat, t_ctx = batch * seq, batch * n_ctx
    t_all = t_lat + t_ctx
    cols, in_total = _col_offsets(d)
    assert w_in.shape[2] == in_total and seq % ROW_TILE == 0 and t_ctx % ROW_TILE == 0
    assert 8 >= batch + 1

    cond = jnp.concatenate([c, c_ctx[None, :], jnp.zeros((8 - batch - 1, d), c.dtype)], axis=0)
    mod = ada_modulation(jax.nn.silu(cond).astype(BF16), ada_w, ada_b)[:, :batch + 1]
    mod = mod.reshape(depth, batch + 1, 6, 1, d)

    xs = jnp.concatenate([x.reshape(t_lat, d), ctx.reshape(t_ctx, d)], axis=0)
    for l in range(depth):
        need_ctx = l < depth - 1
        rows = t_all if need_ctx else t_lat
        sh1, sc1, g1, sh2, sc2, g2 = (mod[l, :, k] for k in range(6))

        h = norm_mod(xs, norm1_g[l].reshape(1, d), sh1, sc1, t_all, seq, BF16)
        z = matmul(h, w_in[l].astype(BF16), t_all, BF16)

        a = chunk_gmlp(z, cols, gm_norm_g[l], gm_ws[l], gm_bs[l], rows)
        bb = neighbourhood_attention(z, cols, na_rpb[l], batch, seq, n_ctx)
        tabs = _decay_tables(ret_decay_fwd[l], ret_decay_bwd[l])
        s0f, s0b = context_state(z, cols, tabs, batch, seq, n_ctx)
        r = retention(z, cols, tabs, s0f, s0b, 0, seq, batch, False)
        if need_ctx:
            bb = jnp.concatenate([bb, context_attention(z, cols, batch, seq, n_ctx)], axis=0)
            zeros = jnp.zeros_like(s0f)
            r = jnp.concatenate([r, retention(z, cols, tabs, zeros, zeros, t_lat, n_ctx, batch, True)], axis=0)

        y = merge_branches(z, a, bb, r, w_branch_a[l].astype(BF16), w_branch_b[l].astype(BF16),
                           w_branch_c[l].astype(BF16), rows)
        x1 = matmul_residual(y, w_out[l].astype(BF16), xs, g1, rows, seq)

        w_router = jnp.concatenate(
            [moe_w_group[l], moe_w_expert[l],
             jnp.zeros((d, ROUTER_LANES - MOE_GROUPS - MOE_EXPERTS), F32)], axis=1).astype(BF16)
        h2, logits = norm_router(x1, norm2_g[l].reshape(1, d), sh2, sc2, w_router, rows, seq)
        eid, weight = _route(logits)
        tables = _dispatch_tables(eid, weight)
        y_moe = moe_experts(h2, tables, moe_w1[l].astype(BF16), moe_w3[l].astype(BF16), moe_w2[l].astype(BF16))
        xs = moe_combine(x1, y_moe, tables[4], g2, rows, seq)

    zero = jnp.zeros((1, 1, d), F32)
    out = norm_mod(xs, final_norm_g.reshape(1, d), zero, zero, t_lat, seq, F32)
    return out.reshape(batch, seq, d)
```

```python
import functools

import jax
import jax.numpy as jnp
import numpy as np
from jax import lax
from jax.experimental import pallas as pl
from jax.experimental.pallas import tpu as pltpu

F32 = jnp.float32
BF16 = jnp.bfloat16
SDS = jax.ShapeDtypeStruct
BS = pl.BlockSpec

EPS = 1e-6
NEG_INF = -1e30
HEAD_DIM = 128
GRID_W = 64
NA_ROWS = 8
NA_COLS = 16
NA_QROWS = 8
NA_KROWS = NA_QROWS + NA_ROWS
RET_DK = 128
RET_DV = 256
CHUNK = 128
ROPE_BASE = 10000.0
MOE_GROUPS = 4
MOE_EPG = 8
MOE_EXPERTS = MOE_GROUPS * MOE_EPG
MOE_TOPK = 2
ROUTER_LANES = 128
N_MIX_HEADS = 8
GM_WIDTH = N_MIX_HEADS * HEAD_DIM
NA_WIDTH = N_MIX_HEADS * HEAD_DIM
RET_QK_WIDTH = N_MIX_HEADS * RET_DK
RET_V_WIDTH = N_MIX_HEADS * RET_DV

V7X_VMEM_BYTES = 64 * 1024 * 1024
VMEM_LIMIT = V7X_VMEM_BYTES - 8 * 1024 * 1024

ROW_TILE = 512
COL_TILE = 1024
MOE_ROWS = 256
COMBINE_ROWS = 128


def _params(*sem):
    return pltpu.CompilerParams(dimension_semantics=sem, vmem_limit_bytes=VMEM_LIMIT)


def _col_offsets(d):
    off, out = 0, {}
    for name, width in (("gate_a", d), ("gate_b", d), ("gate_c", d), ("gm_u", GM_WIDTH), ("gm_v", GM_WIDTH),
                        ("na_q", NA_WIDTH), ("na_k", NA_WIDTH), ("na_v", NA_WIDTH),
                        ("ret_q", RET_QK_WIDTH), ("ret_k", RET_QK_WIDTH), ("ret_v", RET_V_WIDTH),
                        ("ret_g", RET_V_WIDTH)):
        out[name] = off
        off += width
    return out, off


def _dot(a, b):
    return jnp.dot(a, b, preferred_element_type=F32)


def _dot_nt(a, b):
    return lax.dot_general(a, b, (((1,), (1,)), ((), ())), preferred_element_type=F32)


def _norm_mod(x_ref, g_ref, sh_ref, sc_ref):
    x = x_ref[...]
    ms = jnp.mean(x * x, axis=-1, keepdims=True)
    y = x * lax.rsqrt(ms + EPS) * g_ref[...]
    return y * (1.0 + sc_ref[...]) + sh_ref[...]


def _norm_kernel(x_ref, g_ref, sh_ref, sc_ref, o_ref):
    o_ref[...] = _norm_mod(x_ref, g_ref, sh_ref, sc_ref).astype(o_ref.dtype)


def _norm_router_kernel(x_ref, g_ref, sh_ref, sc_ref, wr_ref, h_ref, lg_ref):
    h = _norm_mod(x_ref, g_ref, sh_ref, sc_ref)
    h_ref[...] = h
    lg_ref[...] = _dot(h.astype(BF16), wr_ref[...])


def _seg_map(seg_rows, tile, nseg):
    per = seg_rows // tile
    return lambda i: jnp.minimum(i // per, nseg - 1)


def norm_mod(x, g, shift, scale, rows, seg_rows, out_dtype):
    d = x.shape[1]
    seg = _seg_map(seg_rows, ROW_TILE // 2, shift.shape[0])
    tr = ROW_TILE // 2
    return pl.pallas_call(
        _norm_kernel, out_shape=SDS((rows, d), out_dtype), grid=(rows // tr,),
        in_specs=[BS((tr, d), lambda i: (i, 0)), BS((1, d), lambda i: (0, 0)),
                  BS((None, 1, d), lambda i: (seg(i), 0, 0)), BS((None, 1, d), lambda i: (seg(i), 0, 0))],
        out_specs=BS((tr, d), lambda i: (i, 0)),
        compiler_params=_params("parallel"), name="norm_mod")(x, g, shift, scale)


def norm_router(x, g, shift, scale, w_router, rows, seg_rows):
    d = x.shape[1]
    tr = ROW_TILE // 2
    seg = _seg_map(seg_rows, tr, shift.shape[0])
    return pl.pallas_call(
        _norm_router_kernel,
        out_shape=(SDS((rows, d), F32), SDS((rows, ROUTER_LANES), F32)), grid=(rows // tr,),
        in_specs=[BS((tr, d), lambda i: (i, 0)), BS((1, d), lambda i: (0, 0)),
                  BS((None, 1, d), lambda i: (seg(i), 0, 0)), BS((None, 1, d), lambda i: (seg(i), 0, 0)),
                  BS((d, ROUTER_LANES), lambda i: (0, 0))],
        out_specs=(BS((tr, d), lambda i: (i, 0)), BS((tr, ROUTER_LANES), lambda i: (i, 0))),
        compiler_params=_params("parallel"), name="norm_router")(x, g, shift, scale, w_router)


def _ada_kernel(a_ref, w_ref, b_ref, o_ref):
    o_ref[...] = _dot(a_ref[...], w_ref[...].astype(BF16)) + b_ref[...]


def ada_modulation(act, ada_w, ada_b):
    depth, d, n = ada_w.shape
    tn = ROW_TILE
    return pl.pallas_call(
        _ada_kernel, out_shape=SDS((depth, act.shape[0], n), F32), grid=(depth, n // tn),
        in_specs=[BS(act.shape, lambda l, j: (0, 0)), BS((None, d, tn), lambda l, j: (l, 0, j)),
                  BS((None, 1, tn), lambda l, j: (l, 0, j))],
        out_specs=BS((None, act.shape[0], tn), lambda l, j: (l, 0, j)),
        compiler_params=_params("parallel", "parallel"), name="ada_modulation")(
            act, ada_w, ada_b.reshape(depth, 1, n))


def _mm_kernel(a_ref, b_ref, o_ref):
    o_ref[...] = _dot(a_ref[...], b_ref[...]).astype(o_ref.dtype)


def matmul(a, b, rows, out_dtype):
    k, n = b.shape
    tm, tn = ROW_TILE, COL_TILE
    return pl.pallas_call(
        _mm_kernel, out_shape=SDS((rows, n), out_dtype), grid=(n // tn, rows // tm),
        in_specs=[BS((tm, k), lambda j, i: (i, 0)), BS((k, tn), lambda j, i: (0, j))],
        out_specs=BS((tm, tn), lambda j, i: (i, j)),
        compiler_params=_params("parallel", "parallel"), name="in_proj")(a, b)


def _mm_res_kernel(a_ref, b_ref, x_ref, g_ref, o_ref):
    o_ref[...] = x_ref[...] + g_ref[...] * _dot(a_ref[...], b_ref[...])


def matmul_residual(a, b, x, gate, rows, seg_rows):
    k, n = b.shape
    tm, tn = ROW_TILE, COL_TILE
    seg = _seg_map(seg_rows, tm, gate.shape[0])
    return pl.pallas_call(
        _mm_res_kernel, out_shape=SDS((rows, n), F32), grid=(n // tn, rows // tm),
        in_specs=[BS((tm, k), lambda j, i: (i, 0)), BS((k, tn), lambda j, i: (0, j)),
                  BS((tm, tn), lambda j, i: (i, j)), BS((None, 1, tn), lambda j, i: (seg(i), 0, j))],
        out_specs=BS((tm, tn), lambda j, i: (i, j)),
        compiler_params=_params("parallel", "parallel"), name="out_proj")(a, b, x, gate)


def _merge_kernel(a_ref, b_ref, r_ref, ga_ref, gb_ref, gc_ref, wa_ref, wb_ref, wc_ref, o_ref):
    def branch(x_ref, w_ref, g_ref):
        return jax.nn.sigmoid(g_ref[...].astype(F32)) * _dot(x_ref[...], w_ref[...])
    y = branch(a_ref, wa_ref, ga_ref) + branch(b_ref, wb_ref, gb_ref) + branch(r_ref, wc_ref, gc_ref)
    o_ref[...] = y.astype(o_ref.dtype)


def merge_branches(z, a, bb, r, w_a, w_b, w_c, rows):
    d = w_a.shape[1]
    tm, tn = ROW_TILE, COL_TILE
    nb = d // tn
    row_spec = lambda arr: BS((tm, arr.shape[1]), lambda j, i: (i, 0))
    w_spec = lambda arr: BS((arr.shape[0], tn), lambda j, i: (0, j))
    gate_spec = lambda g: BS((tm, tn), lambda j, i: (i, g * nb + j))
    return pl.pallas_call(
        _merge_kernel, out_shape=SDS((rows, d), BF16), grid=(nb, rows // tm),
        in_specs=[row_spec(a), row_spec(bb), row_spec(r), gate_spec(0), gate_spec(1), gate_spec(2),
                  w_spec(w_a), w_spec(w_b), w_spec(w_c)],
        out_specs=BS((tm, tn), lambda j, i: (i, j)),
        compiler_params=_params("parallel", "parallel"), name="merge_branches")(
            a, bb, r, z, z, z, w_a, w_b, w_c)


GM_CHUNKS_PER_STEP = 4


def _gmlp_kernel(u_ref, v_ref, ng_ref, ws_ref, bs_ref, o_ref):
    for c in range(GM_CHUNKS_PER_STEP):
        rows = slice(c * CHUNK, (c + 1) * CHUNK)
        v = jax.nn.gelu(v_ref[rows, :].astype(F32))
        mu = jnp.mean(v, axis=-1, keepdims=True)
        var = jnp.mean(jnp.square(v - mu), axis=-1, keepdims=True)
        vn = ((v - mu) * lax.rsqrt(var + EPS) * ng_ref[...]).astype(BF16)
        for g in range(N_MIX_HEADS):
            cols = slice(g * HEAD_DIM, (g + 1) * HEAD_DIM)
            mixed = _dot(ws_ref[g], vn[:, cols]) + bs_ref[g]
            u = jax.nn.gelu(u_ref[rows, cols].astype(F32))
            o_ref[rows, cols] = (u * mixed).astype(o_ref.dtype)


def chunk_gmlp(z, cols, norm_g, ws, bs, rows):
    tr = GM_CHUNKS_PER_STEP * CHUNK
    ub, vb = cols["gm_u"] // GM_WIDTH, cols["gm_v"] // GM_WIDTH
    bs_b = jnp.broadcast_to(bs[:, :, None], bs.shape + (HEAD_DIM,)).astype(F32)
    return pl.pallas_call(
        _gmlp_kernel, out_shape=SDS((rows, GM_WIDTH), BF16), grid=(rows // tr,),
        in_specs=[BS((tr, GM_WIDTH), lambda i: (i, ub)), BS((tr, GM_WIDTH), lambda i: (i, vb)),
                  BS((1, GM_WIDTH), lambda i: (0, 0)), BS(ws.shape, lambda i: (0, 0, 0)),
                  BS(bs_b.shape, lambda i: (0, 0, 0))],
        out_specs=BS((tr, GM_WIDTH), lambda i: (i, 0)),
        compiler_params=_params("parallel"), name="chunk_gmlp")(
            z, z, norm_g.reshape(1, GM_WIDTH), ws.astype(BF16), bs_b)


def _na_bias_tables(rpb, grid_rows):
    nq = grid_rows // NA_QROWS
    n_heads = rpb.shape[0]
    tables = []
    i = np.arange(NA_QROWS)[:, None]
    m = np.arange(NA_KROWS)[None, :]
    c = np.arange(GRID_W)[:, None]
    w = np.arange(GRID_W)[None, :]
    c_start = np.clip(c - NA_COLS // 2, 0, GRID_W - NA_COLS)
    col_ok = (w >= c_start) & (w < c_start + NA_COLS)
    dc = np.clip(w - c + NA_COLS - 1, 0, 2 * NA_COLS - 2)
    col_sel = (dc.reshape(-1)[None, :] == np.arange(2 * NA_COLS - 1)[:, None]).astype(np.float32)
    shape2 = (NA_QROWS * GRID_W, NA_KROWS * GRID_W)
    for j in (0, 1, nq - 1):
        base = int(np.clip(NA_QROWS * j - NA_ROWS // 2, 0, grid_rows - NA_KROWS))
        r = NA_QROWS * j + i
        kr = base + m
        r_start = np.clip(r - NA_ROWS // 2, 0, grid_rows - NA_ROWS)
        row_ok = (kr >= r_start) & (kr < r_start + NA_ROWS)
        dr = np.clip(kr - r + NA_ROWS - 1, 0, 2 * NA_ROWS - 2)
        row_sel = (dr.reshape(-1)[:, None] == np.arange(2 * NA_ROWS - 1)[None, :]).astype(np.float32)
        bias = jnp.einsum("pa,hab,bq->hpq", row_sel, rpb.astype(F32), col_sel, precision=lax.Precision.HIGHEST)
        bias = bias.reshape(n_heads, NA_QROWS, NA_KROWS, GRID_W, GRID_W).transpose(0, 1, 3, 2, 4)
        ok = (row_ok[:, None, :, None] & col_ok[None, :, None, :]).reshape(shape2)
        tables.append(jnp.where(ok[None], bias.reshape((n_heads,) + shape2), NEG_INF))
    return jnp.stack(tables)


def _na_kernel(q_ref, k_ref, v_ref, kc_ref, vc_ref, bias_ref, o_ref, *, grid_rows):
    j = pl.program_id(2)
    nq = grid_rows // NA_QROWS
    kind = jnp.where(j == 0, 0, jnp.where(j == nq - 1, 2, 1))
    base_row = jnp.clip(NA_QROWS * j - NA_ROWS // 2, 0, grid_rows - NA_KROWS)
    base = pl.multiple_of(base_row * GRID_W, (NA_ROWS // 2) * GRID_W)
    nk = NA_KROWS * GRID_W
    q = (q_ref[...].astype(F32) * (HEAD_DIM ** -0.5)).astype(BF16)
    s_loc = _dot_nt(q, k_ref[pl.ds(base, nk), :]) + bias_ref[kind]
    s_ctx = _dot_nt(q, kc_ref[...])
    mx = jnp.maximum(jnp.max(s_loc, axis=-1, keepdims=True), jnp.max(s_ctx, axis=-1, keepdims=True))
    e_loc = jnp.exp(s_loc - mx)
    e_ctx = jnp.exp(s_ctx - mx)
    den = jnp.sum(e_loc, axis=-1, keepdims=True) + jnp.sum(e_ctx, axis=-1, keepdims=True)
    o = _dot(e_loc.astype(BF16), v_ref[pl.ds(base, nk), :]) + _dot(e_ctx.astype(BF16), vc_ref[...])
    o_ref[...] = (o / den).astype(o_ref.dtype)


def neighbourhood_attention(z, cols, rpb, batch, seq, n_ctx):
    grid_rows = seq // GRID_W
    assert grid_rows % NA_QROWS == 0 and grid_rows >= 2 * NA_KROWS
    nq = grid_rows // NA_QROWS
    qb = NA_QROWS * GRID_W
    qc, kc, vc = (cols[n] // HEAD_DIM for n in ("na_q", "na_k", "na_v"))
    ctx_blk0 = batch * seq // n_ctx
    bias = _na_bias_tables(rpb, grid_rows)
    return pl.pallas_call(
        functools.partial(_na_kernel, grid_rows=grid_rows),
        out_shape=SDS((batch * seq, NA_WIDTH), BF16), grid=(N_MIX_HEADS, batch, nq),
        in_specs=[BS((qb, HEAD_DIM), lambda h, b, j: (b * nq + j, qc + h)),
                  BS((seq, HEAD_DIM), lambda h, b, j: (b, kc + h)),
                  BS((seq, HEAD_DIM), lambda h, b, j: (b, vc + h)),
                  BS((n_ctx, HEAD_DIM), lambda h, b, j: (ctx_blk0 + b, kc + h)),
                  BS((n_ctx, HEAD_DIM), lambda h, b, j: (ctx_blk0 + b, vc + h)),
                  BS((3, None, qb, NA_KROWS * GRID_W), lambda h, b, j: (0, h, 0, 0))],
        out_specs=BS((qb, HEAD_DIM), lambda h, b, j: (b * nq + j, h)),
        compiler_params=_params("parallel", "parallel", "parallel"), name="neighbourhood_attention")(
            z, z, z, z, z, bias)


def _ctx_attn_kernel(q_ref, k_ref, v_ref, o_ref):
    q = (q_ref[...].astype(F32) * (HEAD_DIM ** -0.5)).astype(BF16)
    s = _dot_nt(q, k_ref[...])
    e = jnp.exp(s - jnp.max(s, axis=-1, keepdims=True))
    o = _dot(e.astype(BF16), v_ref[...]) / jnp.sum(e, axis=-1, keepdims=True)
    o_ref[...] = o.astype(o_ref.dtype)


def context_attention(z, cols, batch, seq, n_ctx):
    qc, kc, vc = (cols[n] // HEAD_DIM for n in ("na_q", "na_k", "na_v"))
    blk0 = batch * seq // n_ctx
    spec = lambda col: BS((n_ctx, HEAD_DIM), lambda b, h: (blk0 + b, col + h))
    return pl.pallas_call(
        _ctx_attn_kernel, out_shape=SDS((batch * n_ctx, NA_WIDTH), BF16), grid=(batch, N_MIX_HEADS),
        in_specs=[spec(qc), spec(kc), spec(vc)],
        out_specs=BS((n_ctx, HEAD_DIM), lambda b, h: (b, h)),
        compiler_params=_params("parallel", "parallel"), name="context_attention")(z, z, z)


def _rope(t_ref, cos_ref, sin_ref):
    t = t_ref[...].astype(F32)
    lane = lax.broadcasted_iota(jnp.int32, t.shape, 1)
    quarter = HEAD_DIM // 4
    partner = jnp.where((lane % (2 * quarter)) < quarter,
                        pltpu.roll(t, HEAD_DIM - quarter, 1), pltpu.roll(t, quarter, 1))
    return t * cos_ref[...] + partner * sin_ref[...]


def _rope_tables(n, identity):
    if identity:
        return jnp.ones((n, HEAD_DIM), F32), jnp.zeros((n, HEAD_DIM), F32)
    nf = HEAD_DIM // 4
    pos = jnp.arange(n)
    p_row = (pos // GRID_W).astype(F32)
    p_col = (pos % GRID_W).astype(F32)
    inv = ROPE_BASE ** (-jnp.arange(nf, dtype=F32) / nf)
    a_row = p_row[:, None] * inv[None, :]
    a_col = p_col[:, None] * inv[None, :]
    cos = jnp.concatenate([jnp.cos(a_row), jnp.cos(a_row), jnp.cos(a_col), jnp.cos(a_col)], axis=-1)
    sin = jnp.concatenate([-jnp.sin(a_row), jnp.sin(a_row), -jnp.sin(a_col), jnp.sin(a_col)], axis=-1)
    return cos, sin


def _decay_tables(dec_f, dec_b):
    log_gf = jnp.log1p(-jnp.exp2(dec_f.astype(F32)))
    log_gb = jnp.log1p(-jnp.exp2(dec_b.astype(F32)))
    idx = jnp.arange(CHUNK, dtype=F32)
    rel = idx[:, None] - idx[None, :]
    d_f = jnp.where(rel >= 0, jnp.exp(log_gf[:, None, None] * jnp.maximum(rel, 0.0)), 0.0)
    d_b = jnp.where(rel < 0, jnp.exp(log_gb[:, None, None] * jnp.maximum(-rel, 0.0)), 0.0)
    bcast = lambda v: jnp.broadcast_to(v[:, :, None], v.shape + (RET_DK,))
    return dict(
        log_gf=log_gf, log_gb=log_gb,
        dfb=d_f + d_b,
        qdf=bcast(jnp.exp(log_gf[:, None] * (idx + 1.0))),
        qdb=bcast(jnp.exp(log_gb[:, None] * (CHUNK - idx))),
        kdf=bcast(jnp.exp(log_gf[:, None] * (CHUNK - 1.0 - idx))),
        kdb=bcast(jnp.exp(log_gb[:, None] * idx)),
        cdf=jnp.broadcast_to(jnp.exp(log_gf * CHUNK)[:, None, None], (log_gf.shape[0], 1, RET_DV)),
        cdb=jnp.broadcast_to(jnp.exp(log_gb * CHUNK)[:, None, None], (log_gb.shape[0], 1, RET_DV)),
    )


def _ctx_state_kernel(k_ref, v_ref, wf_ref, wb_ref, sf_ref, sb_ref):
    k = k_ref[...].astype(F32) * (RET_DK ** -0.5)
    v = v_ref[...]
    sf_ref[...] = _dot((k * wf_ref[...]).T.astype(BF16), v)
    sb_ref[...] = _dot((k * wb_ref[...]).T.astype(BF16), v)


def context_state(z, cols, tabs, batch, seq, n_ctx):
    pos = jnp.arange(n_ctx, dtype=F32)
    bcast = lambda v: jnp.broadcast_to(v[:, :, None], v.shape + (RET_DK,))
    w_f = bcast(jnp.exp(tabs["log_gf"][:, None] * ((n_ctx - 1.0) - pos)))
    w_b = bcast(jnp.exp(tabs["log_gb"][:, None] * pos))
    kc, vc = cols["ret_k"] // RET_DK, cols["ret_v"] // RET_DV
    blk0 = batch * seq // n_ctx
    out = SDS((batch, N_MIX_HEADS, RET_DK, RET_DV), F32)
    tab_spec = BS((None, n_ctx, RET_DK), lambda b, h: (h, 0, 0))
    out_spec = BS((None, None, RET_DK, RET_DV), lambda b, h: (b, h, 0, 0))
    return pl.pallas_call(
        _ctx_state_kernel, out_shape=(out, out), grid=(batch, N_MIX_HEADS),
        in_specs=[BS((n_ctx, RET_DK), lambda b, h: (blk0 + b, kc + h)),
                  BS((n_ctx, RET_DV), lambda b, h: (blk0 + b, vc + h)), tab_spec, tab_spec],
        out_specs=(out_spec, out_spec),
        compiler_params=_params("parallel", "parallel"), name="context_state")(z, z, w_f, w_b)


def _ret_state_kernel(kf_ref, vf_ref, cosf_ref, sinf_ref, kb_ref, vb_ref, cosb_ref, sinb_ref,
                      kdf_ref, kdb_ref, cdf_ref, cdb_ref, s0f_ref, s0b_ref, sf_ref, sb_ref, st_ref, *, cpg):
    @pl.when(pl.program_id(2) == 0)
    def _():
        st_ref[0] = s0f_ref[...]
        st_ref[1] = s0b_ref[...]

    def scan(k_ref, v_ref, cos_ref, sin_ref, kd_ref, cd_ref, out_ref, slot, order):
        kr = _rope(k_ref, cos_ref, sin_ref) * (RET_DK ** -0.5)
        s = st_ref[slot]
        for c in order:
            rows = slice(c * CHUNK, (c + 1) * CHUNK)
            out_ref[c] = s.astype(out_ref.dtype)
            kc = (kr[rows, :] * kd_ref[...]).T.astype(BF16)
            s = s * cd_ref[...] + _dot(kc, v_ref[rows, :])
        st_ref[slot] = s

    scan(kf_ref, vf_ref, cosf_ref, sinf_ref, kdf_ref, cdf_ref, sf_ref, 0, range(cpg))
    scan(kb_ref, vb_ref, cosb_ref, sinb_ref, kdb_ref, cdb_ref, sb_ref, 1, reversed(range(cpg)))


def _ret_out_kernel(q_ref, k_ref, v_ref, g_ref, cos_ref, sin_ref, sf_ref, sb_ref, dfb_ref, qdf_ref, qdb_ref,
                    o_ref, *, cpg):
    qr = _rope(q_ref, cos_ref, sin_ref)
    kr = (_rope(k_ref, cos_ref, sin_ref) * (RET_DK ** -0.5)).astype(BF16)
    for c in range(cpg):
        rows = slice(c * CHUNK, (c + 1) * CHUNK)
        qc = qr[rows, :]
        p = (_dot_nt(qc.astype(BF16), kr[rows, :]) * dfb_ref[...]).astype(BF16)
        o = (_dot(p, v_ref[rows, :]) + _dot((qc * qdf_ref[...]).astype(BF16), sf_ref[c])
             + _dot((qc * qdb_ref[...]).astype(BF16), sb_ref[c]))
        mu = jnp.mean(o, axis=-1, keepdims=True)
        var = jnp.mean(jnp.square(o - mu), axis=-1, keepdims=True)
        on = (o - mu) * lax.rsqrt(var + EPS)
        g = g_ref[rows, :].astype(F32)
        o_ref[rows, :] = (on * (g * jax.nn.sigmoid(g))).astype(o_ref.dtype)


def retention(z, cols, tabs, s0f, s0b, row0, n, batch, rope_identity):
    nc = n // CHUNK
    cpg = min(8, nc)
    ng = nc // cpg
    gr = cpg * CHUNK
    blk0 = row0 // gr
    qc, kc = cols["ret_q"] // RET_DK, cols["ret_k"] // RET_DK
    vc, gc = cols["ret_v"] // RET_DV, cols["ret_g"] // RET_DV
    cos, sin = _rope_tables(n, rope_identity)
    fwd = lambda b, h, g: blk0 + b * ng + g
    bwd = lambda b, h, g: blk0 + b * ng + (ng - 1 - g)
    head_tab = lambda w: BS((None, CHUNK, w), lambda b, h, g: (h, 0, 0))
    c_tab = BS((None, 1, RET_DV), lambda b, h, g: (h, 0, 0))
    s0_spec = BS((None, None, RET_DK, RET_DV), lambda b, h, g: (b, h, 0, 0))
    st_shape = SDS((batch, N_MIX_HEADS, nc, RET_DK, RET_DV), BF16)
    st_spec = lambda rev: BS((None, None, cpg, RET_DK, RET_DV),
                             lambda b, h, g: (b, h, (ng - 1 - g) if rev else g, 0, 0))
    sf, sb = pl.pallas_call(
        functools.partial(_ret_state_kernel, cpg=cpg), out_shape=(st_shape, st_shape),
        grid=(batch, N_MIX_HEADS, ng),
        in_specs=[BS((gr, RET_DK), lambda b, h, g: (fwd(b, h, g), kc + h)),
                  BS((gr, RET_DV), lambda b, h, g: (fwd(b, h, g), vc + h)),
                  BS((gr, RET_DK), lambda b, h, g: (g, 0)), BS((gr, RET_DK), lambda b, h, g: (g, 0)),
                  BS((gr, RET_DK), lambda b, h, g: (bwd(b, h, g), kc + h)),
                  BS((gr, RET_DV), lambda b, h, g: (bwd(b, h, g), vc + h)),
                  BS((gr, RET_DK), lambda b, h, g: (ng - 1 - g, 0)),
                  BS((gr, RET_DK), lambda b, h, g: (ng - 1 - g, 0)),
                  head_tab(RET_DK), head_tab(RET_DK), c_tab, c_tab, s0_spec, s0_spec],
        out_specs=(st_spec(False), st_spec(True)),
        scratch_shapes=[pltpu.VMEM((2, RET_DK, RET_DV), F32)],
        compiler_params=_params("parallel", "parallel", "arbitrary"), name="retention_state")(
            z, z, cos, sin, z, z, cos, sin, tabs["kdf"], tabs["kdb"], tabs["cdf"], tabs["cdb"], s0f, s0b)
    return pl.pallas_call(
        functools.partial(_ret_out_kernel, cpg=cpg), out_shape=SDS((batch * n, RET_V_WIDTH), BF16),
        grid=(batch, N_MIX_HEADS, ng),
        in_specs=[BS((gr, RET_DK), lambda b, h, g: (fwd(b, h, g), qc + h)),
                  BS((gr, RET_DK), lambda b, h, g: (fwd(b, h, g), kc + h)),
                  BS((gr, RET_DV), lambda b, h, g: (fwd(b, h, g), vc + h)),
                  BS((gr, RET_DV), lambda b, h, g: (fwd(b, h, g), gc + h)),
                  BS((gr, RET_DK), lambda b, h, g: (g, 0)), BS((gr, RET_DK), lambda b, h, g: (g, 0)),
                  st_spec(False), st_spec(False),
                  head_tab(CHUNK), head_tab(RET_DK), head_tab(RET_DK)],
        out_specs=BS((gr, RET_DV), lambda b, h, g: (b * ng + g, h)),
        compiler_params=_params("parallel", "parallel", "parallel"), name="retention_out")(
            z, z, z, z, cos, sin, sf, sb, tabs["dfb"], tabs["qdf"], tabs["qdb"])


def _route(logits):
    t = logits.shape[0]
    p_grp = jax.nn.softmax(logits[:, :MOE_GROUPS], axis=-1)
    grp = jnp.argmax(p_grp, axis=-1)
    p_sel = jnp.take_along_axis(p_grp, grp[:, None], axis=-1)
    le = logits[:, MOE_GROUPS:MOE_GROUPS + MOE_EXPERTS].reshape(t, MOE_GROUPS, MOE_EPG)
    le = jnp.take_along_axis(le, grp[:, None, None], axis=1)[:, 0]
    top_w, top_i = lax.top_k(jax.nn.softmax(le, axis=-1), MOE_TOPK)
    weight = p_sel * top_w / jnp.sum(top_w, axis=-1, keepdims=True)
    eid = grp[:, None] * MOE_EPG + top_i
    return eid.astype(jnp.int32), weight


def _dispatch_tables(eid):
    t = eid.shape[0]
    n_assign = t * MOE_TOPK
    flat_e = eid.reshape(-1)
    flat_t = jnp.repeat(jnp.arange(t, dtype=jnp.int32), MOE_TOPK)
    onehot = (flat_e[:, None] == jnp.arange(MOE_EXPERTS, dtype=jnp.int32)[None, :]).astype(jnp.int32)
    csum = jnp.cumsum(onehot, axis=0)
    rank = jnp.take_along_axis(csum, flat_e[:, None], axis=1)[:, 0] - 1
    counts = csum[-1]
    padded = (counts + MOE_ROWS - 1) // MOE_ROWS * MOE_ROWS
    p_end = jnp.cumsum(padded)
    dest = (p_end - padded)[flat_e] + rank
    n_blocks = (n_assign + MOE_EXPERTS * (MOE_ROWS - 1)) // MOE_ROWS
    row_tok = jnp.zeros((n_blocks * MOE_ROWS,), jnp.int32).at[dest].set(flat_t)
    blk_start = jnp.arange(n_blocks, dtype=jnp.int32) * MOE_ROWS
    block_e = jnp.minimum(jnp.sum((p_end[None, :] <= blk_start[:, None]).astype(jnp.int32), axis=1),
                          MOE_EXPERTS - 1)
    n_used = (p_end[-1] // MOE_ROWS).astype(jnp.int32).reshape(1)
    return block_e, n_used, row_tok.reshape(n_blocks, 1, MOE_ROWS), dest.reshape(t, MOE_TOPK).astype(jnp.int32)


def _start_row_gather(src_hbm, idx_ref, idx_of_row, dst, sem, n):
    def issue(r, carry):
        pltpu.make_async_copy(src_hbm.at[pl.ds(idx_ref[0, idx_of_row(r)], 1)], dst.at[pl.ds(r, 1)], sem).start()
        return carry
    lax.fori_loop(0, n, issue, 0, unroll=8)


def _wait_row_gather(src_hbm, dst, sem):
    pltpu.make_async_copy(src_hbm.at[pl.ds(0, dst.shape[0])], dst, sem).wait()


def _moe_kernel(be_ref, nu_ref, tok_ref, tok_next_ref, h_hbm, w1_ref, w3_ref, w2_ref, y_ref, xb, sem):
    i = pl.program_id(0)
    n_used = nu_ref[0]
    slot = i % 2
    row = lambda r: r

    @pl.when((i == 0) & (n_used > 0))
    def _():
        _start_row_gather(h_hbm, tok_ref, row, xb.at[0], sem.at[0], MOE_ROWS)

    @pl.when(i + 1 < n_used)
    def _():
        _start_row_gather(h_hbm, tok_next_ref, row, xb.at[1 - slot], sem.at[1 - slot], MOE_ROWS)

    @pl.when(i < n_used)
    def _():
        _wait_row_gather(h_hbm, xb.at[slot], sem.at[slot])
        x = xb[slot].astype(BF16)
        h1 = _dot(x, w1_ref[...])
        h3 = _dot(x, w3_ref[...])
        hm = (h1 * jax.nn.sigmoid(h1) * h3).astype(BF16)
        y_ref[...] = _dot(hm, w2_ref[...])

    @pl.when(i >= n_used)
    def _():
        y_ref[...] = jnp.zeros_like(y_ref)


def moe_experts(h2, tables, w1, w3, w2):
    block_e, n_used, row_tok, _ = tables
    n_blocks = row_tok.shape[0]
    d, hid = w1.shape[1], w1.shape[2]
    grid_spec = pltpu.PrefetchScalarGridSpec(
        num_scalar_prefetch=2, grid=(n_blocks,),
        in_specs=[BS((None, 1, MOE_ROWS), lambda i, be, nu: (i, 0, 0), memory_space=pltpu.SMEM),
                  BS((None, 1, MOE_ROWS), lambda i, be, nu: (jnp.minimum(i + 1, n_blocks - 1), 0, 0),
                     memory_space=pltpu.SMEM),
                  BS(memory_space=pl.ANY),
                  BS((None, d, hid), lambda i, be, nu: (be[i], 0, 0)),
                  BS((None, d, hid), lambda i, be, nu: (be[i], 0, 0)),
                  BS((None, hid, d), lambda i, be, nu: (be[i], 0, 0))],
        out_specs=BS((MOE_ROWS, d), lambda i, be, nu: (i, 0)),
        scratch_shapes=[pltpu.VMEM((2, MOE_ROWS, d), F32), pltpu.SemaphoreType.DMA((2,))])
    return pl.pallas_call(
        _moe_kernel, out_shape=SDS((n_blocks * MOE_ROWS, d), F32), grid_spec=grid_spec,
        compiler_params=_params("arbitrary"), name="moe_experts")(
            block_e, n_used, row_tok, row_tok, h2, w1, w3, w2)


def _combine_kernel(pos_ref, pos_next_ref, y_hbm, x_ref, w_ref, g_ref, o_ref, yb, sem):
    i = pl.program_id(0)
    tt = COMBINE_ROWS
    slot = i % 2
    idx = lambda r: r

    @pl.when(i == 0)
    def _():
        _start_row_gather(y_hbm, pos_ref, idx, yb.at[0], sem.at[0], MOE_TOPK * tt)

    @pl.when(i + 1 < pl.num_programs(0))
    def _():
        _start_row_gather(y_hbm, pos_next_ref, idx, yb.at[1 - slot], sem.at[1 - slot], MOE_TOPK * tt)

    _wait_row_gather(y_hbm, yb.at[slot], sem.at[slot])
    w = w_ref[...]
    f = yb[slot, 0:tt, :] * w[:, 0:1] + yb[slot, tt:2 * tt, :] * w[:, 1:2]
    o_ref[...] = x_ref[...] + g_ref[...] * f


def moe_combine(x1, y, pos, weight, gate, rows, seg_rows):
    d = x1.shape[1]
    tt = COMBINE_ROWS
    nt = rows // tt
    seg = _seg_map(seg_rows, tt, gate.shape[0])
    pos3 = pos.reshape(nt, tt, MOE_TOPK).transpose(0, 2, 1).reshape(nt, 1, MOE_TOPK * tt)
    return pl.pallas_call(
        _combine_kernel, out_shape=SDS((rows, d), F32), grid=(nt,),
        in_specs=[BS((None, 1, MOE_TOPK * tt), lambda i: (i, 0, 0), memory_space=pltpu.SMEM),
                  BS((None, 1, MOE_TOPK * tt), lambda i: (jnp.minimum(i + 1, nt - 1), 0, 0),
                     memory_space=pltpu.SMEM),
                  BS(memory_space=pl.ANY), BS((tt, d), lambda i: (i, 0)),
                  BS((tt, MOE_TOPK), lambda i: (i, 0)), BS((None, 1, d), lambda i: (seg(i), 0, 0))],
        out_specs=BS((tt, d), lambda i: (i, 0)),
        scratch_shapes=[pltpu.VMEM((2, MOE_TOPK * tt, d), F32), pltpu.SemaphoreType.DMA((2,))],
        compiler_params=_params("arbitrary"), name="moe_combine")(pos3, pos3, y, x1, weight, gate)


def kernel(x, c, ctx, c_ctx, ada_w, ada_b, norm1_g, w_in, gm_norm_g, gm_ws, gm_bs, na_rpb, ret_decay_fwd,
           ret_decay_bwd, w_branch_a, w_branch_b, w_branch_c, w_out, norm2_g, moe_w_group, moe_w_expert,
           moe_w1, moe_w3, moe_w2, final_norm_g):
    batch, seq, d = x.shape
    n_ctx = ctx.shape[1]
    depth = ada_w.shape[0]
    t_lat, t_ctx = batch * seq, batch * n_ctx
    t_all = t_lat + t_ctx
    cols, in_total = _col_offsets(d)
    assert w_in.shape[2] == in_total and seq % ROW_TILE == 0 and t_ctx % ROW_TILE == 0
    assert 8 >= batch + 1

    cond = jnp.concatenate([c, c_ctx[None, :], jnp.zeros((8 - batch - 1, d), c.dtype)], axis=0)
    mod = ada_modulation(jax.nn.silu(cond).astype(BF16), ada_w, ada_b)[:, :batch + 1]
    mod = mod.reshape(depth, batch + 1, 6, 1, d)

    xs = jnp.concatenate([x.reshape(t_lat, d), ctx.reshape(t_ctx, d)], axis=0)
    for l in range(depth):
        need_ctx = l < depth - 1
        rows = t_all if need_ctx else t_lat
        sh1, sc1, g1, sh2, sc2, g2 = (mod[l, :, k] for k in range(6))

        h = norm_mod(xs, norm1_g[l].reshape(1, d), sh1, sc1, t_all, seq, BF16)
        z = matmul(h, w_in[l].astype(BF16), t_all, BF16)

        a = chunk_gmlp(z, cols, gm_norm_g[l], gm_ws[l], gm_bs[l], rows)
        bb = neighbourhood_attention(z, cols, na_rpb[l], batch, seq, n_ctx)
        tabs = _decay_tables(ret_decay_fwd[l], ret_decay_bwd[l])
        s0f, s0b = context_state(z, cols, tabs, batch, seq, n_ctx)
        r = retention(z, cols, tabs, s0f, s0b, 0, seq, batch, False)
        if need_ctx:
            bb = jnp.concatenate([bb, context_attention(z, cols, batch, seq, n_ctx)], axis=0)
            zeros = jnp.zeros_like(s0f)
            r = jnp.concatenate([r, retention(z, cols, tabs, zeros, zeros, t_lat, n_ctx, batch, True)], axis=0)

        y = merge_branches(z, a, bb, r, w_branch_a[l].astype(BF16), w_branch_b[l].astype(BF16),
                           w_branch_c[l].astype(BF16), rows)
        x1 = matmul_residual(y, w_out[l].astype(BF16), xs, g1, rows, seq)

        w_router = jnp.concatenate(
            [moe_w_group[l], moe_w_expert[l],
             jnp.zeros((d, ROUTER_LANES - MOE_GROUPS - MOE_EXPERTS), F32)], axis=1).astype(BF16)
        h2, logits = norm_router(x1, norm2_g[l].reshape(1, d), sh2, sc2, w_router, rows, seq)
        eid, weight = _route(logits)
        tables = _dispatch_tables(eid)
        y_moe = moe_experts(h2, tables, moe_w1[l].astype(BF16), moe_w3[l].astype(BF16), moe_w2[l].astype(BF16))
        xs = moe_combine(x1, y_moe, tables[3], weight, g2, rows, seq)

    zero = jnp.zeros((1, 1, d), F32)
    out = norm_mod(xs, final_norm_g.reshape(1, d), zero, zero, t_lat, seq, F32)
    return out.reshape(batch, seq, d)
```

```python
import functools

import jax
import jax.numpy as jnp
import numpy as np
from jax import lax
from jax.experimental import pallas as pl
from jax.experimental.pallas import tpu as pltpu

F32 = jnp.float32
BF16 = jnp.bfloat16
SDS = jax.ShapeDtypeStruct
BS = pl.BlockSpec

EPS = 1e-6
NEG_INF = -1e30
HEAD_DIM = 128
GRID_W = 64
NA_ROWS = 8
NA_COLS = 16
NA_QROWS = 8
NA_KROWS = NA_QROWS + NA_ROWS
RET_DK = 128
RET_DV = 256
CHUNK = 128
RET_CHUNKS_PER_STEP = 16
ROPE_BASE = 10000.0
MOE_GROUPS = 4
MOE_EPG = 8
MOE_EXPERTS = MOE_GROUPS * MOE_EPG
MOE_TOPK = 2
ROUTER_LANES = 128
N_MIX_HEADS = 8
GM_WIDTH = N_MIX_HEADS * HEAD_DIM
NA_WIDTH = N_MIX_HEADS * HEAD_DIM
RET_QK_WIDTH = N_MIX_HEADS * RET_DK
RET_V_WIDTH = N_MIX_HEADS * RET_DV

V7X_VMEM_BYTES = 64 * 1024 * 1024
VMEM_LIMIT = V7X_VMEM_BYTES - 8 * 1024 * 1024

ROW_TILE = 512
COL_TILE = 1024
MOE_ROWS = 256
COMBINE_ROWS = 128


def _params(*sem):
    return pltpu.CompilerParams(dimension_semantics=sem, vmem_limit_bytes=VMEM_LIMIT)


def _col_offsets(d):
    off, out = 0, {}
    for name, width in (("gate_a", d), ("gate_b", d), ("gate_c", d), ("gm_u", GM_WIDTH), ("gm_v", GM_WIDTH),
                        ("na_q", NA_WIDTH), ("na_k", NA_WIDTH), ("na_v", NA_WIDTH),
                        ("ret_q", RET_QK_WIDTH), ("ret_k", RET_QK_WIDTH), ("ret_v", RET_V_WIDTH),
                        ("ret_g", RET_V_WIDTH)):
        out[name] = off
        off += width
    return out, off


def _dot(a, b):
    return jnp.dot(a, b, preferred_element_type=F32)


def _dot_nt(a, b):
    return lax.dot_general(a, b, (((1,), (1,)), ((), ())), preferred_element_type=F32)


def _norm_mod(x_ref, g_ref, sh_ref, sc_ref):
    x = x_ref[...]
    ms = jnp.mean(x * x, axis=-1, keepdims=True)
    y = x * lax.rsqrt(ms + EPS) * g_ref[...]
    return y * (1.0 + sc_ref[...]) + sh_ref[...]


def _norm_kernel(x_ref, g_ref, sh_ref, sc_ref, o_ref):
    o_ref[...] = _norm_mod(x_ref, g_ref, sh_ref, sc_ref).astype(o_ref.dtype)


def _first_lane(mask, lane):
    return jnp.min(jnp.where(mask, lane, float(ROUTER_LANES)), axis=-1, keepdims=True)


def _route(logits, run_ref):
    n = logits.shape[0]
    lane = lax.broadcasted_iota(jnp.int32, logits.shape, 1).astype(F32)
    neg = -jnp.inf
    is_g = lane < MOE_GROUPS
    gl = jnp.where(is_g, logits, neg)
    eg = jnp.exp(gl - jnp.max(gl, axis=-1, keepdims=True))
    pg = eg / jnp.sum(eg, axis=-1, keepdims=True)
    p_sel = jnp.max(pg, axis=-1, keepdims=True)
    grp = _first_lane((pg == p_sel) & is_g, lane)
    lo = MOE_GROUPS + MOE_EPG * grp
    is_e = (lane >= lo) & (lane < lo + MOE_EPG)
    el = jnp.where(is_e, logits, neg)
    ee = jnp.exp(el - jnp.max(el, axis=-1, keepdims=True))
    pe = ee / jnp.sum(ee, axis=-1, keepdims=True)
    t1 = jnp.max(pe, axis=-1, keepdims=True)
    i1 = _first_lane((pe == t1) & is_e, lane)
    rest = jnp.where(is_e & (lane != i1), pe, -1.0)
    t2 = jnp.max(rest, axis=-1, keepdims=True)
    i2 = _first_lane(rest == t2, lane)
    tsum = t1 + t2
    w1, w2 = p_sel * t1 / tsum, p_sel * t2 / tsum
    e1, e2 = i1 - MOE_GROUPS, i2 - MOE_GROUPS
    oh1, oh2 = lane == e1, lane == e2
    cnt = jnp.where(oh1 | oh2, 1.0, 0.0)
    r_i = lax.broadcasted_iota(jnp.int32, (n, n), 0)
    c_i = lax.broadcasted_iota(jnp.int32, (n, n), 1)
    tri = jnp.where(c_i < r_i, 1.0, 0.0).astype(BF16)
    before = _dot(tri, cnt.astype(BF16)) + run_ref[...]
    r1 = jnp.sum(jnp.where(oh1, before, 0.0), axis=-1, keepdims=True)
    r2 = jnp.sum(jnp.where(oh2, before, 0.0), axis=-1, keepdims=True)
    run_ref[...] = run_ref[...] + jnp.sum(cnt, axis=0, keepdims=True)
    slab = jnp.zeros(logits.shape, F32)
    for k, v in enumerate((e1, e2, r1, r2, w1, w2)):
        slab = jnp.where(lane == float(k), v, slab)
    return slab


def _norm_router_kernel(x_ref, g_ref, sh_ref, sc_ref, wr_ref, h_ref, rt_ref, cnt_ref, run_ref):
    @pl.when(pl.program_id(0) == 0)
    def _():
        run_ref[...] = jnp.zeros_like(run_ref)
    h = _norm_mod(x_ref, g_ref, sh_ref, sc_ref)
    h_ref[...] = h
    rt_ref[...] = _route(_dot(h.astype(BF16), wr_ref[...]), run_ref)
    cnt_ref[...] = run_ref[...]


def _seg_map(seg_rows, tile, nseg):
    per = seg_rows // tile
    return lambda i: jnp.minimum(i // per, nseg - 1)


def norm_mod(x, g, shift, scale, rows, seg_rows, out_dtype):
    d = x.shape[1]
    seg = _seg_map(seg_rows, ROW_TILE // 2, shift.shape[0])
    tr = ROW_TILE // 2
    return pl.pallas_call(
        _norm_kernel, out_shape=SDS((rows, d), out_dtype), grid=(rows // tr,),
        in_specs=[BS((tr, d), lambda i: (i, 0)), BS((1, d), lambda i: (0, 0)),
                  BS((None, 1, d), lambda i: (seg(i), 0, 0)), BS((None, 1, d), lambda i: (seg(i), 0, 0))],
        out_specs=BS((tr, d), lambda i: (i, 0)),
        compiler_params=_params("parallel"), name="norm_mod")(x, g, shift, scale)


def norm_router(x, g, shift, scale, w_router, rows, seg_rows):
    d = x.shape[1]
    tr = ROW_TILE // 2
    seg = _seg_map(seg_rows, tr, shift.shape[0])
    return pl.pallas_call(
        _norm_router_kernel,
        out_shape=(SDS((rows, d), F32), SDS((rows, ROUTER_LANES), F32), SDS((1, ROUTER_LANES), F32)),
        grid=(rows // tr,),
        in_specs=[BS((tr, d), lambda i: (i, 0)), BS((1, d), lambda i: (0, 0)),
                  BS((None, 1, d), lambda i: (seg(i), 0, 0)), BS((None, 1, d), lambda i: (seg(i), 0, 0)),
                  BS((d, ROUTER_LANES), lambda i: (0, 0))],
        out_specs=(BS((tr, d), lambda i: (i, 0)), BS((tr, ROUTER_LANES), lambda i: (i, 0)),
                   BS((1, ROUTER_LANES), lambda i: (0, 0))),
        scratch_shapes=[pltpu.VMEM((1, ROUTER_LANES), F32)],
        compiler_params=_params("arbitrary"), name="norm_router")(x, g, shift, scale, w_router)


def _ada_kernel(a_ref, w_ref, b_ref, o_ref):
    o_ref[...] = _dot(a_ref[...], w_ref[...].astype(BF16)) + b_ref[...]


def ada_modulation(act, ada_w, ada_b):
    depth, d, n = ada_w.shape
    tn = ROW_TILE
    return pl.pallas_call(
        _ada_kernel, out_shape=SDS((depth, act.shape[0], n), F32), grid=(depth, n // tn),
        in_specs=[BS(act.shape, lambda l, j: (0, 0)), BS((None, d, tn), lambda l, j: (l, 0, j)),
                  BS((None, 1, tn), lambda l, j: (l, 0, j))],
        out_specs=BS((None, act.shape[0], tn), lambda l, j: (l, 0, j)),
        compiler_params=_params("parallel", "parallel"), name="ada_modulation")(
            act, ada_w, ada_b.reshape(depth, 1, n))


def _cast_kernel(x_ref, o_ref):
    o_ref[...] = x_ref[...].astype(o_ref.dtype)


def cast_layer_bf16(w, layer):
    _, e, r, c = w.shape
    return pl.pallas_call(
        _cast_kernel, out_shape=SDS((e, r, c), BF16), grid=(e,),
        in_specs=[BS((None, None, r, c), lambda i: (layer, i, 0, 0))],
        out_specs=BS((None, r, c), lambda i: (i, 0, 0)),
        compiler_params=_params("parallel"), name="cast_bf16")(w)


def _in_proj_kernel(a_ref, b_ref, o_ref, bq_ref):
    @pl.when(pl.program_id(1) == 0)
    def _():
        bq_ref[...] = b_ref[...].astype(BF16)
    o_ref[...] = _dot(a_ref[...], bq_ref[...]).astype(o_ref.dtype)


def in_proj(a, w, layer, rows, out_dtype):
    _, k, n = w.shape
    tm, tn = ROW_TILE, COL_TILE
    return pl.pallas_call(
        _in_proj_kernel, out_shape=SDS((rows, n), out_dtype), grid=(n // tn, rows // tm),
        in_specs=[BS((tm, k), lambda j, i: (i, 0)),
                  BS((None, k, tn), lambda j, i: (layer, 0, j), pipeline_mode=pl.Buffered(1))],
        out_specs=BS((tm, tn), lambda j, i: (i, j)),
        scratch_shapes=[pltpu.VMEM((k, tn), BF16)],
        compiler_params=_params("parallel", "arbitrary"), name="in_proj")(a, w)


def _mm_res_kernel(a_ref, b_ref, x_ref, g_ref, o_ref):
    o_ref[...] = x_ref[...] + g_ref[...] * _dot(a_ref[...], b_ref[...])


def matmul_residual(a, b, x, gate, rows, seg_rows):
    k, n = b.shape
    tm, tn = ROW_TILE, COL_TILE
    seg = _seg_map(seg_rows, tm, gate.shape[0])
    return pl.pallas_call(
        _mm_res_kernel, out_shape=SDS((rows, n), F32), grid=(n // tn, rows // tm),
        in_specs=[BS((tm, k), lambda j, i: (i, 0)), BS((k, tn), lambda j, i: (0, j)),
                  BS((tm, tn), lambda j, i: (i, j)), BS((None, 1, tn), lambda j, i: (seg(i), 0, j))],
        out_specs=BS((tm, tn), lambda j, i: (i, j)),
        compiler_params=_params("parallel", "parallel"), name="out_proj")(a, b, x, gate)


def _merge_kernel(a_ref, b_ref, r_ref, ga_ref, gb_ref, gc_ref, wa_ref, wb_ref, wc_ref, o_ref):
    def branch(x_ref, w_ref, g_ref):
        return jax.nn.sigmoid(g_ref[...].astype(F32)) * _dot(x_ref[...], w_ref[...])
    y = branch(a_ref, wa_ref, ga_ref) + branch(b_ref, wb_ref, gb_ref) + branch(r_ref, wc_ref, gc_ref)
    o_ref[...] = y.astype(o_ref.dtype)


def merge_branches(z, a, bb, r, w_a, w_b, w_c, rows):
    d = w_a.shape[1]
    tm, tn = ROW_TILE, COL_TILE
    nb = d // tn
    row_spec = lambda arr: BS((tm, arr.shape[1]), lambda j, i: (i, 0))
    w_spec = lambda arr: BS((arr.shape[0], tn), lambda j, i: (0, j))
    gate_spec = lambda g: BS((tm, tn), lambda j, i: (i, g * nb + j))
    return pl.pallas_call(
        _merge_kernel, out_shape=SDS((rows, d), BF16), grid=(nb, rows // tm),
        in_specs=[row_spec(a), row_spec(bb), row_spec(r), gate_spec(0), gate_spec(1), gate_spec(2),
                  w_spec(w_a), w_spec(w_b), w_spec(w_c)],
        out_specs=BS((tm, tn), lambda j, i: (i, j)),
        compiler_params=_params("parallel", "parallel"), name="merge_branches")(
            a, bb, r, z, z, z, w_a, w_b, w_c)


GM_CHUNKS_PER_STEP = 4


def _gmlp_kernel(u_ref, v_ref, ng_ref, ws_ref, bs_ref, o_ref):
    for c in range(GM_CHUNKS_PER_STEP):
        rows = slice(c * CHUNK, (c + 1) * CHUNK)
        v = jax.nn.gelu(v_ref[rows, :].astype(F32))
        mu = jnp.mean(v, axis=-1, keepdims=True)
        var = jnp.mean(jnp.square(v - mu), axis=-1, keepdims=True)
        vn = ((v - mu) * lax.rsqrt(var + EPS) * ng_ref[...]).astype(BF16)
        for g in range(N_MIX_HEADS):
            cols = slice(g * HEAD_DIM, (g + 1) * HEAD_DIM)
            mixed = _dot(ws_ref[g], vn[:, cols]) + bs_ref[g]
            u = jax.nn.gelu(u_ref[rows, cols].astype(F32))
            o_ref[rows, cols] = (u * mixed).astype(o_ref.dtype)


def chunk_gmlp(z, cols, norm_g, ws, bs, rows):
    tr = GM_CHUNKS_PER_STEP * CHUNK
    ub, vb = cols["gm_u"] // GM_WIDTH, cols["gm_v"] // GM_WIDTH
    bs_b = jnp.broadcast_to(bs[:, :, None], bs.shape + (HEAD_DIM,)).astype(F32)
    return pl.pallas_call(
        _gmlp_kernel, out_shape=SDS((rows, GM_WIDTH), BF16), grid=(rows // tr,),
        in_specs=[BS((tr, GM_WIDTH), lambda i: (i, ub)), BS((tr, GM_WIDTH), lambda i: (i, vb)),
                  BS((1, GM_WIDTH), lambda i: (0, 0)), BS(ws.shape, lambda i: (0, 0, 0)),
                  BS(bs_b.shape, lambda i: (0, 0, 0))],
        out_specs=BS((tr, GM_WIDTH), lambda i: (i, 0)),
        compiler_params=_params("parallel"), name="chunk_gmlp")(
            z, z, norm_g.reshape(1, GM_WIDTH), ws.astype(BF16), bs_b)


def _na_bias_tables(rpb, grid_rows):
    nq = grid_rows // NA_QROWS
    depth, n_heads = rpb.shape[:2]
    row_sel, ok = [], []
    i = np.arange(NA_QROWS)[:, None]
    m = np.arange(NA_KROWS)[None, :]
    c = np.arange(GRID_W)[:, None]
    w = np.arange(GRID_W)[None, :]
    c_start = np.clip(c - NA_COLS // 2, 0, GRID_W - NA_COLS)
    col_ok = (w >= c_start) & (w < c_start + NA_COLS)
    dc = np.clip(w - c + NA_COLS - 1, 0, 2 * NA_COLS - 2)
    col_sel = (dc.reshape(-1)[None, :] == np.arange(2 * NA_COLS - 1)[:, None]).astype(np.float32)
    shape2 = (NA_QROWS * GRID_W, NA_KROWS * GRID_W)
    for j in (0, 1, nq - 1):
        base = int(np.clip(NA_QROWS * j - NA_ROWS // 2, 0, grid_rows - NA_KROWS))
        r = NA_QROWS * j + i
        kr = base + m
        r_start = np.clip(r - NA_ROWS // 2, 0, grid_rows - NA_ROWS)
        row_ok = (kr >= r_start) & (kr < r_start + NA_ROWS)
        dr = np.clip(kr - r + NA_ROWS - 1, 0, 2 * NA_ROWS - 2)
        row_sel.append((dr.reshape(-1)[:, None] == np.arange(2 * NA_ROWS - 1)[None, :]).astype(np.float32))
        ok.append((row_ok[:, None, :, None] & col_ok[None, :, None, :]).reshape(shape2))
    bias = jnp.einsum("kpa,lhab,bq->lkhpq", np.stack(row_sel), rpb.astype(F32), col_sel,
                      precision=lax.Precision.HIGHEST)
    bias = bias.reshape(depth, 3, n_heads, NA_QROWS, NA_KROWS, GRID_W, GRID_W).transpose(0, 1, 2, 3, 5, 4, 6)
    return jnp.where(np.stack(ok)[None, :, None], bias.reshape((depth, 3, n_heads) + shape2), NEG_INF)


def _na_kernel(q_ref, k_ref, v_ref, kc_ref, vc_ref, bias_ref, o_ref, *, grid_rows):
    j = pl.program_id(2)
    nq = grid_rows // NA_QROWS
    kind = jnp.where(j == 0, 0, jnp.where(j == nq - 1, 2, 1))
    base_row = jnp.clip(NA_QROWS * j - NA_ROWS // 2, 0, grid_rows - NA_KROWS)
    base = pl.multiple_of(base_row * GRID_W, (NA_ROWS // 2) * GRID_W)
    nk = NA_KROWS * GRID_W
    q = (q_ref[...].astype(F32) * (HEAD_DIM ** -0.5)).astype(BF16)
    s_loc = _dot_nt(q, k_ref[pl.ds(base, nk), :]) + bias_ref[kind]
    s_ctx = _dot_nt(q, kc_ref[...])
    mx = jnp.maximum(jnp.max(s_loc, axis=-1, keepdims=True), jnp.max(s_ctx, axis=-1, keepdims=True))
    e_loc = jnp.exp(s_loc - mx)
    e_ctx = jnp.exp(s_ctx - mx)
    den = jnp.sum(e_loc, axis=-1, keepdims=True) + jnp.sum(e_ctx, axis=-1, keepdims=True)
    o = _dot(e_loc.astype(BF16), v_ref[pl.ds(base, nk), :]) + _dot(e_ctx.astype(BF16), vc_ref[...])
    o_ref[...] = (o / den).astype(o_ref.dtype)


def neighbourhood_attention(z, cols, bias, layer, batch, seq, n_ctx):
    grid_rows = seq // GRID_W
    assert grid_rows % NA_QROWS == 0 and grid_rows >= 2 * NA_KROWS
    nq = grid_rows // NA_QROWS
    qb = NA_QROWS * GRID_W
    qc, kc, vc = (cols[n] // HEAD_DIM for n in ("na_q", "na_k", "na_v"))
    ctx_blk0 = batch * seq // n_ctx
    return pl.pallas_call(
        functools.partial(_na_kernel, grid_rows=grid_rows),
        out_shape=SDS((batch * seq, NA_WIDTH), BF16), grid=(N_MIX_HEADS, batch, nq),
        in_specs=[BS((qb, HEAD_DIM), lambda h, b, j: (b * nq + j, qc + h)),
                  BS((seq, HEAD_DIM), lambda h, b, j: (b, kc + h)),
                  BS((seq, HEAD_DIM), lambda h, b, j: (b, vc + h)),
                  BS((n_ctx, HEAD_DIM), lambda h, b, j: (ctx_blk0 + b, kc + h)),
                  BS((n_ctx, HEAD_DIM), lambda h, b, j: (ctx_blk0 + b, vc + h)),
                  BS((None, 3, None, qb, NA_KROWS * GRID_W), lambda h, b, j: (layer, 0, h, 0, 0))],
        out_specs=BS((qb, HEAD_DIM), lambda h, b, j: (b * nq + j, h)),
        compiler_params=_params("parallel", "parallel", "parallel"), name="neighbourhood_attention")(
            z, z, z, z, z, bias)


def _ctx_attn_kernel(q_ref, k_ref, v_ref, o_ref):
    q = (q_ref[...].astype(F32) * (HEAD_DIM ** -0.5)).astype(BF16)
    s = _dot_nt(q, k_ref[...])
    e = jnp.exp(s - jnp.max(s, axis=-1, keepdims=True))
    o = _dot(e.astype(BF16), v_ref[...]) / jnp.sum(e, axis=-1, keepdims=True)
    o_ref[...] = o.astype(o_ref.dtype)


def context_attention(z, cols, batch, seq, n_ctx):
    qc, kc, vc = (cols[n] // HEAD_DIM for n in ("na_q", "na_k", "na_v"))
    blk0 = batch * seq // n_ctx
    spec = lambda col: BS((n_ctx, HEAD_DIM), lambda b, h: (blk0 + b, col + h))
    return pl.pallas_call(
        _ctx_attn_kernel, out_shape=SDS((batch * n_ctx, NA_WIDTH), BF16), grid=(batch, N_MIX_HEADS),
        in_specs=[spec(qc), spec(kc), spec(vc)],
        out_specs=BS((n_ctx, HEAD_DIM), lambda b, h: (b, h)),
        compiler_params=_params("parallel", "parallel"), name="context_attention")(z, z, z)


def _rope(t_ref, cos_ref, sin_ref):
    t = t_ref[...].astype(F32)
    lane = lax.broadcasted_iota(jnp.int32, t.shape, 1)
    quarter = HEAD_DIM // 4
    partner = jnp.where((lane % (2 * quarter)) < quarter,
                        pltpu.roll(t, HEAD_DIM - quarter, 1), pltpu.roll(t, quarter, 1))
    return t * cos_ref[...] + partner * sin_ref[...]


def _rope_tables(n, identity):
    if identity:
        return jnp.ones((n, HEAD_DIM), F32), jnp.zeros((n, HEAD_DIM), F32)
    nf = HEAD_DIM // 4
    pos = jnp.arange(n)
    p_row = (pos // GRID_W).astype(F32)
    p_col = (pos % GRID_W).astype(F32)
    inv = ROPE_BASE ** (-jnp.arange(nf, dtype=F32) / nf)
    a_row = p_row[:, None] * inv[None, :]
    a_col = p_col[:, None] * inv[None, :]
    cos = jnp.concatenate([jnp.cos(a_row), jnp.cos(a_row), jnp.cos(a_col), jnp.cos(a_col)], axis=-1)
    sin = jnp.concatenate([-jnp.sin(a_row), jnp.sin(a_row), -jnp.sin(a_col), jnp.sin(a_col)], axis=-1)
    return cos, sin


def _decay_tables(dec_f, dec_b):
    log_gf = jnp.log1p(-jnp.exp2(dec_f.astype(F32)))
    log_gb = jnp.log1p(-jnp.exp2(dec_b.astype(F32)))
    idx = jnp.arange(CHUNK, dtype=F32)
    rel = idx[:, None] - idx[None, :]
    d_f = jnp.where(rel >= 0, jnp.exp(log_gf[:, None, None] * jnp.maximum(rel, 0.0)), 0.0)
    d_b = jnp.where(rel < 0, jnp.exp(log_gb[:, None, None] * jnp.maximum(-rel, 0.0)), 0.0)
    bcast = lambda v: jnp.broadcast_to(v[:, :, None], v.shape + (RET_DK,))
    return dict(
        log_gf=log_gf, log_gb=log_gb,
        dfb=d_f + d_b,
        qdf=bcast(jnp.exp(log_gf[:, None] * (idx + 1.0))),
        qdb=bcast(jnp.exp(log_gb[:, None] * (CHUNK - idx))),
        kdf=bcast(jnp.exp(log_gf[:, None] * (CHUNK - 1.0 - idx))),
        kdb=bcast(jnp.exp(log_gb[:, None] * idx)),
        cdf=jnp.broadcast_to(jnp.exp(log_gf * CHUNK)[:, None, None], (log_gf.shape[0], 1, RET_DV)),
        cdb=jnp.broadcast_to(jnp.exp(log_gb * CHUNK)[:, None, None], (log_gb.shape[0], 1, RET_DV)),
    )


def _ctx_state_kernel(k_ref, v_ref, wf_ref, wb_ref, sf_ref, sb_ref):
    k = k_ref[...].astype(F32) * (RET_DK ** -0.5)
    v = v_ref[...]
    sf_ref[...] = _dot((k * wf_ref[...]).T.astype(BF16), v)
    sb_ref[...] = _dot((k * wb_ref[...]).T.astype(BF16), v)


def context_state(z, cols, tabs, batch, seq, n_ctx):
    pos = jnp.arange(n_ctx, dtype=F32)
    bcast = lambda v: jnp.broadcast_to(v[:, :, None], v.shape + (RET_DK,))
    w_f = bcast(jnp.exp(tabs["log_gf"][:, None] * ((n_ctx - 1.0) - pos)))
    w_b = bcast(jnp.exp(tabs["log_gb"][:, None] * pos))
    kc, vc = cols["ret_k"] // RET_DK, cols["ret_v"] // RET_DV
    blk0 = batch * seq // n_ctx
    out = SDS((batch, N_MIX_HEADS, RET_DK, RET_DV), F32)
    tab_spec = BS((None, n_ctx, RET_DK), lambda b, h: (h, 0, 0))
    out_spec = BS((None, None, RET_DK, RET_DV), lambda b, h: (b, h, 0, 0))
    return pl.pallas_call(
        _ctx_state_kernel, out_shape=(out, out), grid=(batch, N_MIX_HEADS),
        in_specs=[BS((n_ctx, RET_DK), lambda b, h: (blk0 + b, kc + h)),
                  BS((n_ctx, RET_DV), lambda b, h: (blk0 + b, vc + h)), tab_spec, tab_spec],
        out_specs=(out_spec, out_spec),
        compiler_params=_params("parallel", "parallel"), name="context_state")(z, z, w_f, w_b)


def _ret_state_kernel(kf_ref, vf_ref, cosf_ref, sinf_ref, kb_ref, vb_ref, cosb_ref, sinb_ref,
                      kdf_ref, kdb_ref, cdf_ref, cdb_ref, s0f_ref, s0b_ref, sf_ref, sb_ref, st_ref, *, cpg):
    @pl.when(pl.program_id(2) == 0)
    def _():
        st_ref[0] = s0f_ref[...]
        st_ref[1] = s0b_ref[...]

    def scan(k_ref, v_ref, cos_ref, sin_ref, kd_ref, cd_ref, out_ref, slot, order):
        kr = _rope(k_ref, cos_ref, sin_ref) * (RET_DK ** -0.5)
        s = st_ref[slot]
        for c in order:
            rows = slice(c * CHUNK, (c + 1) * CHUNK)
            out_ref[c] = s.astype(out_ref.dtype)
            kc = (kr[rows, :] * kd_ref[...]).T.astype(BF16)
            s = s * cd_ref[...] + _dot(kc, v_ref[rows, :])
        st_ref[slot] = s

    scan(kf_ref, vf_ref, cosf_ref, sinf_ref, kdf_ref, cdf_ref, sf_ref, 0, range(cpg))
    scan(kb_ref, vb_ref, cosb_ref, sinb_ref, kdb_ref, cdb_ref, sb_ref, 1, reversed(range(cpg)))


def _ret_out_kernel(q_ref, k_ref, v_ref, g_ref, cos_ref, sin_ref, sf_ref, sb_ref, dfb_ref, qdf_ref, qdb_ref,
                    o_ref, *, cpg):
    qr = _rope(q_ref, cos_ref, sin_ref)
    kr = (_rope(k_ref, cos_ref, sin_ref) * (RET_DK ** -0.5)).astype(BF16)
    for c in range(cpg):
        rows = slice(c * CHUNK, (c + 1) * CHUNK)
        qc = qr[rows, :]
        p = (_dot_nt(qc.astype(BF16), kr[rows, :]) * dfb_ref[...]).astype(BF16)
        o = (_dot(p, v_ref[rows, :]) + _dot((qc * qdf_ref[...]).astype(BF16), sf_ref[c])
             + _dot((qc * qdb_ref[...]).astype(BF16), sb_ref[c]))
        mu = jnp.mean(o, axis=-1, keepdims=True)
        var = jnp.mean(jnp.square(o - mu), axis=-1, keepdims=True)
        on = (o - mu) * lax.rsqrt(var + EPS)
        g = g_ref[rows, :].astype(F32)
        o_ref[rows, :] = (on * (g * jax.nn.sigmoid(g))).astype(o_ref.dtype)


def retention(z, cols, tabs, s0f, s0b, row0, n, batch, rope_identity):
    nc = n // CHUNK
    cpg = min(RET_CHUNKS_PER_STEP, nc)
    ng = nc // cpg
    gr = cpg * CHUNK
    blk0 = row0 // gr
    qc, kc = cols["ret_q"] // RET_DK, cols["ret_k"] // RET_DK
    vc, gc = cols["ret_v"] // RET_DV, cols["ret_g"] // RET_DV
    cos, sin = _rope_tables(n, rope_identity)
    fwd = lambda b, h, g: blk0 + b * ng + g
    bwd = lambda b, h, g: blk0 + b * ng + (ng - 1 - g)
    head_tab = lambda w: BS((None, CHUNK, w), lambda b, h, g: (h, 0, 0))
    c_tab = BS((None, 1, RET_DV), lambda b, h, g: (h, 0, 0))
    s0_spec = BS((None, None, RET_DK, RET_DV), lambda b, h, g: (b, h, 0, 0))
    st_shape = SDS((batch, N_MIX_HEADS, nc, RET_DK, RET_DV), BF16)
    st_spec = lambda rev: BS((None, None, cpg, RET_DK, RET_DV),
                             lambda b, h, g: (b, h, (ng - 1 - g) if rev else g, 0, 0))
    sf, sb = pl.pallas_call(
        functools.partial(_ret_state_kernel, cpg=cpg), out_shape=(st_shape, st_shape),
        grid=(batch, N_MIX_HEADS, ng),
        in_specs=[BS((gr, RET_DK), lambda b, h, g: (fwd(b, h, g), kc + h)),
                  BS((gr, RET_DV), lambda b, h, g: (fwd(b, h, g), vc + h)),
                  BS((gr, RET_DK), lambda b, h, g: (g, 0)), BS((gr, RET_DK), lambda b, h, g: (g, 0)),
                  BS((gr, RET_DK), lambda b, h, g: (bwd(b, h, g), kc + h)),
                  BS((gr, RET_DV), lambda b, h, g: (bwd(b, h, g), vc + h)),
                  BS((gr, RET_DK), lambda b, h, g: (ng - 1 - g, 0)),
                  BS((gr, RET_DK), lambda b, h, g: (ng - 1 - g, 0)),
                  head_tab(RET_DK), head_tab(RET_DK), c_tab, c_tab, s0_spec, s0_spec],
        out_specs=(st_spec(False), st_spec(True)),
        scratch_shapes=[pltpu.VMEM((2, RET_DK, RET_DV), F32)],
        compiler_params=_params("parallel", "parallel", "arbitrary"), name="retention_state")(
            z, z, cos, sin, z, z, cos, sin, tabs["kdf"], tabs["kdb"], tabs["cdf"], tabs["cdb"], s0f, s0b)
    return pl.pallas_call(
        functools.partial(_ret_out_kernel, cpg=cpg), out_shape=SDS((batch * n, RET_V_WIDTH), BF16),
        grid=(batch, N_MIX_HEADS, ng),
        in_specs=[BS((gr, RET_DK), lambda b, h, g: (fwd(b, h, g), qc + h)),
                  BS((gr, RET_DK), lambda b, h, g: (fwd(b, h, g), kc + h)),
                  BS((gr, RET_DV), lambda b, h, g: (fwd(b, h, g), vc + h)),
                  BS((gr, RET_DV), lambda b, h, g: (fwd(b, h, g), gc + h)),
                  BS((gr, RET_DK), lambda b, h, g: (g, 0)), BS((gr, RET_DK), lambda b, h, g: (g, 0)),
                  st_spec(False), st_spec(False),
                  head_tab(CHUNK), head_tab(RET_DK), head_tab(RET_DK)],
        out_specs=BS((gr, RET_DV), lambda b, h, g: (b * ng + g, h)),
        compiler_params=_params("parallel", "parallel", "parallel"), name="retention_out")(
            z, z, z, z, cos, sin, sf, sb, tabs["dfb"], tabs["qdf"], tabs["qdb"])


def _dispatch_tables(routed, counts):
    t = routed.shape[0]
    n_assign = t * MOE_TOPK
    flat_e = routed[:, 0:MOE_TOPK].astype(jnp.int32).reshape(-1)
    rank = routed[:, MOE_TOPK:2 * MOE_TOPK].astype(jnp.int32).reshape(-1)
    flat_t = jnp.repeat(jnp.arange(t, dtype=jnp.int32), MOE_TOPK)
    counts = counts[0, :MOE_EXPERTS].astype(jnp.int32)
    padded = (counts + MOE_ROWS - 1) // MOE_ROWS * MOE_ROWS
    p_end = jnp.cumsum(padded)
    dest = (p_end - padded)[flat_e] + rank
    n_blocks = (n_assign + MOE_EXPERTS * (MOE_ROWS - 1)) // MOE_ROWS
    row_tok = jnp.zeros((n_blocks * MOE_ROWS,), jnp.int32).at[dest].set(flat_t)
    blk_start = jnp.arange(n_blocks, dtype=jnp.int32) * MOE_ROWS
    block_e = jnp.minimum(jnp.sum((p_end[None, :] <= blk_start[:, None]).astype(jnp.int32), axis=1),
                          MOE_EXPERTS - 1)
    n_used = (p_end[-1] // MOE_ROWS).astype(jnp.int32).reshape(1)
    return block_e, n_used, row_tok.reshape(n_blocks, 1, MOE_ROWS), dest.reshape(t, MOE_TOPK).astype(jnp.int32)


def _start_row_gather(src_hbm, idx_ref, idx_of_row, dst, sem, n):
    def issue(r, carry):
        pltpu.make_async_copy(src_hbm.at[pl.ds(idx_ref[0, idx_of_row(r)], 1)], dst.at[pl.ds(r, 1)], sem).start()
        return carry
    lax.fori_loop(0, n, issue, 0, unroll=8)


def _wait_row_gather(src_hbm, dst, sem):
    pltpu.make_async_copy(src_hbm.at[pl.ds(0, dst.shape[0])], dst, sem).wait()


def _moe_kernel(be_ref, nu_ref, tok_ref, tok_next_ref, h_hbm, w1_ref, w3_ref, w2_ref, y_ref, xb, sem):
    i = pl.program_id(0)
    n_used = nu_ref[0]
    slot = i % 2
    row = lambda r: r

    @pl.when((i == 0) & (n_used > 0))
    def _():
        _start_row_gather(h_hbm, tok_ref, row, xb.at[0], sem.at[0], MOE_ROWS)

    @pl.when(i + 1 < n_used)
    def _():
        _start_row_gather(h_hbm, tok_next_ref, row, xb.at[1 - slot], sem.at[1 - slot], MOE_ROWS)

    @pl.when(i < n_used)
    def _():
        _wait_row_gather(h_hbm, xb.at[slot], sem.at[slot])
        x = xb[slot].astype(BF16)
        h1 = _dot(x, w1_ref[...])
        h3 = _dot(x, w3_ref[...])
        hm = (h1 * jax.nn.sigmoid(h1) * h3).astype(BF16)
        y_ref[...] = _dot(hm, w2_ref[...])

    @pl.when(i >= n_used)
    def _():
        y_ref[...] = jnp.zeros_like(y_ref)


def moe_experts(h2, tables, w1, w3, w2):
    block_e, n_used, row_tok, _ = tables
    n_blocks = row_tok.shape[0]
    d, hid = w1.shape[1], w1.shape[2]
    grid_spec = pltpu.PrefetchScalarGridSpec(
        num_scalar_prefetch=2, grid=(n_blocks,),
        in_specs=[BS((None, 1, MOE_ROWS), lambda i, be, nu: (i, 0, 0), memory_space=pltpu.SMEM),
                  BS((None, 1, MOE_ROWS), lambda i, be, nu: (jnp.minimum(i + 1, n_blocks - 1), 0, 0),
                     memory_space=pltpu.SMEM),
                  BS(memory_space=pl.ANY),
                  BS((None, d, hid), lambda i, be, nu: (be[i], 0, 0)),
                  BS((None, d, hid), lambda i, be, nu: (be[i], 0, 0)),
                  BS((None, hid, d), lambda i, be, nu: (be[i], 0, 0))],
        out_specs=BS((MOE_ROWS, d), lambda i, be, nu: (i, 0)),
        scratch_shapes=[pltpu.VMEM((2, MOE_ROWS, d), F32), pltpu.SemaphoreType.DMA((2,))])
    return pl.pallas_call(
        _moe_kernel, out_shape=SDS((n_blocks * MOE_ROWS, d), F32), grid_spec=grid_spec,
        compiler_params=_params("arbitrary"), name="moe_experts")(
            block_e, n_used, row_tok, row_tok, h2, w1, w3, w2)


def _combine_kernel(pos_ref, pos_next_ref, y_hbm, x_ref, w_ref, g_ref, o_ref, yb, sem):
    i = pl.program_id(0)
    tt = COMBINE_ROWS
    slot = i % 2
    idx = lambda r: r

    @pl.when(i == 0)
    def _():
        _start_row_gather(y_hbm, pos_ref, idx, yb.at[0], sem.at[0], MOE_TOPK * tt)

    @pl.when(i + 1 < pl.num_programs(0))
    def _():
        _start_row_gather(y_hbm, pos_next_ref, idx, yb.at[1 - slot], sem.at[1 - slot], MOE_TOPK * tt)

    _wait_row_gather(y_hbm, yb.at[slot], sem.at[slot])
    w = w_ref[...]
    w0, w1 = (w[:, 2 * MOE_TOPK + k:2 * MOE_TOPK + k + 1] for k in range(MOE_TOPK))
    f = yb[slot, 0:tt, :] * w0 + yb[slot, tt:2 * tt, :] * w1
    o_ref[...] = x_ref[...] + g_ref[...] * f


def moe_combine(x1, y, pos, routed, gate, rows, seg_rows):
    d = x1.shape[1]
    tt = COMBINE_ROWS
    nt = rows // tt
    seg = _seg_map(seg_rows, tt, gate.shape[0])
    pos3 = pos.reshape(nt, tt, MOE_TOPK).transpose(0, 2, 1).reshape(nt, 1, MOE_TOPK * tt)
    return pl.pallas_call(
        _combine_kernel, out_shape=SDS((rows, d), F32), grid=(nt,),
        in_specs=[BS((None, 1, MOE_TOPK * tt), lambda i: (i, 0, 0), memory_space=pltpu.SMEM),
                  BS((None, 1, MOE_TOPK * tt), lambda i: (jnp.minimum(i + 1, nt - 1), 0, 0),
                     memory_space=pltpu.SMEM),
                  BS(memory_space=pl.ANY), BS((tt, d), lambda i: (i, 0)),
                  BS((tt, ROUTER_LANES), lambda i: (i, 0)), BS((None, 1, d), lambda i: (seg(i), 0, 0))],
        out_specs=BS((tt, d), lambda i: (i, 0)),
        scratch_shapes=[pltpu.VMEM((2, MOE_TOPK * tt, d), F32), pltpu.SemaphoreType.DMA((2,))],
        compiler_params=_params("arbitrary"), name="moe_combine")(pos3, pos3, y, x1, routed, gate)


def kernel(x, c, ctx, c_ctx, ada_w, ada_b, norm1_g, w_in, gm_norm_g, gm_ws, gm_bs, na_rpb, ret_decay_fwd,
           ret_decay_bwd, w_branch_a, w_branch_b, w_branch_c, w_out, norm2_g, moe_w_group, moe_w_expert,
           moe_w1, moe_w3, moe_w2, final_norm_g):
    batch, seq, d = x.shape
    n_ctx = ctx.shape[1]
    depth = ada_w.shape[0]
    t_lat, t_ctx = batch * seq, batch * n_ctx
    t_all = t_lat + t_ctx
    cols, in_total = _col_offsets(d)
    assert w_in.shape[2] == in_total and seq % ROW_TILE == 0 and t_ctx % ROW_TILE == 0
    assert 8 >= batch + 1

    cond = jnp.concatenate([c, c_ctx[None, :], jnp.zeros((8 - batch - 1, d), c.dtype)], axis=0)
    mod = ada_modulation(jax.nn.silu(cond).astype(BF16), ada_w, ada_b)[:, :batch + 1]
    mod = mod.reshape(depth, batch + 1, 6, 1, d)

    na_bias = _na_bias_tables(na_rpb, seq // GRID_W)
    xs = jnp.concatenate([x.reshape(t_lat, d), ctx.reshape(t_ctx, d)], axis=0)
    for l in range(depth):
        need_ctx = l < depth - 1
        rows = t_all if need_ctx else t_lat
        sh1, sc1, g1, sh2, sc2, g2 = (mod[l, :, k] for k in range(6))

        h = norm_mod(xs, norm1_g[l].reshape(1, d), sh1, sc1, t_all, seq, BF16)
        z = in_proj(h, w_in, l, t_all, BF16)

        a = chunk_gmlp(z, cols, gm_norm_g[l], gm_ws[l], gm_bs[l], rows)
        bb = neighbourhood_attention(z, cols, na_bias, l, batch, seq, n_ctx)
        tabs = _decay_tables(ret_decay_fwd[l], ret_decay_bwd[l])
        s0f, s0b = context_state(z, cols, tabs, batch, seq, n_ctx)
        r = retention(z, cols, tabs, s0f, s0b, 0, seq, batch, False)
        if need_ctx:
            bb = jnp.concatenate([bb, context_attention(z, cols, batch, seq, n_ctx)], axis=0)
            zeros = jnp.zeros_like(s0f)
            r = jnp.concatenate([r, retention(z, cols, tabs, zeros, zeros, t_lat, n_ctx, batch, True)], axis=0)

        y = merge_branches(z, a, bb, r, w_branch_a[l].astype(BF16), w_branch_b[l].astype(BF16),
                           w_branch_c[l].astype(BF16), rows)
        x1 = matmul_residual(y, w_out[l].astype(BF16), xs, g1, rows, seq)

        w_router = jnp.concatenate(
            [moe_w_group[l], moe_w_expert[l],
             jnp.zeros((d, ROUTER_LANES - MOE_GROUPS - MOE_EXPERTS), F32)], axis=1).astype(BF16)
        h2, routed, counts = norm_router(x1, norm2_g[l].reshape(1, d), sh2, sc2, w_router, rows, seq)
        tables = _dispatch_tables(routed, counts)
        y_moe = moe_experts(h2, tables, cast_layer_bf16(moe_w1, l), cast_layer_bf16(moe_w3, l),
                            cast_layer_bf16(moe_w2, l))
        xs = moe_combine(x1, y_moe, tables[3], routed, g2, rows, seq)

    zero = jnp.zeros((1, 1, d), F32)
    out = norm_mod(xs, final_norm_g.reshape(1, d), zero, zero, t_lat, seq, F32)
    return out.reshape(batch, seq, d)
```

```python
import functools

import jax
import jax.numpy as jnp
import numpy as np
from jax import lax
from jax.experimental import pallas as pl
from jax.experimental.pallas import tpu as pltpu

F32 = jnp.float32
BF16 = jnp.bfloat16
SDS = jax.ShapeDtypeStruct
BS = pl.BlockSpec

EPS = 1e-6
NEG_INF = -1e30
HEAD_DIM = 128
GRID_W = 64
NA_ROWS = 8
NA_COLS = 16
NA_QROWS = 8
NA_KROWS = NA_QROWS + NA_ROWS
RET_DK = 128
RET_DV = 256
CHUNK = 128
RET_CHUNKS_PER_STEP = 16
ROPE_BASE = 10000.0
MOE_GROUPS = 4
MOE_EPG = 8
MOE_EXPERTS = MOE_GROUPS * MOE_EPG
MOE_TOPK = 2
ROUTER_LANES = 128
N_MIX_HEADS = 8
GM_WIDTH = N_MIX_HEADS * HEAD_DIM
NA_WIDTH = N_MIX_HEADS * HEAD_DIM
RET_QK_WIDTH = N_MIX_HEADS * RET_DK
RET_V_WIDTH = N_MIX_HEADS * RET_DV

V7X_VMEM_BYTES = 64 * 1024 * 1024
VMEM_LIMIT = V7X_VMEM_BYTES - 8 * 1024 * 1024

ROW_TILE = 512
COL_TILE = 1024
MOE_ROWS = 256
COMBINE_ROWS = 128


def _params(*sem):
    return pltpu.CompilerParams(dimension_semantics=sem, vmem_limit_bytes=VMEM_LIMIT)


def _col_offsets(d):
    off, out = 0, {}
    for name, width in (("gate_a", d), ("gate_b", d), ("gate_c", d), ("gm_u", GM_WIDTH), ("gm_v", GM_WIDTH),
                        ("na_q", NA_WIDTH), ("na_k", NA_WIDTH), ("na_v", NA_WIDTH),
                        ("ret_q", RET_QK_WIDTH), ("ret_k", RET_QK_WIDTH), ("ret_v", RET_V_WIDTH),
                        ("ret_g", RET_V_WIDTH)):
        out[name] = off
        off += width
    return out, off


def _dot(a, b):
    return jnp.dot(a, b, preferred_element_type=F32)


def _dot_nt(a, b):
    return lax.dot_general(a, b, (((1,), (1,)), ((), ())), preferred_element_type=F32)


_HI16 = 0xFFFF0000


def _pack_halves(v):
    n = v.shape[1] // 2
    bits = lax.bitcast_convert_type(v.astype(BF16).astype(F32), jnp.uint32)
    return lax.shift_right_logical(bits[:, :n], jnp.uint32(16)) | (bits[:, n:] & jnp.uint32(_HI16))


def _unpack_halves(u):
    lo = lax.bitcast_convert_type(lax.shift_left(u, jnp.uint32(16)), F32)
    hi = lax.bitcast_convert_type(u & jnp.uint32(_HI16), F32)
    return lo, hi


def _norm_mod(x_ref, g_ref, sh_ref, sc_ref):
    x = x_ref[...]
    ms = jnp.mean(x * x, axis=-1, keepdims=True)
    y = x * lax.rsqrt(ms + EPS) * g_ref[...]
    return y * (1.0 + sc_ref[...]) + sh_ref[...]


def _norm_kernel(x_ref, g_ref, sh_ref, sc_ref, o_ref):
    o_ref[...] = _norm_mod(x_ref, g_ref, sh_ref, sc_ref).astype(o_ref.dtype)


def _first_lane(mask, lane):
    return jnp.min(jnp.where(mask, lane, float(ROUTER_LANES)), axis=-1, keepdims=True)


def _route(logits, run_ref):
    n = logits.shape[0]
    lane = lax.broadcasted_iota(jnp.int32, logits.shape, 1).astype(F32)
    neg = -jnp.inf
    is_g = lane < MOE_GROUPS
    gl = jnp.where(is_g, logits, neg)
    eg = jnp.exp(gl - jnp.max(gl, axis=-1, keepdims=True))
    pg = eg / jnp.sum(eg, axis=-1, keepdims=True)
    p_sel = jnp.max(pg, axis=-1, keepdims=True)
    grp = _first_lane((pg == p_sel) & is_g, lane)
    lo = MOE_GROUPS + MOE_EPG * grp
    is_e = (lane >= lo) & (lane < lo + MOE_EPG)
    el = jnp.where(is_e, logits, neg)
    ee = jnp.exp(el - jnp.max(el, axis=-1, keepdims=True))
    pe = ee / jnp.sum(ee, axis=-1, keepdims=True)
    t1 = jnp.max(pe, axis=-1, keepdims=True)
    i1 = _first_lane((pe == t1) & is_e, lane)
    rest = jnp.where(is_e & (lane != i1), pe, -1.0)
    t2 = jnp.max(rest, axis=-1, keepdims=True)
    i2 = _first_lane(rest == t2, lane)
    tsum = t1 + t2
    w1, w2 = p_sel * t1 / tsum, p_sel * t2 / tsum
    e1, e2 = i1 - MOE_GROUPS, i2 - MOE_GROUPS
    oh1, oh2 = lane == e1, lane == e2
    cnt = jnp.where(oh1 | oh2, 1.0, 0.0)
    r_i = lax.broadcasted_iota(jnp.int32, (n, n), 0)
    c_i = lax.broadcasted_iota(jnp.int32, (n, n), 1)
    tri = jnp.where(c_i < r_i, 1.0, 0.0).astype(BF16)
    before = _dot(tri, cnt.astype(BF16)) + run_ref[...]
    r1 = jnp.sum(jnp.where(oh1, before, 0.0), axis=-1, keepdims=True)
    r2 = jnp.sum(jnp.where(oh2, before, 0.0), axis=-1, keepdims=True)
    run_ref[...] = run_ref[...] + jnp.sum(cnt, axis=0, keepdims=True)
    slab = jnp.zeros(logits.shape, F32)
    for k, v in enumerate((e1, e2, r1, r2, w1, w2)):
        slab = jnp.where(lane == float(k), v, slab)
    return slab


def _norm_router_kernel(x_ref, g_ref, sh_ref, sc_ref, wr_ref, h_ref, rt_ref, cnt_ref, run_ref):
    @pl.when(pl.program_id(0) == 0)
    def _():
        run_ref[...] = jnp.zeros_like(run_ref)
    h = _norm_mod(x_ref, g_ref, sh_ref, sc_ref)
    h_ref[...] = _pack_halves(h)
    rt_ref[...] = _route(_dot(h.astype(BF16), wr_ref[...]), run_ref)
    cnt_ref[...] = run_ref[...]


def _seg_map(seg_rows, tile, nseg):
    per = seg_rows // tile
    return lambda i: jnp.minimum(i // per, nseg - 1)


def norm_mod(x, g, shift, scale, rows, seg_rows, out_dtype):
    d = x.shape[1]
    seg = _seg_map(seg_rows, ROW_TILE // 2, shift.shape[0])
    tr = ROW_TILE // 2
    return pl.pallas_call(
        _norm_kernel, out_shape=SDS((rows, d), out_dtype), grid=(rows // tr,),
        in_specs=[BS((tr, d), lambda i: (i, 0)), BS((1, d), lambda i: (0, 0)),
                  BS((None, 1, d), lambda i: (seg(i), 0, 0)), BS((None, 1, d), lambda i: (seg(i), 0, 0))],
        out_specs=BS((tr, d), lambda i: (i, 0)),
        compiler_params=_params("parallel"), name="norm_mod")(x, g, shift, scale)


def norm_router(x, g, shift, scale, w_router, rows, seg_rows):
    d = x.shape[1]
    tr = ROW_TILE // 2
    seg = _seg_map(seg_rows, tr, shift.shape[0])
    return pl.pallas_call(
        _norm_router_kernel,
        out_shape=(SDS((rows, d // 2), jnp.uint32), SDS((rows, ROUTER_LANES), F32),
                   SDS((1, ROUTER_LANES), F32)),
        grid=(rows // tr,),
        in_specs=[BS((tr, d), lambda i: (i, 0)), BS((1, d), lambda i: (0, 0)),
                  BS((None, 1, d), lambda i: (seg(i), 0, 0)), BS((None, 1, d), lambda i: (seg(i), 0, 0)),
                  BS((d, ROUTER_LANES), lambda i: (0, 0))],
        out_specs=(BS((tr, d // 2), lambda i: (i, 0)), BS((tr, ROUTER_LANES), lambda i: (i, 0)),
                   BS((1, ROUTER_LANES), lambda i: (0, 0))),
        scratch_shapes=[pltpu.VMEM((1, ROUTER_LANES), F32)],
        compiler_params=_params("arbitrary"), name="norm_router")(x, g, shift, scale, w_router)


def _ada_kernel(a_ref, w_ref, b_ref, o_ref):
    o_ref[...] = _dot(a_ref[...], w_ref[...].astype(BF16)) + b_ref[...]


def ada_modulation(act, ada_w, ada_b):
    depth, d, n = ada_w.shape
    tn = ROW_TILE
    return pl.pallas_call(
        _ada_kernel, out_shape=SDS((depth, act.shape[0], n), F32), grid=(depth, n // tn),
        in_specs=[BS(act.shape, lambda l, j: (0, 0)), BS((None, d, tn), lambda l, j: (l, 0, j)),
                  BS((None, 1, tn), lambda l, j: (l, 0, j))],
        out_specs=BS((None, act.shape[0], tn), lambda l, j: (l, 0, j)),
        compiler_params=_params("parallel", "parallel"), name="ada_modulation")(
            act, ada_w, ada_b.reshape(depth, 1, n))


def _cast_kernel(x_ref, o_ref):
    o_ref[...] = x_ref[...].astype(o_ref.dtype)


def cast_layer_bf16(w, layer):
    _, e, r, c = w.shape
    return pl.pallas_call(
        _cast_kernel, out_shape=SDS((e, r, c), BF16), grid=(e,),
        in_specs=[BS((None, None, r, c), lambda i: (layer, i, 0, 0))],
        out_specs=BS((None, r, c), lambda i: (i, 0, 0)),
        compiler_params=_params("parallel"), name="cast_bf16")(w)


def _in_proj_kernel(a_ref, b_ref, o_ref, bq_ref):
    @pl.when(pl.program_id(1) == 0)
    def _():
        bq_ref[...] = b_ref[...].astype(BF16)
    o_ref[...] = _dot(a_ref[...], bq_ref[...]).astype(o_ref.dtype)


def in_proj(a, w, layer, rows, out_dtype):
    _, k, n = w.shape
    tm, tn = ROW_TILE, COL_TILE
    return pl.pallas_call(
        _in_proj_kernel, out_shape=SDS((rows, n), out_dtype), grid=(n // tn, rows // tm),
        in_specs=[BS((tm, k), lambda j, i: (i, 0)),
                  BS((None, k, tn), lambda j, i: (layer, 0, j), pipeline_mode=pl.Buffered(1))],
        out_specs=BS((tm, tn), lambda j, i: (i, j)),
        scratch_shapes=[pltpu.VMEM((k, tn), BF16)],
        compiler_params=_params("parallel", "arbitrary"), name="in_proj")(a, w)


def _mm_res_kernel(a_ref, b_ref, x_ref, g_ref, o_ref):
    o_ref[...] = x_ref[...] + g_ref[...] * _dot(a_ref[...], b_ref[...])


def matmul_residual(a, b, x, gate, rows, seg_rows):
    k, n = b.shape
    tm, tn = ROW_TILE, COL_TILE
    seg = _seg_map(seg_rows, tm, gate.shape[0])
    return pl.pallas_call(
        _mm_res_kernel, out_shape=SDS((rows, n), F32), grid=(n // tn, rows // tm),
        in_specs=[BS((tm, k), lambda j, i: (i, 0)), BS((k, tn), lambda j, i: (0, j)),
                  BS((tm, tn), lambda j, i: (i, j)), BS((None, 1, tn), lambda j, i: (seg(i), 0, j))],
        out_specs=BS((tm, tn), lambda j, i: (i, j)),
        compiler_params=_params("parallel", "parallel"), name="out_proj")(a, b, x, gate)


def _merge_kernel(a_ref, b_ref, r_ref, ga_ref, gb_ref, gc_ref, wa_ref, wb_ref, wc_ref, o_ref):
    def branch(x_ref, w_ref, g_ref):
        return jax.nn.sigmoid(g_ref[...].astype(F32)) * _dot(x_ref[...], w_ref[...])
    y = branch(a_ref, wa_ref, ga_ref) + branch(b_ref, wb_ref, gb_ref) + branch(r_ref, wc_ref, gc_ref)
    o_ref[...] = y.astype(o_ref.dtype)


def merge_branches(z, a, bb, r, w_a, w_b, w_c, rows):
    d = w_a.shape[1]
    tm, tn = ROW_TILE, COL_TILE
    nb = d // tn
    row_spec = lambda arr: BS((tm, arr.shape[1]), lambda j, i: (i, 0))
    w_spec = lambda arr: BS((arr.shape[0], tn), lambda j, i: (0, j))
    gate_spec = lambda g: BS((tm, tn), lambda j, i: (i, g * nb + j))
    return pl.pallas_call(
        _merge_kernel, out_shape=SDS((rows, d), BF16), grid=(nb, rows // tm),
        in_specs=[row_spec(a), row_spec(bb), row_spec(r), gate_spec(0), gate_spec(1), gate_spec(2),
                  w_spec(w_a), w_spec(w_b), w_spec(w_c)],
        out_specs=BS((tm, tn), lambda j, i: (i, j)),
        compiler_params=_params("parallel", "parallel"), name="merge_branches")(
            a, bb, r, z, z, z, w_a, w_b, w_c)


GM_CHUNKS_PER_STEP = 4


def _gmlp_kernel(u_ref, v_ref, ng_ref, ws_ref, bs_ref, o_ref):
    for c in range(GM_CHUNKS_PER_STEP):
        rows = slice(c * CHUNK, (c + 1) * CHUNK)
        v = jax.nn.gelu(v_ref[rows, :].astype(F32))
        mu = jnp.mean(v, axis=-1, keepdims=True)
        var = jnp.mean(jnp.square(v - mu), axis=-1, keepdims=True)
        vn = ((v - mu) * lax.rsqrt(var + EPS) * ng_ref[...]).astype(BF16)
        for g in range(N_MIX_HEADS):
            cols = slice(g * HEAD_DIM, (g + 1) * HEAD_DIM)
            mixed = _dot(ws_ref[g], vn[:, cols]) + bs_ref[g]
            u = jax.nn.gelu(u_ref[rows, cols].astype(F32))
            o_ref[rows, cols] = (u * mixed).astype(o_ref.dtype)


def chunk_gmlp(z, cols, norm_g, ws, bs, rows):
    tr = GM_CHUNKS_PER_STEP * CHUNK
    ub, vb = cols["gm_u"] // GM_WIDTH, cols["gm_v"] // GM_WIDTH
    bs_b = jnp.broadcast_to(bs[:, :, None], bs.shape + (HEAD_DIM,)).astype(F32)
    return pl.pallas_call(
        _gmlp_kernel, out_shape=SDS((rows, GM_WIDTH), BF16), grid=(rows // tr,),
        in_specs=[BS((tr, GM_WIDTH), lambda i: (i, ub)), BS((tr, GM_WIDTH), lambda i: (i, vb)),
                  BS((1, GM_WIDTH), lambda i: (0, 0)), BS(ws.shape, lambda i: (0, 0, 0)),
                  BS(bs_b.shape, lambda i: (0, 0, 0))],
        out_specs=BS((tr, GM_WIDTH), lambda i: (i, 0)),
        compiler_params=_params("parallel"), name="chunk_gmlp")(
            z, z, norm_g.reshape(1, GM_WIDTH), ws.astype(BF16), bs_b)


def _na_bias_tables(rpb, grid_rows):
    nq = grid_rows // NA_QROWS
    depth, n_heads = rpb.shape[:2]
    n_dr = 2 * NA_ROWS - 1
    i = np.arange(NA_QROWS)[:, None]
    m = np.arange(NA_KROWS)[None, :]
    c = np.arange(GRID_W)[:, None]
    w = np.arange(GRID_W)[None, :]
    c_start = np.clip(c - NA_COLS // 2, 0, GRID_W - NA_COLS)
    col_ok = (w >= c_start) & (w < c_start + NA_COLS)
    dc = np.clip(w - c + NA_COLS - 1, 0, 2 * NA_COLS - 2)
    col_sel = (dc.reshape(-1)[None, :] == np.arange(2 * NA_COLS - 1)[:, None]).astype(np.float32)
    band = jnp.einsum("lhab,bq->lhaq", rpb.astype(F32), col_sel, precision=lax.Precision.HIGHEST)
    band = jnp.where(col_ok[None, None, None], band.reshape(depth, n_heads, n_dr, GRID_W, GRID_W), NEG_INF)
    band = jnp.concatenate([band, jnp.full((depth, n_heads, 1, GRID_W, GRID_W), NEG_INF, F32)], axis=2)
    slots = []
    for j in (0, 1, nq - 1):
        base = int(np.clip(NA_QROWS * j - NA_ROWS // 2, 0, grid_rows - NA_KROWS))
        r = NA_QROWS * j + i
        kr = base + m
        r_start = np.clip(r - NA_ROWS // 2, 0, grid_rows - NA_ROWS)
        row_ok = (kr >= r_start) & (kr < r_start + NA_ROWS)
        slots.append(np.where(row_ok, kr - r + NA_ROWS - 1, n_dr))
    slots = jnp.asarray(np.stack(slots).reshape(-1), jnp.int32)
    grid_spec = pltpu.PrefetchScalarGridSpec(
        num_scalar_prefetch=1, grid=(depth, 3, n_heads),
        in_specs=[BS((None, None, n_dr + 1, GRID_W, GRID_W), lambda l, k, h, s: (l, h, 0, 0, 0))],
        out_specs=BS((None, None, None, NA_QROWS * GRID_W, NA_KROWS * GRID_W), lambda l, k, h, s: (l, k, h, 0, 0)))
    return pl.pallas_call(
        _na_bias_kernel, grid_spec=grid_spec,
        out_shape=SDS((depth, 3, n_heads, NA_QROWS * GRID_W, NA_KROWS * GRID_W), F32),
        compiler_params=_params("parallel", "parallel", "parallel"), name="na_bias_tables")(slots, band)


def _na_bias_kernel(slot_ref, band_ref, o_ref):
    kind = pl.program_id(1)
    for i in range(NA_QROWS):
        tiles = [band_ref[slot_ref[(kind * NA_QROWS + i) * NA_KROWS + m]] for m in range(NA_KROWS)]
        o_ref[i * GRID_W:(i + 1) * GRID_W, :] = jnp.concatenate(tiles, axis=1)


def _na_kernel(q_ref, k_ref, v_ref, kc_ref, vc_ref, bias_ref, o_ref, *, grid_rows):
    j = pl.program_id(2)
    nq = grid_rows // NA_QROWS
    kind = jnp.where(j == 0, 0, jnp.where(j == nq - 1, 2, 1))
    base_row = jnp.clip(NA_QROWS * j - NA_ROWS // 2, 0, grid_rows - NA_KROWS)
    base = pl.multiple_of(base_row * GRID_W, (NA_ROWS // 2) * GRID_W)
    nk = NA_KROWS * GRID_W
    q = (q_ref[...].astype(F32) * (HEAD_DIM ** -0.5)).astype(BF16)
    s_loc = _dot_nt(q, k_ref[pl.ds(base, nk), :]) + bias_ref[kind]
    s_ctx = _dot_nt(q, kc_ref[...])
    mx = jnp.maximum(jnp.max(s_loc, axis=-1, keepdims=True), jnp.max(s_ctx, axis=-1, keepdims=True))
    e_loc = jnp.exp(s_loc - mx)
    e_ctx = jnp.exp(s_ctx - mx)
    den = jnp.sum(e_loc, axis=-1, keepdims=True) + jnp.sum(e_ctx, axis=-1, keepdims=True)
    o = _dot(e_loc.astype(BF16), v_ref[pl.ds(base, nk), :]) + _dot(e_ctx.astype(BF16), vc_ref[...])
    o_ref[...] = (o / den).astype(o_ref.dtype)


def neighbourhood_attention(z, cols, bias, layer, batch, seq, n_ctx):
    grid_rows = seq // GRID_W
    assert grid_rows % NA_QROWS == 0 and grid_rows >= 2 * NA_KROWS
    nq = grid_rows // NA_QROWS
    qb = NA_QROWS * GRID_W
    qc, kc, vc = (cols[n] // HEAD_DIM for n in ("na_q", "na_k", "na_v"))
    ctx_blk0 = batch * seq // n_ctx
    return pl.pallas_call(
        functools.partial(_na_kernel, grid_rows=grid_rows),
        out_shape=SDS((batch * seq, NA_WIDTH), BF16), grid=(N_MIX_HEADS, batch, nq),
        in_specs=[BS((qb, HEAD_DIM), lambda h, b, j: (b * nq + j, qc + h)),
                  BS((seq, HEAD_DIM), lambda h, b, j: (b, kc + h)),
                  BS((seq, HEAD_DIM), lambda h, b, j: (b, vc + h)),
                  BS((n_ctx, HEAD_DIM), lambda h, b, j: (ctx_blk0 + b, kc + h)),
                  BS((n_ctx, HEAD_DIM), lambda h, b, j: (ctx_blk0 + b, vc + h)),
                  BS((None, 3, None, qb, NA_KROWS * GRID_W), lambda h, b, j: (layer, 0, h, 0, 0))],
        out_specs=BS((qb, HEAD_DIM), lambda h, b, j: (b * nq + j, h)),
        compiler_params=_params("parallel", "parallel", "parallel"), name="neighbourhood_attention")(
            z, z, z, z, z, bias)


def _ctx_attn_kernel(q_ref, k_ref, v_ref, o_ref):
    q = (q_ref[...].astype(F32) * (HEAD_DIM ** -0.5)).astype(BF16)
    s = _dot_nt(q, k_ref[...])
    e = jnp.exp(s - jnp.max(s, axis=-1, keepdims=True))
    o = _dot(e.astype(BF16), v_ref[...]) / jnp.sum(e, axis=-1, keepdims=True)
    o_ref[...] = o.astype(o_ref.dtype)


def context_attention(z, cols, batch, seq, n_ctx):
    qc, kc, vc = (cols[n] // HEAD_DIM for n in ("na_q", "na_k", "na_v"))
    blk0 = batch * seq // n_ctx
    spec = lambda col: BS((n_ctx, HEAD_DIM), lambda b, h: (blk0 + b, col + h))
    return pl.pallas_call(
        _ctx_attn_kernel, out_shape=SDS((batch * n_ctx, NA_WIDTH), BF16), grid=(batch, N_MIX_HEADS),
        in_specs=[spec(qc), spec(kc), spec(vc)],
        out_specs=BS((n_ctx, HEAD_DIM), lambda b, h: (b, h)),
        compiler_params=_params("parallel", "parallel"), name="context_attention")(z, z, z)


def _rope(t_ref, cos_ref, sin_ref):
    t = t_ref[...].astype(F32)
    lane = lax.broadcasted_iota(jnp.int32, t.shape, 1)
    quarter = HEAD_DIM // 4
    partner = jnp.where((lane % (2 * quarter)) < quarter,
                        pltpu.roll(t, HEAD_DIM - quarter, 1), pltpu.roll(t, quarter, 1))
    return t * cos_ref[...] + partner * sin_ref[...]


def _rope_tables(n, identity):
    if identity:
        return jnp.ones((n, HEAD_DIM), F32), jnp.zeros((n, HEAD_DIM), F32)
    nf = HEAD_DIM // 4
    pos = jnp.arange(n)
    p_row = (pos // GRID_W).astype(F32)
    p_col = (pos % GRID_W).astype(F32)
    inv = ROPE_BASE ** (-jnp.arange(nf, dtype=F32) / nf)
    a_row = p_row[:, None] * inv[None, :]
    a_col = p_col[:, None] * inv[None, :]
    cos = jnp.concatenate([jnp.cos(a_row), jnp.cos(a_row), jnp.cos(a_col), jnp.cos(a_col)], axis=-1)
    sin = jnp.concatenate([-jnp.sin(a_row), jnp.sin(a_row), -jnp.sin(a_col), jnp.sin(a_col)], axis=-1)
    return cos, sin


def _decay_tables(dec_f, dec_b):
    log_gf = jnp.log1p(-jnp.exp2(dec_f.astype(F32)))
    log_gb = jnp.log1p(-jnp.exp2(dec_b.astype(F32)))
    idx = jnp.arange(CHUNK, dtype=F32)
    rel = idx[:, None] - idx[None, :]
    d_f = jnp.where(rel >= 0, jnp.exp(log_gf[:, None, None] * jnp.maximum(rel, 0.0)), 0.0)
    d_b = jnp.where(rel < 0, jnp.exp(log_gb[:, None, None] * jnp.maximum(-rel, 0.0)), 0.0)
    bcast = lambda v: jnp.broadcast_to(v[:, :, None], v.shape + (RET_DK,))
    return dict(
        log_gf=log_gf, log_gb=log_gb,
        dfb=d_f + d_b,
        qdf=bcast(jnp.exp(log_gf[:, None] * (idx + 1.0))),
        qdb=bcast(jnp.exp(log_gb[:, None] * (CHUNK - idx))),
        kdf=bcast(jnp.exp(log_gf[:, None] * (CHUNK - 1.0 - idx))),
        kdb=bcast(jnp.exp(log_gb[:, None] * idx)),
        cdf=jnp.broadcast_to(jnp.exp(log_gf * CHUNK)[:, None, None], (log_gf.shape[0], 1, RET_DV)),
        cdb=jnp.broadcast_to(jnp.exp(log_gb * CHUNK)[:, None, None], (log_gb.shape[0], 1, RET_DV)),
    )


def _ctx_state_kernel(k_ref, v_ref, wf_ref, wb_ref, sf_ref, sb_ref):
    k = k_ref[...].astype(F32) * (RET_DK ** -0.5)
    v = v_ref[...]
    sf_ref[...] = _dot((k * wf_ref[...]).T.astype(BF16), v)
    sb_ref[...] = _dot((k * wb_ref[...]).T.astype(BF16), v)


def context_state(z, cols, tabs, batch, seq, n_ctx):
    pos = jnp.arange(n_ctx, dtype=F32)
    bcast = lambda v: jnp.broadcast_to(v[:, :, None], v.shape + (RET_DK,))
    w_f = bcast(jnp.exp(tabs["log_gf"][:, None] * ((n_ctx - 1.0) - pos)))
    w_b = bcast(jnp.exp(tabs["log_gb"][:, None] * pos))
    kc, vc = cols["ret_k"] // RET_DK, cols["ret_v"] // RET_DV
    blk0 = batch * seq // n_ctx
    out = SDS((batch, N_MIX_HEADS, RET_DK, RET_DV), F32)
    tab_spec = BS((None, n_ctx, RET_DK), lambda b, h: (h, 0, 0))
    out_spec = BS((None, None, RET_DK, RET_DV), lambda b, h: (b, h, 0, 0))
    return pl.pallas_call(
        _ctx_state_kernel, out_shape=(out, out), grid=(batch, N_MIX_HEADS),
        in_specs=[BS((n_ctx, RET_DK), lambda b, h: (blk0 + b, kc + h)),
                  BS((n_ctx, RET_DV), lambda b, h: (blk0 + b, vc + h)), tab_spec, tab_spec],
        out_specs=(out_spec, out_spec),
        compiler_params=_params("parallel", "parallel"), name="context_state")(z, z, w_f, w_b)


def _ret_state_kernel(kf_ref, vf_ref, cosf_ref, sinf_ref, kb_ref, vb_ref, cosb_ref, sinb_ref,
                      kdf_ref, kdb_ref, cdf_ref, cdb_ref, s0f_ref, s0b_ref, sf_ref, sb_ref, st_ref, *, cpg):
    @pl.when(pl.program_id(2) == 0)
    def _():
        st_ref[0] = s0f_ref[...]
        st_ref[1] = s0b_ref[...]

    def scan(k_ref, v_ref, cos_ref, sin_ref, kd_ref, cd_ref, out_ref, slot, order):
        kr = _rope(k_ref, cos_ref, sin_ref) * (RET_DK ** -0.5)
        s = st_ref[slot]
        for c in order:
            rows = slice(c * CHUNK, (c + 1) * CHUNK)
            out_ref[c] = s.astype(out_ref.dtype)
            kc = (kr[rows, :] * kd_ref[...]).T.astype(BF16)
            s = s * cd_ref[...] + _dot(kc, v_ref[rows, :])
        st_ref[slot] = s

    scan(kf_ref, vf_ref, cosf_ref, sinf_ref, kdf_ref, cdf_ref, sf_ref, 0, range(cpg))
    scan(kb_ref, vb_ref, cosb_ref, sinb_ref, kdb_ref, cdb_ref, sb_ref, 1, reversed(range(cpg)))


def _ret_out_kernel(q_ref, k_ref, v_ref, g_ref, cos_ref, sin_ref, sf_ref, sb_ref, dfb_ref, qdf_ref, qdb_ref,
                    o_ref, *, cpg):
    qr = _rope(q_ref, cos_ref, sin_ref)
    kr = (_rope(k_ref, cos_ref, sin_ref) * (RET_DK ** -0.5)).astype(BF16)
    for c in range(cpg):
        rows = slice(c * CHUNK, (c + 1) * CHUNK)
        qc = qr[rows, :]
        p = (_dot_nt(qc.astype(BF16), kr[rows, :]) * dfb_ref[...]).astype(BF16)
        o = (_dot(p, v_ref[rows, :]) + _dot((qc * qdf_ref[...]).astype(BF16), sf_ref[c])
             + _dot((qc * qdb_ref[...]).astype(BF16), sb_ref[c]))
        mu = jnp.mean(o, axis=-1, keepdims=True)
        var = jnp.mean(jnp.square(o - mu), axis=-1, keepdims=True)
        on = (o - mu) * lax.rsqrt(var + EPS)
        g = g_ref[rows, :].astype(F32)
        o_ref[rows, :] = (on * (g * jax.nn.sigmoid(g))).astype(o_ref.dtype)


def retention(z, cols, tabs, s0f, s0b, row0, n, batch, rope_identity):
    nc = n // CHUNK
    cpg = min(RET_CHUNKS_PER_STEP, nc)
    ng = nc // cpg
    gr = cpg * CHUNK
    blk0 = row0 // gr
    qc, kc = cols["ret_q"] // RET_DK, cols["ret_k"] // RET_DK
    vc, gc = cols["ret_v"] // RET_DV, cols["ret_g"] // RET_DV
    cos, sin = _rope_tables(n, rope_identity)
    fwd = lambda b, h, g: blk0 + b * ng + g
    bwd = lambda b, h, g: blk0 + b * ng + (ng - 1 - g)
    head_tab = lambda w: BS((None, CHUNK, w), lambda b, h, g: (h, 0, 0))
    c_tab = BS((None, 1, RET_DV), lambda b, h, g: (h, 0, 0))
    s0_spec = BS((None, None, RET_DK, RET_DV), lambda b, h, g: (b, h, 0, 0))
    st_shape = SDS((batch, N_MIX_HEADS, nc, RET_DK, RET_DV), BF16)
    st_spec = lambda rev: BS((None, None, cpg, RET_DK, RET_DV),
                             lambda b, h, g: (b, h, (ng - 1 - g) if rev else g, 0, 0))
    sf, sb = pl.pallas_call(
        functools.partial(_ret_state_kernel, cpg=cpg), out_shape=(st_shape, st_shape),
        grid=(batch, N_MIX_HEADS, ng),
        in_specs=[BS((gr, RET_DK), lambda b, h, g: (fwd(b, h, g), kc + h)),
                  BS((gr, RET_DV), lambda b, h, g: (fwd(b, h, g), vc + h)),
                  BS((gr, RET_DK), lambda b, h, g: (g, 0)), BS((gr, RET_DK), lambda b, h, g: (g, 0)),
                  BS((gr, RET_DK), lambda b, h, g: (bwd(b, h, g), kc + h)),
                  BS((gr, RET_DV), lambda b, h, g: (bwd(b, h, g), vc + h)),
                  BS((gr, RET_DK), lambda b, h, g: (ng - 1 - g, 0)),
                  BS((gr, RET_DK), lambda b, h, g: (ng - 1 - g, 0)),
                  head_tab(RET_DK), head_tab(RET_DK), c_tab, c_tab, s0_spec, s0_spec],
        out_specs=(st_spec(False), st_spec(True)),
        scratch_shapes=[pltpu.VMEM((2, RET_DK, RET_DV), F32)],
        compiler_params=_params("parallel", "parallel", "arbitrary"), name="retention_state")(
            z, z, cos, sin, z, z, cos, sin, tabs["kdf"], tabs["kdb"], tabs["cdf"], tabs["cdb"], s0f, s0b)
    return pl.pallas_call(
        functools.partial(_ret_out_kernel, cpg=cpg), out_shape=SDS((batch * n, RET_V_WIDTH), BF16),
        grid=(batch, N_MIX_HEADS, ng),
        in_specs=[BS((gr, RET_DK), lambda b, h, g: (fwd(b, h, g), qc + h)),
                  BS((gr, RET_DK), lambda b, h, g: (fwd(b, h, g), kc + h)),
                  BS((gr, RET_DV), lambda b, h, g: (fwd(b, h, g), vc + h)),
                  BS((gr, RET_DV), lambda b, h, g: (fwd(b, h, g), gc + h)),
                  BS((gr, RET_DK), lambda b, h, g: (g, 0)), BS((gr, RET_DK), lambda b, h, g: (g, 0)),
                  st_spec(False), st_spec(False),
                  head_tab(CHUNK), head_tab(RET_DK), head_tab(RET_DK)],
        out_specs=BS((gr, RET_DV), lambda b, h, g: (b * ng + g, h)),
        compiler_params=_params("parallel", "parallel", "parallel"), name="retention_out")(
            z, z, z, z, cos, sin, sf, sb, tabs["dfb"], tabs["qdf"], tabs["qdb"])


def _dispatch_tables(routed, counts):
    t = routed.shape[0]
    n_assign = t * MOE_TOPK
    flat_e = routed[:, 0:MOE_TOPK].astype(jnp.int32).reshape(-1)
    rank = routed[:, MOE_TOPK:2 * MOE_TOPK].astype(jnp.int32).reshape(-1)
    flat_t = jnp.repeat(jnp.arange(t, dtype=jnp.int32), MOE_TOPK)
    counts = counts[0, :MOE_EXPERTS].astype(jnp.int32)
    padded = (counts + MOE_ROWS - 1) // MOE_ROWS * MOE_ROWS
    p_end = jnp.cumsum(padded)
    dest = (p_end - padded)[flat_e] + rank
    n_blocks = (n_assign + MOE_EXPERTS * (MOE_ROWS - 1)) // MOE_ROWS
    row_tok = jnp.zeros((n_blocks * MOE_ROWS,), jnp.int32).at[dest].set(flat_t)
    blk_start = jnp.arange(n_blocks, dtype=jnp.int32) * MOE_ROWS
    block_e = jnp.minimum(jnp.sum((p_end[None, :] <= blk_start[:, None]).astype(jnp.int32), axis=1),
                          MOE_EXPERTS - 1)
    n_used = (p_end[-1] // MOE_ROWS).astype(jnp.int32).reshape(1)
    return block_e, n_used, row_tok.reshape(n_blocks, 1, MOE_ROWS), dest.reshape(t, MOE_TOPK).astype(jnp.int32)


def _start_row_gather(src_hbm, idx_ref, idx_of_row, dst, sem, n):
    def issue(r, carry):
        pltpu.make_async_copy(src_hbm.at[pl.ds(idx_ref[0, idx_of_row(r)], 1)], dst.at[pl.ds(r, 1)], sem).start()
        return carry
    lax.fori_loop(0, n, issue, 0, unroll=8)


def _wait_row_gather(src_hbm, dst, sem):
    pltpu.make_async_copy(src_hbm.at[pl.ds(0, dst.shape[0])], dst, sem).wait()


def _moe_kernel(be_ref, nu_ref, tok_ref, tok_next_ref, h_hbm, w1_ref, w3_ref, w2_ref, y_ref, xb, sem):
    i = pl.program_id(0)
    n_used = nu_ref[0]
    slot = i % 2
    row = lambda r: r

    @pl.when((i == 0) & (n_used > 0))
    def _():
        _start_row_gather(h_hbm, tok_ref, row, xb.at[0], sem.at[0], MOE_ROWS)

    @pl.when(i + 1 < n_used)
    def _():
        _start_row_gather(h_hbm, tok_next_ref, row, xb.at[1 - slot], sem.at[1 - slot], MOE_ROWS)

    @pl.when(i < n_used)
    def _():
        _wait_row_gather(h_hbm, xb.at[slot], sem.at[slot])
        half = xb.shape[2]
        x_lo, x_hi = (v.astype(BF16) for v in _unpack_halves(xb[slot]))
        up = lambda w_ref: _dot(x_lo, w_ref[0:half, :]) + _dot(x_hi, w_ref[half:2 * half, :])
        h1 = up(w1_ref)
        h3 = up(w3_ref)
        hm = (h1 * jax.nn.sigmoid(h1) * h3).astype(BF16)
        y_ref[...] = _pack_halves(_dot(hm, w2_ref[...]))

    @pl.when(i >= n_used)
    def _():
        y_ref[...] = jnp.zeros_like(y_ref)


def moe_experts(h2, tables, w1, w3, w2):
    block_e, n_used, row_tok, _ = tables
    n_blocks = row_tok.shape[0]
    d, hid = w1.shape[1], w1.shape[2]
    assert h2.shape[1] == d // 2 and h2.dtype == jnp.uint32
    grid_spec = pltpu.PrefetchScalarGridSpec(
        num_scalar_prefetch=2, grid=(n_blocks,),
        in_specs=[BS((None, 1, MOE_ROWS), lambda i, be, nu: (i, 0, 0), memory_space=pltpu.SMEM),
                  BS((None, 1, MOE_ROWS), lambda i, be, nu: (jnp.minimum(i + 1, n_blocks - 1), 0, 0),
                     memory_space=pltpu.SMEM),
                  BS(memory_space=pl.ANY),
                  BS((None, d, hid), lambda i, be, nu: (be[i], 0, 0)),
                  BS((None, d, hid), lambda i, be, nu: (be[i], 0, 0)),
                  BS((None, hid, d), lambda i, be, nu: (be[i], 0, 0))],
        out_specs=BS((MOE_ROWS, d // 2), lambda i, be, nu: (i, 0)),
        scratch_shapes=[pltpu.VMEM((2, MOE_ROWS, d // 2), jnp.uint32), pltpu.SemaphoreType.DMA((2,))])
    return pl.pallas_call(
        _moe_kernel, out_shape=SDS((n_blocks * MOE_ROWS, d // 2), jnp.uint32), grid_spec=grid_spec,
        compiler_params=_params("arbitrary"), name="moe_experts")(
            block_e, n_used, row_tok, row_tok, h2, w1, w3, w2)


def _combine_kernel(pos_ref, pos_next_ref, y_hbm, x_ref, w_ref, g_ref, ng_ref, nsh_ref, nsc_ref, *refs,
                    keep_stream):
    outs, (yb, sem) = refs[:-2], refs[-2:]
    i = pl.program_id(0)
    tt = COMBINE_ROWS
    slot = i % 2
    idx = lambda r: r

    @pl.when(i == 0)
    def _():
        _start_row_gather(y_hbm, pos_ref, idx, yb.at[0], sem.at[0], MOE_TOPK * tt)

    @pl.when(i + 1 < pl.num_programs(0))
    def _():
        _start_row_gather(y_hbm, pos_next_ref, idx, yb.at[1 - slot], sem.at[1 - slot], MOE_TOPK * tt)

    _wait_row_gather(y_hbm, yb.at[slot], sem.at[slot])
    w = w_ref[...]
    w0, w1 = (w[:, 2 * MOE_TOPK + k:2 * MOE_TOPK + k + 1] for k in range(MOE_TOPK))
    half = yb.shape[2]
    halves = (slice(0, half), slice(half, 2 * half))
    ya = _unpack_halves(yb[slot, 0:tt, :])
    yb_ = _unpack_halves(yb[slot, tt:2 * tt, :])
    xn = [x_ref[:, s] + g_ref[:, s] * (a * w0 + b * w1) for s, a, b in zip(halves, ya, yb_)]
    if keep_stream:
        for s, v in zip(halves, xn):
            outs[0][:, s] = v
    h_ref = outs[-1]
    ms = sum(jnp.sum(v * v, axis=-1, keepdims=True) for v in xn) / (2 * half)
    inv = lax.rsqrt(ms + EPS)
    for s, v in zip(halves, xn):
        h_ref[:, s] = ((v * inv * ng_ref[:, s]) * (1.0 + nsc_ref[:, s]) + nsh_ref[:, s]).astype(h_ref.dtype)


def moe_combine(x1, y, pos, routed, gate, rows, seg_rows, norm_g, norm_shift, norm_scale, out_dtype, keep_stream):
    d = x1.shape[1]
    tt = COMBINE_ROWS
    nt = rows // tt
    seg = _seg_map(seg_rows, tt, gate.shape[0])
    nseg = _seg_map(seg_rows, tt, norm_shift.shape[0])
    pos3 = pos.reshape(nt, tt, MOE_TOPK).transpose(0, 2, 1).reshape(nt, 1, MOE_TOPK * tt)
    row_spec = BS((tt, d), lambda i: (i, 0))
    out_shape = ((SDS((rows, d), F32),) if keep_stream else ()) + (SDS((rows, d), out_dtype),)
    res = pl.pallas_call(
        functools.partial(_combine_kernel, keep_stream=keep_stream), out_shape=out_shape, grid=(nt,),
        in_specs=[BS((None, 1, MOE_TOPK * tt), lambda i: (i, 0, 0), memory_space=pltpu.SMEM),
                  BS((None, 1, MOE_TOPK * tt), lambda i: (jnp.minimum(i + 1, nt - 1), 0, 0),
                     memory_space=pltpu.SMEM),
                  BS(memory_space=pl.ANY), row_spec,
                  BS((tt, ROUTER_LANES), lambda i: (i, 0)), BS((None, 1, d), lambda i: (seg(i), 0, 0)),
                  BS((1, d), lambda i: (0, 0)), BS((None, 1, d), lambda i: (nseg(i), 0, 0)),
                  BS((None, 1, d), lambda i: (nseg(i), 0, 0))],
        out_specs=tuple(row_spec for _ in out_shape),
        scratch_shapes=[pltpu.VMEM((2, MOE_TOPK * tt, d // 2), jnp.uint32), pltpu.SemaphoreType.DMA((2,))],
        compiler_params=_params("arbitrary"), name="moe_combine")(
            pos3, pos3, y, x1, routed, gate, norm_g, norm_shift, norm_scale)
    return res if keep_stream else res[0]


def kernel(x, c, ctx, c_ctx, ada_w, ada_b, norm1_g, w_in, gm_norm_g, gm_ws, gm_bs, na_rpb, ret_decay_fwd,
           ret_decay_bwd, w_branch_a, w_branch_b, w_branch_c, w_out, norm2_g, moe_w_group, moe_w_expert,
           moe_w1, moe_w3, moe_w2, final_norm_g):
    batch, seq, d = x.shape
    n_ctx = ctx.shape[1]
    depth = ada_w.shape[0]
    t_lat, t_ctx = batch * seq, batch * n_ctx
    t_all = t_lat + t_ctx
    cols, in_total = _col_offsets(d)
    assert w_in.shape[2] == in_total and seq % ROW_TILE == 0 and t_ctx % ROW_TILE == 0
    assert 8 >= batch + 1

    cond = jnp.concatenate([c, c_ctx[None, :], jnp.zeros((8 - batch - 1, d), c.dtype)], axis=0)
    mod = ada_modulation(jax.nn.silu(cond).astype(BF16), ada_w, ada_b)[:, :batch + 1]
    mod = mod.reshape(depth, batch + 1, 6, 1, d)

    na_bias = _na_bias_tables(na_rpb, seq // GRID_W)
    xs = jnp.concatenate([x.reshape(t_lat, d), ctx.reshape(t_ctx, d)], axis=0)
    h = norm_mod(xs, norm1_g[0].reshape(1, d), mod[0, :, 0], mod[0, :, 1], t_all, seq, BF16)
    for l in range(depth):
        need_ctx = l < depth - 1
        rows = t_all if need_ctx else t_lat
        sh1, sc1, g1, sh2, sc2, g2 = (mod[l, :, k] for k in range(6))

        z = in_proj(h, w_in, l, t_all, BF16)

        a = chunk_gmlp(z, cols, gm_norm_g[l], gm_ws[l], gm_bs[l], rows)
        bb = neighbourhood_attention(z, cols, na_bias, l, batch, seq, n_ctx)
        tabs = _decay_tables(ret_decay_fwd[l], ret_decay_bwd[l])
        s0f, s0b = context_state(z, cols, tabs, batch, seq, n_ctx)
        r = retention(z, cols, tabs, s0f, s0b, 0, seq, batch, False)
        if need_ctx:
            bb = jnp.concatenate([bb, context_attention(z, cols, batch, seq, n_ctx)], axis=0)
            zeros = jnp.zeros_like(s0f)
            r = jnp.concatenate([r, retention(z, cols, tabs, zeros, zeros, t_lat, n_ctx, batch, True)], axis=0)

        y = merge_branches(z, a, bb, r, w_branch_a[l].astype(BF16), w_branch_b[l].astype(BF16),
                           w_branch_c[l].astype(BF16), rows)
        x1 = matmul_residual(y, w_out[l].astype(BF16), xs, g1, rows, seq)

        w_router = jnp.concatenate(
            [moe_w_group[l], moe_w_expert[l],
             jnp.zeros((d, ROUTER_LANES - MOE_GROUPS - MOE_EXPERTS), F32)], axis=1).astype(BF16)
        h2, routed, counts = norm_router(x1, norm2_g[l].reshape(1, d), sh2, sc2, w_router, rows, seq)
        tables = _dispatch_tables(routed, counts)
        y_moe = moe_experts(h2, tables, cast_layer_bf16(moe_w1, l), cast_layer_bf16(moe_w3, l),
                            cast_layer_bf16(moe_w2, l))
        if need_ctx:
            xs, h = moe_combine(x1, y_moe, tables[3], routed, g2, rows, seq, norm1_g[l + 1].reshape(1, d),
                                mod[l + 1, :, 0], mod[l + 1, :, 1], BF16, True)
        else:
            zero = jnp.zeros((1, 1, d), F32)
            out = moe_combine(x1, y_moe, tables[3], routed, g2, rows, seq, final_norm_g.reshape(1, d),
                              zero, zero, F32, False)
    return out.reshape(batch, seq, d)
```

```python
import functools

import jax
import jax.numpy as jnp
import numpy as np
from jax import lax
from jax.experimental import pallas as pl
from jax.experimental.pallas import tpu as pltpu

F32 = jnp.float32
BF16 = jnp.bfloat16
SDS = jax.ShapeDtypeStruct
BS = pl.BlockSpec

EPS = 1e-6
NEG_INF = -1e30
HEAD_DIM = 128
GRID_W = 64
NA_ROWS = 8
NA_COLS = 16
NA_QROWS = 8
NA_KROWS = NA_QROWS + NA_ROWS
NA_SUB_QROWS = 4
NA_SUB_KROWS = NA_SUB_QROWS + NA_ROWS
RET_DK = 128
RET_DV = 256
CHUNK = 128
RET_CHUNKS_PER_STEP = 16
ROPE_BASE = 10000.0
MOE_GROUPS = 4
MOE_EPG = 8
MOE_EXPERTS = MOE_GROUPS * MOE_EPG
MOE_TOPK = 2
ROUTER_LANES = 128
N_MIX_HEADS = 8
GM_WIDTH = N_MIX_HEADS * HEAD_DIM
NA_WIDTH = N_MIX_HEADS * HEAD_DIM
RET_QK_WIDTH = N_MIX_HEADS * RET_DK
RET_V_WIDTH = N_MIX_HEADS * RET_DV

V7X_VMEM_BYTES = 64 * 1024 * 1024
VMEM_LIMIT = V7X_VMEM_BYTES - 8 * 1024 * 1024

ROW_TILE = 512
COL_TILE = 1024
MOE_ROWS = 256
COMBINE_ROWS = 128


def _params(*sem):
    return pltpu.CompilerParams(dimension_semantics=sem, vmem_limit_bytes=VMEM_LIMIT)


def _col_offsets(d):
    off, out = 0, {}
    for name, width in (("gate_a", d), ("gate_b", d), ("gate_c", d), ("gm_u", GM_WIDTH), ("gm_v", GM_WIDTH),
                        ("na_q", NA_WIDTH), ("na_k", NA_WIDTH), ("na_v", NA_WIDTH),
                        ("ret_q", RET_QK_WIDTH), ("ret_k", RET_QK_WIDTH), ("ret_v", RET_V_WIDTH),
                        ("ret_g", RET_V_WIDTH)):
        out[name] = off
        off += width
    return out, off


def _dot(a, b):
    return jnp.dot(a, b, preferred_element_type=F32)


def _dot_nt(a, b):
    return lax.dot_general(a, b, (((1,), (1,)), ((), ())), preferred_element_type=F32)


_HI16 = 0xFFFF0000


def _pack_halves(v):
    n = v.shape[1] // 2
    bits = lax.bitcast_convert_type(v.astype(BF16).astype(F32), jnp.uint32)
    return lax.shift_right_logical(bits[:, :n], jnp.uint32(16)) | (bits[:, n:] & jnp.uint32(_HI16))


def _unpack_halves(u):
    lo = lax.bitcast_convert_type(lax.shift_left(u, jnp.uint32(16)), F32)
    hi = lax.bitcast_convert_type(u & jnp.uint32(_HI16), F32)
    return lo, hi


def _norm_mod(x_ref, g_ref, sh_ref, sc_ref):
    x = x_ref[...]
    ms = jnp.mean(x * x, axis=-1, keepdims=True)
    y = x * lax.rsqrt(ms + EPS) * g_ref[...]
    return y * (1.0 + sc_ref[...]) + sh_ref[...]


def _norm_kernel(x_ref, xt_ref, g_ref, sh_ref, sc_ref, o_ref, *, n_main):
    @pl.when(pl.program_id(0) < n_main)
    def _():
        o_ref[...] = _norm_mod(x_ref, g_ref, sh_ref, sc_ref).astype(o_ref.dtype)

    @pl.when(pl.program_id(0) >= n_main)
    def _():
        o_ref[...] = _norm_mod(xt_ref, g_ref, sh_ref, sc_ref).astype(o_ref.dtype)


def _first_lane(mask, lane):
    return jnp.min(jnp.where(mask, lane, float(ROUTER_LANES)), axis=-1, keepdims=True)


def _route(logits, run_ref):
    n = logits.shape[0]
    lane = lax.broadcasted_iota(jnp.int32, logits.shape, 1).astype(F32)
    neg = -jnp.inf
    is_g = lane < MOE_GROUPS
    gl = jnp.where(is_g, logits, neg)
    eg = jnp.exp(gl - jnp.max(gl, axis=-1, keepdims=True))
    pg = eg / jnp.sum(eg, axis=-1, keepdims=True)
    p_sel = jnp.max(pg, axis=-1, keepdims=True)
    grp = _first_lane((pg == p_sel) & is_g, lane)
    lo = MOE_GROUPS + MOE_EPG * grp
    is_e = (lane >= lo) & (lane < lo + MOE_EPG)
    el = jnp.where(is_e, logits, neg)
    ee = jnp.exp(el - jnp.max(el, axis=-1, keepdims=True))
    pe = ee / jnp.sum(ee, axis=-1, keepdims=True)
    t1 = jnp.max(pe, axis=-1, keepdims=True)
    i1 = _first_lane((pe == t1) & is_e, lane)
    rest = jnp.where(is_e & (lane != i1), pe, -1.0)
    t2 = jnp.max(rest, axis=-1, keepdims=True)
    i2 = _first_lane(rest == t2, lane)
    tsum = t1 + t2
    w1, w2 = p_sel * t1 / tsum, p_sel * t2 / tsum
    e1, e2 = i1 - MOE_GROUPS, i2 - MOE_GROUPS
    oh1, oh2 = lane == e1, lane == e2
    cnt = jnp.where(oh1 | oh2, 1.0, 0.0)
    r_i = lax.broadcasted_iota(jnp.int32, (n, n), 0)
    c_i = lax.broadcasted_iota(jnp.int32, (n, n), 1)
    tri = jnp.where(c_i < r_i, 1.0, 0.0).astype(BF16)
    before = _dot(tri, cnt.astype(BF16)) + run_ref[...]
    r1 = jnp.sum(jnp.where(oh1, before, 0.0), axis=-1, keepdims=True)
    r2 = jnp.sum(jnp.where(oh2, before, 0.0), axis=-1, keepdims=True)
    run_ref[...] = run_ref[...] + jnp.sum(cnt, axis=0, keepdims=True)
    slab = jnp.zeros(logits.shape, F32)
    for k, v in enumerate((e1, e2, r1, r2, w1, w2)):
        slab = jnp.where(lane == float(k), v, slab)
    return slab


def _norm_router_kernel(x_ref, g_ref, sh_ref, sc_ref, wr_ref, h_ref, rt_ref, cnt_ref, run_ref):
    @pl.when(pl.program_id(0) == 0)
    def _():
        run_ref[...] = jnp.zeros_like(run_ref)
    h = _norm_mod(x_ref, g_ref, sh_ref, sc_ref)
    h_ref[...] = _pack_halves(h)
    rt_ref[...] = _route(_dot(h.astype(BF16), wr_ref[...]), run_ref)
    cnt_ref[...] = run_ref[...]


def _seg_map(seg_rows, tile, nseg):
    per = seg_rows // tile
    return lambda i: jnp.minimum(i // per, nseg - 1)


def norm_mod(stream, g, shift, scale, seg_rows, out_dtype):
    main, tail = stream
    d = main.shape[1]
    tr = ROW_TILE // 2
    seg = _seg_map(seg_rows, tr, shift.shape[0])
    n_main = main.shape[0] // tr
    rows = main.shape[0] + tail.shape[0]
    return pl.pallas_call(
        functools.partial(_norm_kernel, n_main=n_main), out_shape=SDS((rows, d), out_dtype), grid=(rows // tr,),
        in_specs=[BS((tr, d), lambda i: (jnp.minimum(i, n_main - 1), 0)),
                  BS((tr, d), lambda i: (jnp.maximum(i - n_main, 0), 0)), BS((1, d), lambda i: (0, 0)),
                  BS((None, 1, d), lambda i: (seg(i), 0, 0)), BS((None, 1, d), lambda i: (seg(i), 0, 0))],
        out_specs=BS((tr, d), lambda i: (i, 0)),
        compiler_params=_params("parallel"), name="norm_mod")(main, tail, g, shift, scale)


def norm_router(x, g, shift, scale, w_router, rows, seg_rows):
    d = x.shape[1]
    tr = ROW_TILE // 2
    seg = _seg_map(seg_rows, tr, shift.shape[0])
    return pl.pallas_call(
        _norm_router_kernel,
        out_shape=(SDS((rows, d // 2), jnp.uint32), SDS((rows, ROUTER_LANES), F32),
                   SDS((1, ROUTER_LANES), F32)),
        grid=(rows // tr,),
        in_specs=[BS((tr, d), lambda i: (i, 0)), BS((1, d), lambda i: (0, 0)),
                  BS((None, 1, d), lambda i: (seg(i), 0, 0)), BS((None, 1, d), lambda i: (seg(i), 0, 0)),
                  BS((d, ROUTER_LANES), lambda i: (0, 0))],
        out_specs=(BS((tr, d // 2), lambda i: (i, 0)), BS((tr, ROUTER_LANES), lambda i: (i, 0)),
                   BS((1, ROUTER_LANES), lambda i: (0, 0))),
        scratch_shapes=[pltpu.VMEM((1, ROUTER_LANES), F32)],
        compiler_params=_params("arbitrary"), name="norm_router")(x, g, shift, scale, w_router)


def _ada_kernel(a_ref, w_ref, b_ref, o_ref):
    o_ref[...] = _dot(a_ref[...], w_ref[...].astype(BF16)) + b_ref[...]


def ada_modulation(act, ada_w, ada_b):
    depth, d, n = ada_w.shape
    tn = ROW_TILE
    return pl.pallas_call(
        _ada_kernel, out_shape=SDS((depth, act.shape[0], n), F32), grid=(depth, n // tn),
        in_specs=[BS(act.shape, lambda l, j: (0, 0)), BS((None, d, tn), lambda l, j: (l, 0, j)),
                  BS((None, 1, tn), lambda l, j: (l, 0, j))],
        out_specs=BS((None, act.shape[0], tn), lambda l, j: (l, 0, j)),
        compiler_params=_params("parallel", "parallel"), name="ada_modulation")(
            act, ada_w, ada_b.reshape(depth, 1, n))


def _in_proj_kernel(a_ref, b_ref, *refs, n_side):
    side_in, o_ref, side_out, bq_ref = refs[:n_side], refs[n_side], refs[n_side + 1:-1], refs[-1]
    @pl.when(pl.program_id(1) == 0)
    def _():
        bq_ref[...] = b_ref[...].astype(BF16)
    o_ref[...] = _dot(a_ref[...], bq_ref[...]).astype(o_ref.dtype)
    for s_in, s_out in zip(side_in, side_out):
        s_out[...] = s_in[...].astype(BF16)


BF16_SUBLANES = 16


def in_proj(a, w, layer, rows, out_dtype, side):
    _, k, n = w.shape
    tm, tn = ROW_TILE, COL_TILE
    nj, ni = n // tn, rows // tm
    n_steps = nj * ni
    in_specs, out_specs, out_shapes = [], [], []
    for s in side:
        r, c = s.shape[-2:]
        n_mat = int(np.prod(s.shape[1:-2]))
        rps = BF16_SUBLANES
        while n_mat * (r // rps) > n_steps:
            rps *= 2
        assert r % rps == 0
        bpm = r // rps
        blk = lambda j, i, last=n_mat * bpm - 1: jnp.minimum(j * ni + i, last)
        if s.ndim == 4:
            in_specs.append(BS((None, None, rps, c),
                               lambda j, i, blk=blk, bpm=bpm: (layer, blk(j, i) // bpm, blk(j, i) % bpm, 0)))
            out_specs.append(BS((None, rps, c), lambda j, i, blk=blk, bpm=bpm: (blk(j, i) // bpm, blk(j, i) % bpm, 0)))
        else:
            in_specs.append(BS((None, rps, c), lambda j, i, blk=blk: (layer, blk(j, i), 0)))
            out_specs.append(BS((rps, c), lambda j, i, blk=blk: (blk(j, i), 0)))
        out_shapes.append(SDS(s.shape[1:], BF16))
    res = pl.pallas_call(
        functools.partial(_in_proj_kernel, n_side=len(side)),
        out_shape=[SDS((rows, n), out_dtype)] + out_shapes, grid=(nj, ni),
        in_specs=[BS((tm, k), lambda j, i: (i, 0)),
                  BS((None, k, tn), lambda j, i: (layer, 0, j), pipeline_mode=pl.Buffered(1))] + in_specs,
        out_specs=[BS((tm, tn), lambda j, i: (i, j))] + out_specs,
        scratch_shapes=[pltpu.VMEM((k, tn), BF16)],
        compiler_params=_params("arbitrary", "arbitrary"), name="in_proj")(a, w, *side)
    return res[0], list(res[1:])


def _mm_res_kernel(a_ref, b_ref, x_ref, xt_ref, g_ref, o_ref, *, n_main):
    upd = g_ref[...] * _dot(a_ref[...], b_ref[...])

    @pl.when(pl.program_id(1) < n_main)
    def _():
        o_ref[...] = x_ref[...] + upd

    @pl.when(pl.program_id(1) >= n_main)
    def _():
        o_ref[...] = xt_ref[...] + upd


def matmul_residual(a, b, stream, gate, rows, seg_rows):
    k, n = b.shape
    tm, tn = ROW_TILE, COL_TILE
    seg = _seg_map(seg_rows, tm, gate.shape[0])
    main, tail = stream
    n_main = main.shape[0] // tm
    return pl.pallas_call(
        functools.partial(_mm_res_kernel, n_main=n_main), out_shape=SDS((rows, n), F32),
        grid=(n // tn, rows // tm),
        in_specs=[BS((tm, k), lambda j, i: (i, 0)), BS((k, tn), lambda j, i: (0, j)),
                  BS((tm, tn), lambda j, i: (jnp.minimum(i, n_main - 1), j)),
                  BS((tm, tn), lambda j, i: (jnp.maximum(i - n_main, 0), j)),
                  BS((None, 1, tn), lambda j, i: (seg(i), 0, j))],
        out_specs=BS((tm, tn), lambda j, i: (i, j)),
        compiler_params=_params("parallel", "parallel"), name="out_proj")(a, b, main, tail, gate)


def _merge_kernel(a_ref, b_ref, r_ref, ga_ref, gb_ref, gc_ref, wa_ref, wb_ref, wc_ref, o_ref):
    def branch(x_ref, w_ref, g_ref):
        return jax.nn.sigmoid(g_ref[...].astype(F32)) * _dot(x_ref[...], w_ref[...])
    y = branch(a_ref, wa_ref, ga_ref) + branch(b_ref, wb_ref, gb_ref) + branch(r_ref, wc_ref, gc_ref)
    o_ref[...] = y.astype(o_ref.dtype)


def merge_branches(z, a, bb, r, w_a, w_b, w_c, rows):
    d = w_a.shape[1]
    tm, tn = ROW_TILE, COL_TILE
    nb = d // tn
    row_spec = lambda arr: BS((tm, arr.shape[1]), lambda j, i: (i, 0))
    w_spec = lambda arr: BS((arr.shape[0], tn), lambda j, i: (0, j))
    gate_spec = lambda g: BS((tm, tn), lambda j, i: (i, g * nb + j))
    return pl.pallas_call(
        _merge_kernel, out_shape=SDS((rows, d), BF16), grid=(nb, rows // tm),
        in_specs=[row_spec(a), row_spec(bb), row_spec(r), gate_spec(0), gate_spec(1), gate_spec(2),
                  w_spec(w_a), w_spec(w_b), w_spec(w_c)],
        out_specs=BS((tm, tn), lambda j, i: (i, j)),
        compiler_params=_params("parallel", "parallel"), name="merge_branches")(
            a, bb, r, z, z, z, w_a, w_b, w_c)


GM_CHUNKS_PER_STEP = 4


def _gmlp_kernel(u_ref, v_ref, ng_ref, ws_ref, bs_ref, o_ref):
    for c in range(GM_CHUNKS_PER_STEP):
        rows = slice(c * CHUNK, (c + 1) * CHUNK)
        v = jax.nn.gelu(v_ref[rows, :].astype(F32))
        mu = jnp.mean(v, axis=-1, keepdims=True)
        var = jnp.mean(jnp.square(v - mu), axis=-1, keepdims=True)
        vn = ((v - mu) * lax.rsqrt(var + EPS) * ng_ref[...]).astype(BF16)
        for g in range(N_MIX_HEADS):
            cols = slice(g * HEAD_DIM, (g + 1) * HEAD_DIM)
            mixed = _dot(ws_ref[g], vn[:, cols]) + bs_ref[g]
            u = jax.nn.gelu(u_ref[rows, cols].astype(F32))
            o_ref[rows, cols] = (u * mixed).astype(o_ref.dtype)


def chunk_gmlp(z, cols, norm_g, ws, bs, rows):
    tr = GM_CHUNKS_PER_STEP * CHUNK
    ub, vb = cols["gm_u"] // GM_WIDTH, cols["gm_v"] // GM_WIDTH
    bs_b = jnp.broadcast_to(bs[:, :, None], bs.shape + (HEAD_DIM,)).astype(F32)
    return pl.pallas_call(
        _gmlp_kernel, out_shape=SDS((rows, GM_WIDTH), BF16), grid=(rows // tr,),
        in_specs=[BS((tr, GM_WIDTH), lambda i: (i, ub)), BS((tr, GM_WIDTH), lambda i: (i, vb)),
                  BS((1, GM_WIDTH), lambda i: (0, 0)), BS(ws.shape, lambda i: (0, 0, 0)),
                  BS(bs_b.shape, lambda i: (0, 0, 0))],
        out_specs=BS((tr, GM_WIDTH), lambda i: (i, 0)),
        compiler_params=_params("parallel"), name="chunk_gmlp")(
            z, z, norm_g.reshape(1, GM_WIDTH), ws.astype(BF16), bs_b)


def _na_bias_tables(rpb, grid_rows):
    nq = grid_rows // NA_QROWS
    depth, n_heads = rpb.shape[:2]
    n_dr = 2 * NA_ROWS - 1
    i = np.arange(NA_QROWS)[:, None]
    m = np.arange(NA_KROWS)[None, :]
    c = np.arange(GRID_W)[:, None]
    w = np.arange(GRID_W)[None, :]
    c_start = np.clip(c - NA_COLS // 2, 0, GRID_W - NA_COLS)
    col_ok = (w >= c_start) & (w < c_start + NA_COLS)
    dc = np.clip(w - c + NA_COLS - 1, 0, 2 * NA_COLS - 2)
    col_sel = (dc.reshape(-1)[None, :] == np.arange(2 * NA_COLS - 1)[:, None]).astype(np.float32)
    band = jnp.einsum("lhab,bq->lhaq", rpb.astype(F32), col_sel, precision=lax.Precision.HIGHEST)
    band = jnp.where(col_ok[None, None, None], band.reshape(depth, n_heads, n_dr, GRID_W, GRID_W), NEG_INF)
    band = jnp.concatenate([band, jnp.full((depth, n_heads, 1, GRID_W, GRID_W), NEG_INF, F32)], axis=2)
    slots, offsets = [], []
    for j in (0, 1, nq - 1):
        base = int(np.clip(NA_QROWS * j - NA_ROWS // 2, 0, grid_rows - NA_KROWS))
        r = NA_QROWS * j + i
        kr = base + m
        r_start = np.clip(r - NA_ROWS // 2, 0, grid_rows - NA_ROWS)
        row_ok = (kr >= r_start) & (kr < r_start + NA_ROWS)
        slot = np.where(row_ok, kr - r + NA_ROWS - 1, n_dr)
        for s in range(NA_QROWS // NA_SUB_QROWS):
            sub = slice(s * NA_SUB_QROWS, (s + 1) * NA_SUB_QROWS)
            off = min(int(np.argmax(row_ok[sub].any(axis=0))), NA_KROWS - NA_SUB_KROWS)
            assert not row_ok[sub, :off].any() and not row_ok[sub, off + NA_SUB_KROWS:].any()
            slots.append(slot[sub, off:off + NA_SUB_KROWS])
            offsets.append(off)
    n_sub = NA_QROWS // NA_SUB_QROWS
    slots = jnp.asarray(np.stack(slots).reshape(-1), jnp.int32)
    out_block = (n_sub, NA_SUB_QROWS * GRID_W, NA_SUB_KROWS * GRID_W)
    grid_spec = pltpu.PrefetchScalarGridSpec(
        num_scalar_prefetch=1, grid=(depth, 3, n_heads),
        in_specs=[BS((None, None, n_dr + 1, GRID_W, GRID_W), lambda l, k, h, s: (l, h, 0, 0, 0))],
        out_specs=BS((None, None, None) + out_block, lambda l, k, h, s: (l, k, h, 0, 0, 0)))
    tables = pl.pallas_call(
        _na_bias_kernel, grid_spec=grid_spec, out_shape=SDS((depth, 3, n_heads) + out_block, F32),
        compiler_params=_params("parallel", "parallel", "parallel"), name="na_bias_tables")(slots, band)
    return tables, np.asarray(offsets).reshape(3, n_sub)


def _na_bias_kernel(slot_ref, band_ref, o_ref):
    kind = pl.program_id(1)
    for s in range(NA_QROWS // NA_SUB_QROWS):
        for i in range(NA_SUB_QROWS):
            row = ((kind * (NA_QROWS // NA_SUB_QROWS) + s) * NA_SUB_QROWS + i) * NA_SUB_KROWS
            tiles = [band_ref[slot_ref[row + m]] for m in range(NA_SUB_KROWS)]
            o_ref[s, i * GRID_W:(i + 1) * GRID_W, :] = jnp.concatenate(tiles, axis=1)


def _na_kernel(q_ref, k_ref, v_ref, kc_ref, vc_ref, bias_ref, o_ref, *, grid_rows, offsets):
    j = pl.program_id(2)
    nq = grid_rows // NA_QROWS
    kind = jnp.where(j == 0, 0, jnp.where(j == nq - 1, 2, 1))
    base_row = jnp.clip(NA_QROWS * j - NA_ROWS // 2, 0, grid_rows - NA_KROWS)
    nk = NA_SUB_KROWS * GRID_W
    nsq = NA_SUB_QROWS * GRID_W
    for s in range(NA_QROWS // NA_SUB_QROWS):
        off = jnp.where(j == 0, int(offsets[0, s]), jnp.where(j == nq - 1, int(offsets[2, s]), int(offsets[1, s])))
        start = pl.multiple_of((base_row + off) * GRID_W, (NA_ROWS // 2) * GRID_W)
        rows = slice(s * nsq, (s + 1) * nsq)
        q = (q_ref[rows, :].astype(F32) * (HEAD_DIM ** -0.5)).astype(BF16)
        s_loc = _dot_nt(q, k_ref[pl.ds(start, nk), :]) + bias_ref[kind, s]
        s_ctx = _dot_nt(q, kc_ref[...])
        mx = jnp.maximum(jnp.max(s_loc, axis=-1, keepdims=True), jnp.max(s_ctx, axis=-1, keepdims=True))
        e_loc = jnp.exp(s_loc - mx)
        e_ctx = jnp.exp(s_ctx - mx)
        den = jnp.sum(e_loc, axis=-1, keepdims=True) + jnp.sum(e_ctx, axis=-1, keepdims=True)
        o = _dot(e_loc.astype(BF16), v_ref[pl.ds(start, nk), :]) + _dot(e_ctx.astype(BF16), vc_ref[...])
        o_ref[rows, :] = (o / den).astype(o_ref.dtype)


def neighbourhood_attention(z, cols, bias_tables, layer, batch, seq, n_ctx):
    bias, offsets = bias_tables
    assert np.all(offsets % (NA_ROWS // 2) == 0)
    grid_rows = seq // GRID_W
    assert grid_rows % NA_QROWS == 0 and grid_rows >= 2 * NA_KROWS
    nq = grid_rows // NA_QROWS
    qb = NA_QROWS * GRID_W
    qc, kc, vc = (cols[n] // HEAD_DIM for n in ("na_q", "na_k", "na_v"))
    ctx_blk0 = batch * seq // n_ctx
    return pl.pallas_call(
        functools.partial(_na_kernel, grid_rows=grid_rows, offsets=offsets),
        out_shape=SDS((batch * seq, NA_WIDTH), BF16), grid=(N_MIX_HEADS, batch, nq),
        in_specs=[BS((qb, HEAD_DIM), lambda h, b, j: (b * nq + j, qc + h)),
                  BS((seq, HEAD_DIM), lambda h, b, j: (b, kc + h)),
                  BS((seq, HEAD_DIM), lambda h, b, j: (b, vc + h)),
                  BS((n_ctx, HEAD_DIM), lambda h, b, j: (ctx_blk0 + b, kc + h)),
                  BS((n_ctx, HEAD_DIM), lambda h, b, j: (ctx_blk0 + b, vc + h)),
                  BS((None, 3, None) + bias.shape[3:], lambda h, b, j: (layer, 0, h, 0, 0, 0))],
        out_specs=BS((qb, HEAD_DIM), lambda h, b, j: (b * nq + j, h)),
        compiler_params=_params("parallel", "parallel", "parallel"), name="neighbourhood_attention")(
            z, z, z, z, z, bias)


def _ctx_attn_kernel(q_ref, k_ref, v_ref, o_ref):
    q = (q_ref[...].astype(F32) * (HEAD_DIM ** -0.5)).astype(BF16)
    s = _dot_nt(q, k_ref[...])
    e = jnp.exp(s - jnp.max(s, axis=-1, keepdims=True))
    o = _dot(e.astype(BF16), v_ref[...]) / jnp.sum(e, axis=-1, keepdims=True)
    o_ref[...] = o.astype(o_ref.dtype)


def context_attention(z, cols, batch, seq, n_ctx):
    qc, kc, vc = (cols[n] // HEAD_DIM for n in ("na_q", "na_k", "na_v"))
    blk0 = batch * seq // n_ctx
    spec = lambda col: BS((n_ctx, HEAD_DIM), lambda b, h: (blk0 + b, col + h))
    return pl.pallas_call(
        _ctx_attn_kernel, out_shape=SDS((batch * n_ctx, NA_WIDTH), BF16), grid=(batch, N_MIX_HEADS),
        in_specs=[spec(qc), spec(kc), spec(vc)],
        out_specs=BS((n_ctx, HEAD_DIM), lambda b, h: (b, h)),
        compiler_params=_params("parallel", "parallel"), name="context_attention")(z, z, z)


def _rope(t_ref, cos_ref, sin_ref):
    t = t_ref[...].astype(F32)
    lane = lax.broadcasted_iota(jnp.int32, t.shape, 1)
    quarter = HEAD_DIM // 4
    partner = jnp.where((lane % (2 * quarter)) < quarter,
                        pltpu.roll(t, HEAD_DIM - quarter, 1), pltpu.roll(t, quarter, 1))
    return t * cos_ref[...] + partner * sin_ref[...]


def _rope_tables(n, identity):
    if identity:
        return jnp.ones((n, HEAD_DIM), F32), jnp.zeros((n, HEAD_DIM), F32)
    nf = HEAD_DIM // 4
    pos = jnp.arange(n)
    p_row = (pos // GRID_W).astype(F32)
    p_col = (pos % GRID_W).astype(F32)
    inv = ROPE_BASE ** (-jnp.arange(nf, dtype=F32) / nf)
    a_row = p_row[:, None] * inv[None, :]
    a_col = p_col[:, None] * inv[None, :]
    cos = jnp.concatenate([jnp.cos(a_row), jnp.cos(a_row), jnp.cos(a_col), jnp.cos(a_col)], axis=-1)
    sin = jnp.concatenate([-jnp.sin(a_row), jnp.sin(a_row), -jnp.sin(a_col), jnp.sin(a_col)], axis=-1)
    return cos, sin


def _decay_tables(dec_f, dec_b):
    log_gf = jnp.log1p(-jnp.exp2(dec_f.astype(F32)))
    log_gb = jnp.log1p(-jnp.exp2(dec_b.astype(F32)))
    idx = jnp.arange(CHUNK, dtype=F32)
    rel = idx[:, None] - idx[None, :]
    d_f = jnp.where(rel >= 0, jnp.exp(log_gf[:, None, None] * jnp.maximum(rel, 0.0)), 0.0)
    d_b = jnp.where(rel < 0, jnp.exp(log_gb[:, None, None] * jnp.maximum(-rel, 0.0)), 0.0)
    bcast = lambda v: jnp.broadcast_to(v[:, :, None], v.shape + (RET_DK,))
    return dict(
        log_gf=log_gf, log_gb=log_gb,
        dfb=d_f + d_b,
        qdf=bcast(jnp.exp(log_gf[:, None] * (idx + 1.0))),
        qdb=bcast(jnp.exp(log_gb[:, None] * (CHUNK - idx))),
        kdf=bcast(jnp.exp(log_gf[:, None] * (CHUNK - 1.0 - idx))),
        kdb=bcast(jnp.exp(log_gb[:, None] * idx)),
        cdf=jnp.broadcast_to(jnp.exp(log_gf * CHUNK)[:, None, None], (log_gf.shape[0], 1, RET_DV)),
        cdb=jnp.broadcast_to(jnp.exp(log_gb * CHUNK)[:, None, None], (log_gb.shape[0], 1, RET_DV)),
    )


def _ctx_state_kernel(k_ref, v_ref, wf_ref, wb_ref, sf_ref, sb_ref):
    k = k_ref[...].astype(F32) * (RET_DK ** -0.5)
    v = v_ref[...]
    sf_ref[...] = _dot((k * wf_ref[...]).T.astype(BF16), v)
    sb_ref[...] = _dot((k * wb_ref[...]).T.astype(BF16), v)


def context_state(z, cols, tabs, batch, seq, n_ctx):
    pos = jnp.arange(n_ctx, dtype=F32)
    bcast = lambda v: jnp.broadcast_to(v[:, :, None], v.shape + (RET_DK,))
    w_f = bcast(jnp.exp(tabs["log_gf"][:, None] * ((n_ctx - 1.0) - pos)))
    w_b = bcast(jnp.exp(tabs["log_gb"][:, None] * pos))
    kc, vc = cols["ret_k"] // RET_DK, cols["ret_v"] // RET_DV
    blk0 = batch * seq // n_ctx
    out = SDS((batch, N_MIX_HEADS, RET_DK, RET_DV), F32)
    tab_spec = BS((None, n_ctx, RET_DK), lambda b, h: (h, 0, 0))
    out_spec = BS((None, None, RET_DK, RET_DV), lambda b, h: (b, h, 0, 0))
    return pl.pallas_call(
        _ctx_state_kernel, out_shape=(out, out), grid=(batch, N_MIX_HEADS),
        in_specs=[BS((n_ctx, RET_DK), lambda b, h: (blk0 + b, kc + h)),
                  BS((n_ctx, RET_DV), lambda b, h: (blk0 + b, vc + h)), tab_spec, tab_spec],
        out_specs=(out_spec, out_spec),
        compiler_params=_params("parallel", "parallel"), name="context_state")(z, z, w_f, w_b)


def _ret_state_kernel(kf_ref, vf_ref, cosf_ref, sinf_ref, kb_ref, vb_ref, cosb_ref, sinb_ref,
                      kdf_ref, kdb_ref, cdf_ref, cdb_ref, s0f_ref, s0b_ref, sf_ref, sb_ref, st_ref, *, cpg):
    @pl.when(pl.program_id(2) == 0)
    def _():
        st_ref[0] = s0f_ref[...]
        st_ref[1] = s0b_ref[...]

    def scan(k_ref, v_ref, cos_ref, sin_ref, kd_ref, cd_ref, out_ref, slot, order):
        kr = _rope(k_ref, cos_ref, sin_ref) * (RET_DK ** -0.5)
        s = st_ref[slot]
        for c in order:
            rows = slice(c * CHUNK, (c + 1) * CHUNK)
            out_ref[c] = s.astype(out_ref.dtype)
            kc = (kr[rows, :] * kd_ref[...]).T.astype(BF16)
            s = s * cd_ref[...] + _dot(kc, v_ref[rows, :])
        st_ref[slot] = s

    scan(kf_ref, vf_ref, cosf_ref, sinf_ref, kdf_ref, cdf_ref, sf_ref, 0, range(cpg))
    scan(kb_ref, vb_ref, cosb_ref, sinb_ref, kdb_ref, cdb_ref, sb_ref, 1, reversed(range(cpg)))


def _ret_out_kernel(q_ref, k_ref, v_ref, g_ref, cos_ref, sin_ref, sf_ref, sb_ref, dfb_ref, qdf_ref, qdb_ref,
                    o_ref, *, cpg):
    qr = _rope(q_ref, cos_ref, sin_ref)
    kr = (_rope(k_ref, cos_ref, sin_ref) * (RET_DK ** -0.5)).astype(BF16)
    for c in range(cpg):
        rows = slice(c * CHUNK, (c + 1) * CHUNK)
        qc = qr[rows, :]
        p = (_dot_nt(qc.astype(BF16), kr[rows, :]) * dfb_ref[...]).astype(BF16)
        o = (_dot(p, v_ref[rows, :]) + _dot((qc * qdf_ref[...]).astype(BF16), sf_ref[c])
             + _dot((qc * qdb_ref[...]).astype(BF16), sb_ref[c]))
        mu = jnp.mean(o, axis=-1, keepdims=True)
        var = jnp.mean(jnp.square(o - mu), axis=-1, keepdims=True)
        on = (o - mu) * lax.rsqrt(var + EPS)
        g = g_ref[rows, :].astype(F32)
        o_ref[rows, :] = (on * (g * jax.nn.sigmoid(g))).astype(o_ref.dtype)


def retention(z, cols, tabs, s0f, s0b, row0, n, batch, rope_identity):
    nc = n // CHUNK
    cpg = min(RET_CHUNKS_PER_STEP, nc)
    ng = nc // cpg
    gr = cpg * CHUNK
    blk0 = row0 // gr
    qc, kc = cols["ret_q"] // RET_DK, cols["ret_k"] // RET_DK
    vc, gc = cols["ret_v"] // RET_DV, cols["ret_g"] // RET_DV
    cos, sin = _rope_tables(n, rope_identity)
    fwd = lambda b, h, g: blk0 + b * ng + g
    bwd = lambda b, h, g: blk0 + b * ng + (ng - 1 - g)
    head_tab = lambda w: BS((None, CHUNK, w), lambda b, h, g: (h, 0, 0))
    c_tab = BS((None, 1, RET_DV), lambda b, h, g: (h, 0, 0))
    s0_spec = BS((None, None, RET_DK, RET_DV), lambda b, h, g: (b, h, 0, 0))
    st_shape = SDS((batch, N_MIX_HEADS, nc, RET_DK, RET_DV), BF16)
    st_spec = lambda rev: BS((None, None, cpg, RET_DK, RET_DV),
                             lambda b, h, g: (b, h, (ng - 1 - g) if rev else g, 0, 0))
    sf, sb = pl.pallas_call(
        functools.partial(_ret_state_kernel, cpg=cpg), out_shape=(st_shape, st_shape),
        grid=(batch, N_MIX_HEADS, ng),
        in_specs=[BS((gr, RET_DK), lambda b, h, g: (fwd(b, h, g), kc + h)),
                  BS((gr, RET_DV), lambda b, h, g: (fwd(b, h, g), vc + h)),
                  BS((gr, RET_DK), lambda b, h, g: (g, 0)), BS((gr, RET_DK), lambda b, h, g: (g, 0)),
                  BS((gr, RET_DK), lambda b, h, g: (bwd(b, h, g), kc + h)),
                  BS((gr, RET_DV), lambda b, h, g: (bwd(b, h, g), vc + h)),
                  BS((gr, RET_DK), lambda b, h, g: (ng - 1 - g, 0)),
                  BS((gr, RET_DK), lambda b, h, g: (ng - 1 - g, 0)),
                  head_tab(RET_DK), head_tab(RET_DK), c_tab, c_tab, s0_spec, s0_spec],
        out_specs=(st_spec(False), st_spec(True)),
        scratch_shapes=[pltpu.VMEM((2, RET_DK, RET_DV), F32)],
        compiler_params=_params("parallel", "parallel", "arbitrary"), name="retention_state")(
            z, z, cos, sin, z, z, cos, sin, tabs["kdf"], tabs["kdb"], tabs["cdf"], tabs["cdb"], s0f, s0b)
    return pl.pallas_call(
        functools.partial(_ret_out_kernel, cpg=cpg), out_shape=SDS((batch * n, RET_V_WIDTH), BF16),
        grid=(batch, N_MIX_HEADS, ng),
        in_specs=[BS((gr, RET_DK), lambda b, h, g: (fwd(b, h, g), qc + h)),
                  BS((gr, RET_DK), lambda b, h, g: (fwd(b, h, g), kc + h)),
                  BS((gr, RET_DV), lambda b, h, g: (fwd(b, h, g), vc + h)),
                  BS((gr, RET_DV), lambda b, h, g: (fwd(b, h, g), gc + h)),
                  BS((gr, RET_DK), lambda b, h, g: (g, 0)), BS((gr, RET_DK), lambda b, h, g: (g, 0)),
                  st_spec(False), st_spec(False),
                  head_tab(CHUNK), head_tab(RET_DK), head_tab(RET_DK)],
        out_specs=BS((gr, RET_DV), lambda b, h, g: (b * ng + g, h)),
        compiler_params=_params("parallel", "parallel", "parallel"), name="retention_out")(
            z, z, z, z, cos, sin, sf, sb, tabs["dfb"], tabs["qdf"], tabs["qdb"])


def _dispatch_tables(routed, counts):
    t = routed.shape[0]
    n_assign = t * MOE_TOPK
    flat_e = routed[:, 0:MOE_TOPK].astype(jnp.int32).reshape(-1)
    rank = routed[:, MOE_TOPK:2 * MOE_TOPK].astype(jnp.int32).reshape(-1)
    flat_t = jnp.repeat(jnp.arange(t, dtype=jnp.int32), MOE_TOPK)
    counts = counts[0, :MOE_EXPERTS].astype(jnp.int32)
    padded = (counts + MOE_ROWS - 1) // MOE_ROWS * MOE_ROWS
    p_end = jnp.cumsum(padded)
    dest = (p_end - padded)[flat_e] + rank
    n_blocks = (n_assign + MOE_EXPERTS * (MOE_ROWS - 1)) // MOE_ROWS
    row_tok = jnp.zeros((n_blocks * MOE_ROWS,), jnp.int32).at[dest].set(flat_t)
    blk_start = jnp.arange(n_blocks, dtype=jnp.int32) * MOE_ROWS
    block_e = jnp.minimum(jnp.sum((p_end[None, :] <= blk_start[:, None]).astype(jnp.int32), axis=1),
                          MOE_EXPERTS - 1)
    n_used = (p_end[-1] // MOE_ROWS).astype(jnp.int32).reshape(1)
    return block_e, n_used, row_tok.reshape(n_blocks, 1, MOE_ROWS), dest.reshape(t, MOE_TOPK).astype(jnp.int32)


def _start_row_gather(src_hbm, idx_ref, idx_of_row, dst, sem, n):
    def issue(r, carry):
        pltpu.make_async_copy(src_hbm.at[pl.ds(idx_ref[0, idx_of_row(r)], 1)], dst.at[pl.ds(r, 1)], sem).start()
        return carry
    lax.fori_loop(0, n, issue, 0, unroll=8)


def _wait_row_gather(src_hbm, dst, sem):
    pltpu.make_async_copy(src_hbm.at[pl.ds(0, dst.shape[0])], dst, sem).wait()


def _moe_kernel(be_ref, nu_ref, tok_ref, tok_next_ref, h_hbm, w1_ref, w3_ref, w2_ref, y_ref, xb, sem):
    i = pl.program_id(0)
    n_used = nu_ref[0]
    slot = i % 2
    row = lambda r: r

    @pl.when((i == 0) & (n_used > 0))
    def _():
        _start_row_gather(h_hbm, tok_ref, row, xb.at[0], sem.at[0], MOE_ROWS)

    @pl.when(i + 1 < n_used)
    def _():
        _start_row_gather(h_hbm, tok_next_ref, row, xb.at[1 - slot], sem.at[1 - slot], MOE_ROWS)

    @pl.when(i < n_used)
    def _():
        _wait_row_gather(h_hbm, xb.at[slot], sem.at[slot])
        half = xb.shape[2]
        x_lo, x_hi = (v.astype(BF16) for v in _unpack_halves(xb[slot]))
        up = lambda w_ref: _dot(x_lo, w_ref[0:half, :]) + _dot(x_hi, w_ref[half:2 * half, :])
        h1 = up(w1_ref)
        h3 = up(w3_ref)
        hm = (h1 * jax.nn.sigmoid(h1) * h3).astype(BF16)
        y_ref[...] = _pack_halves(_dot(hm, w2_ref[...]))

    @pl.when(i >= n_used)
    def _():
        y_ref[...] = jnp.zeros_like(y_ref)


def moe_experts(h2, tables, w1, w3, w2):
    block_e, n_used, row_tok, _ = tables
    n_blocks = row_tok.shape[0]
    d, hid = w1.shape[1], w1.shape[2]
    assert h2.shape[1] == d // 2 and h2.dtype == jnp.uint32
    grid_spec = pltpu.PrefetchScalarGridSpec(
        num_scalar_prefetch=2, grid=(n_blocks,),
        in_specs=[BS((None, 1, MOE_ROWS), lambda i, be, nu: (i, 0, 0), memory_space=pltpu.SMEM),
                  BS((None, 1, MOE_ROWS), lambda i, be, nu: (jnp.minimum(i + 1, n_blocks - 1), 0, 0),
                     memory_space=pltpu.SMEM),
                  BS(memory_space=pl.ANY),
                  BS((None, d, hid), lambda i, be, nu: (be[i], 0, 0)),
                  BS((None, d, hid), lambda i, be, nu: (be[i], 0, 0)),
                  BS((None, hid, d), lambda i, be, nu: (be[i], 0, 0))],
        out_specs=BS((MOE_ROWS, d // 2), lambda i, be, nu: (i, 0)),
        scratch_shapes=[pltpu.VMEM((2, MOE_ROWS, d // 2), jnp.uint32), pltpu.SemaphoreType.DMA((2,))])
    return pl.pallas_call(
        _moe_kernel, out_shape=SDS((n_blocks * MOE_ROWS, d // 2), jnp.uint32), grid_spec=grid_spec,
        compiler_params=_params("arbitrary"), name="moe_experts")(
            block_e, n_used, row_tok, row_tok, h2, w1, w3, w2)


def _combine_kernel(pos_ref, pos_next_ref, y_hbm, x_ref, w_ref, g_ref, ng_ref, nsh_ref, nsc_ref, *refs,
                    keep_stream):
    outs, (yb, sem) = refs[:-2], refs[-2:]
    i = pl.program_id(0)
    tt = COMBINE_ROWS
    slot = i % 2
    idx = lambda r: r

    @pl.when(i == 0)
    def _():
        _start_row_gather(y_hbm, pos_ref, idx, yb.at[0], sem.at[0], MOE_TOPK * tt)

    @pl.when(i + 1 < pl.num_programs(0))
    def _():
        _start_row_gather(y_hbm, pos_next_ref, idx, yb.at[1 - slot], sem.at[1 - slot], MOE_TOPK * tt)

    _wait_row_gather(y_hbm, yb.at[slot], sem.at[slot])
    w = w_ref[...]
    w0, w1 = (w[:, 2 * MOE_TOPK + k:2 * MOE_TOPK + k + 1] for k in range(MOE_TOPK))
    half = yb.shape[2]
    halves = (slice(0, half), slice(half, 2 * half))
    ya = _unpack_halves(yb[slot, 0:tt, :])
    yb_ = _unpack_halves(yb[slot, tt:2 * tt, :])
    xn = [x_ref[:, s] + g_ref[:, s] * (a * w0 + b * w1) for s, a, b in zip(halves, ya, yb_)]
    if keep_stream:
        for s, v in zip(halves, xn):
            outs[0][:, s] = v
    h_ref = outs[-1]
    ms = sum(jnp.sum(v * v, axis=-1, keepdims=True) for v in xn) / (2 * half)
    inv = lax.rsqrt(ms + EPS)
    for s, v in zip(halves, xn):
        h_ref[:, s] = ((v * inv * ng_ref[:, s]) * (1.0 + nsc_ref[:, s]) + nsh_ref[:, s]).astype(h_ref.dtype)


def moe_combine(x1, y, pos, routed, gate, rows, seg_rows, norm_g, norm_shift, norm_scale, out_dtype, keep_stream):
    d = x1.shape[1]
    tt = COMBINE_ROWS
    nt = rows // tt
    seg = _seg_map(seg_rows, tt, gate.shape[0])
    nseg = _seg_map(seg_rows, tt, norm_shift.shape[0])
    pos3 = pos.reshape(nt, tt, MOE_TOPK).transpose(0, 2, 1).reshape(nt, 1, MOE_TOPK * tt)
    row_spec = BS((tt, d), lambda i: (i, 0))
    out_shape = ((SDS((rows, d), F32),) if keep_stream else ()) + (SDS((rows, d), out_dtype),)
    res = pl.pallas_call(
        functools.partial(_combine_kernel, keep_stream=keep_stream), out_shape=out_shape, grid=(nt,),
        in_specs=[BS((None, 1, MOE_TOPK * tt), lambda i: (i, 0, 0), memory_space=pltpu.SMEM),
                  BS((None, 1, MOE_TOPK * tt), lambda i: (jnp.minimum(i + 1, nt - 1), 0, 0),
                     memory_space=pltpu.SMEM),
                  BS(memory_space=pl.ANY), row_spec,
                  BS((tt, ROUTER_LANES), lambda i: (i, 0)), BS((None, 1, d), lambda i: (seg(i), 0, 0)),
                  BS((1, d), lambda i: (0, 0)), BS((None, 1, d), lambda i: (nseg(i), 0, 0)),
                  BS((None, 1, d), lambda i: (nseg(i), 0, 0))],
        out_specs=tuple(row_spec for _ in out_shape),
        scratch_shapes=[pltpu.VMEM((2, MOE_TOPK * tt, d // 2), jnp.uint32), pltpu.SemaphoreType.DMA((2,))],
        compiler_params=_params("arbitrary"), name="moe_combine")(
            pos3, pos3, y, x1, routed, gate, norm_g, norm_shift, norm_scale)
    return res if keep_stream else res[0]


def kernel(x, c, ctx, c_ctx, ada_w, ada_b, norm1_g, w_in, gm_norm_g, gm_ws, gm_bs, na_rpb, ret_decay_fwd,
           ret_decay_bwd, w_branch_a, w_branch_b, w_branch_c, w_out, norm2_g, moe_w_group, moe_w_expert,
           moe_w1, moe_w3, moe_w2, final_norm_g):
    batch, seq, d = x.shape
    n_ctx = ctx.shape[1]
    depth = ada_w.shape[0]
    t_lat, t_ctx = batch * seq, batch * n_ctx
    t_all = t_lat + t_ctx
    cols, in_total = _col_offsets(d)
    assert w_in.shape[2] == in_total and seq % ROW_TILE == 0 and t_ctx % ROW_TILE == 0
    assert 8 >= batch + 1

    cond = jnp.concatenate([c, c_ctx[None, :], jnp.zeros((8 - batch - 1, d), c.dtype)], axis=0)
    mod = ada_modulation(jax.nn.silu(cond).astype(BF16), ada_w, ada_b)[:, :batch + 1]
    mod = mod.reshape(depth, batch + 1, 6, 1, d)

    na_bias = _na_bias_tables(na_rpb, seq // GRID_W)
    stream = (x.reshape(t_lat, d), ctx.reshape(t_ctx, d))
    h = norm_mod(stream, norm1_g[0].reshape(1, d), mod[0, :, 0], mod[0, :, 1], seq, BF16)
    for l in range(depth):
        need_ctx = l < depth - 1
        rows = t_all if need_ctx else t_lat
        sh1, sc1, g1, sh2, sc2, g2 = (mod[l, :, k] for k in range(6))

        z, (w_a, w_b, w_c, w_o, w1, w3, w2) = in_proj(
            h, w_in, l, t_all, BF16, [w_branch_a, w_branch_b, w_branch_c, w_out, moe_w1, moe_w3, moe_w2])

        a = chunk_gmlp(z, cols, gm_norm_g[l], gm_ws[l], gm_bs[l], rows)
        bb = neighbourhood_attention(z, cols, na_bias, l, batch, seq, n_ctx)
        tabs = _decay_tables(ret_decay_fwd[l], ret_decay_bwd[l])
        s0f, s0b = context_state(z, cols, tabs, batch, seq, n_ctx)
        r = retention(z, cols, tabs, s0f, s0b, 0, seq, batch, False)
        if need_ctx:
            bb = jnp.concatenate([bb, context_attention(z, cols, batch, seq, n_ctx)], axis=0)
            zeros = jnp.zeros_like(s0f)
            r = jnp.concatenate([r, retention(z, cols, tabs, zeros, zeros, t_lat, n_ctx, batch, True)], axis=0)

        y = merge_branches(z, a, bb, r, w_a, w_b, w_c, rows)
        x1 = matmul_residual(y, w_o, stream, g1, rows, seq)

        w_router = jnp.concatenate(
            [moe_w_group[l], moe_w_expert[l],
             jnp.zeros((d, ROUTER_LANES - MOE_GROUPS - MOE_EXPERTS), F32)], axis=1).astype(BF16)
        h2, routed, counts = norm_router(x1, norm2_g[l].reshape(1, d), sh2, sc2, w_router, rows, seq)
        tables = _dispatch_tables(routed, counts)
        y_moe = moe_experts(h2, tables, w1, w3, w2)
        if need_ctx:
            xs, h = moe_combine(x1, y_moe, tables[3], routed, g2, rows, seq, norm1_g[l + 1].reshape(1, d),
                                mod[l + 1, :, 0], mod[l + 1, :, 1], BF16, True)
            stream = (xs, xs)
        else:
            zero = jnp.zeros((1, 1, d), F32)
            out = moe_combine(x1, y_moe, tables[3], routed, g2, rows, seq, final_norm_g.reshape(1, d),
                              zero, zero, F32, False)
    return out.reshape(batch, seq, d)
```

```python
import functools

import jax
import jax.numpy as jnp
import numpy as np
from jax import lax
from jax.experimental import pallas as pl
from jax.experimental.pallas import tpu as pltpu

F32 = jnp.float32
BF16 = jnp.bfloat16
SDS = jax.ShapeDtypeStruct
BS = pl.BlockSpec

EPS = 1e-6
NEG_INF = -1e30
HEAD_DIM = 128
GRID_W = 64
NA_ROWS = 8
NA_COLS = 16
NA_QROWS = 8
NA_KROWS = NA_QROWS + NA_ROWS
NA_SUB_QROWS = 4
NA_SUB_KROWS = NA_SUB_QROWS + NA_ROWS
RET_DK = 128
RET_DV = 256
CHUNK = 128
RET_CHUNKS_PER_STEP = 16
ROPE_BASE = 10000.0
MOE_GROUPS = 4
MOE_EPG = 8
MOE_EXPERTS = MOE_GROUPS * MOE_EPG
MOE_TOPK = 2
ROUTER_LANES = 128
N_MIX_HEADS = 8
GM_WIDTH = N_MIX_HEADS * HEAD_DIM
NA_WIDTH = N_MIX_HEADS * HEAD_DIM
RET_QK_WIDTH = N_MIX_HEADS * RET_DK
RET_V_WIDTH = N_MIX_HEADS * RET_DV

V7X_VMEM_BYTES = 64 * 1024 * 1024
VMEM_LIMIT = V7X_VMEM_BYTES - 8 * 1024 * 1024

ROW_TILE = 512
COL_TILE = 1024
MOE_ROWS = 256
COMBINE_ROWS = 128


def _params(*sem):
    return pltpu.CompilerParams(dimension_semantics=sem, vmem_limit_bytes=VMEM_LIMIT)


def _col_offsets(d):
    off, out = 0, {}
    for name, width in (("gate_a", d), ("gate_b", d), ("gate_c", d), ("gm_u", GM_WIDTH), ("gm_v", GM_WIDTH),
                        ("na_q", NA_WIDTH), ("na_k", NA_WIDTH), ("na_v", NA_WIDTH),
                        ("ret_q", RET_QK_WIDTH), ("ret_k", RET_QK_WIDTH), ("ret_v", RET_V_WIDTH),
                        ("ret_g", RET_V_WIDTH)):
        out[name] = off
        off += width
    return out, off


def _dot(a, b):
    return jnp.dot(a, b, preferred_element_type=F32)


def _dot_nt(a, b):
    return lax.dot_general(a, b, (((1,), (1,)), ((), ())), preferred_element_type=F32)


_HI16 = 0xFFFF0000


def _pack_halves(v):
    n = v.shape[1] // 2
    bits = lax.bitcast_convert_type(v.astype(BF16).astype(F32), jnp.uint32)
    return lax.shift_right_logical(bits[:, :n], jnp.uint32(16)) | (bits[:, n:] & jnp.uint32(_HI16))


def _unpack_halves(u):
    lo = lax.bitcast_convert_type(lax.shift_left(u, jnp.uint32(16)), F32)
    hi = lax.bitcast_convert_type(u & jnp.uint32(_HI16), F32)
    return lo, hi


def _norm_mod(x_ref, g_ref, sh_ref, sc_ref):
    x = x_ref[...]
    ms = jnp.mean(x * x, axis=-1, keepdims=True)
    y = x * lax.rsqrt(ms + EPS) * g_ref[...]
    return y * (1.0 + sc_ref[...]) + sh_ref[...]


def _norm_kernel(x_ref, xt_ref, g_ref, sh_ref, sc_ref, o_ref, *, n_main):
    @pl.when(pl.program_id(0) < n_main)
    def _():
        o_ref[...] = _norm_mod(x_ref, g_ref, sh_ref, sc_ref).astype(o_ref.dtype)

    @pl.when(pl.program_id(0) >= n_main)
    def _():
        o_ref[...] = _norm_mod(xt_ref, g_ref, sh_ref, sc_ref).astype(o_ref.dtype)


def _first_lane(mask, lane):
    return jnp.min(jnp.where(mask, lane, float(ROUTER_LANES)), axis=-1, keepdims=True)


def _route(logits, run_ref):
    n = logits.shape[0]
    lane = lax.broadcasted_iota(jnp.int32, logits.shape, 1).astype(F32)
    neg = -jnp.inf
    is_g = lane < MOE_GROUPS
    gl = jnp.where(is_g, logits, neg)
    eg = jnp.exp(gl - jnp.max(gl, axis=-1, keepdims=True))
    pg = eg / jnp.sum(eg, axis=-1, keepdims=True)
    p_sel = jnp.max(pg, axis=-1, keepdims=True)
    grp = _first_lane((pg == p_sel) & is_g, lane)
    lo = MOE_GROUPS + MOE_EPG * grp
    is_e = (lane >= lo) & (lane < lo + MOE_EPG)
    el = jnp.where(is_e, logits, neg)
    ee = jnp.exp(el - jnp.max(el, axis=-1, keepdims=True))
    pe = ee / jnp.sum(ee, axis=-1, keepdims=True)
    t1 = jnp.max(pe, axis=-1, keepdims=True)
    i1 = _first_lane((pe == t1) & is_e, lane)
    rest = jnp.where(is_e & (lane != i1), pe, -1.0)
    t2 = jnp.max(rest, axis=-1, keepdims=True)
    i2 = _first_lane(rest == t2, lane)
    tsum = t1 + t2
    w1, w2 = p_sel * t1 / tsum, p_sel * t2 / tsum
    e1, e2 = i1 - MOE_GROUPS, i2 - MOE_GROUPS
    oh1, oh2 = lane == e1, lane == e2
    cnt = jnp.where(oh1 | oh2, 1.0, 0.0)
    r_i = lax.broadcasted_iota(jnp.int32, (n, n), 0)
    c_i = lax.broadcasted_iota(jnp.int32, (n, n), 1)
    tri = jnp.where(c_i < r_i, 1.0, 0.0).astype(BF16)
    before = _dot(tri, cnt.astype(BF16)) + run_ref[...]
    r1 = jnp.sum(jnp.where(oh1, before, 0.0), axis=-1, keepdims=True)
    r2 = jnp.sum(jnp.where(oh2, before, 0.0), axis=-1, keepdims=True)
    run_ref[...] = run_ref[...] + jnp.sum(cnt, axis=0, keepdims=True)
    slab = jnp.zeros(logits.shape, F32)
    for k, v in enumerate((e1, e2, r1, r2, w1, w2)):
        slab = jnp.where(lane == float(k), v, slab)
    return slab


def _norm_router_kernel(x_ref, g_ref, sh_ref, sc_ref, wr_ref, h_ref, rt_ref, cnt_ref, run_ref):
    @pl.when(pl.program_id(0) == 0)
    def _():
        run_ref[...] = jnp.zeros_like(run_ref)
    h = _norm_mod(x_ref, g_ref, sh_ref, sc_ref)
    h_ref[...] = _pack_halves(h)
    rt_ref[...] = _route(_dot(h.astype(BF16), wr_ref[...]), run_ref)
    cnt_ref[...] = run_ref[...]


def _seg_map(seg_rows, tile, nseg):
    per = seg_rows // tile
    return lambda i: jnp.minimum(i // per, nseg - 1)


def norm_mod(stream, g, shift, scale, seg_rows, out_dtype):
    main, tail = stream
    d = main.shape[1]
    tr = ROW_TILE // 2
    seg = _seg_map(seg_rows, tr, shift.shape[0])
    n_main = main.shape[0] // tr
    rows = main.shape[0] + tail.shape[0]
    return pl.pallas_call(
        functools.partial(_norm_kernel, n_main=n_main), out_shape=SDS((rows, d), out_dtype), grid=(rows // tr,),
        in_specs=[BS((tr, d), lambda i: (jnp.minimum(i, n_main - 1), 0)),
                  BS((tr, d), lambda i: (jnp.maximum(i - n_main, 0), 0)), BS((1, d), lambda i: (0, 0)),
                  BS((None, 1, d), lambda i: (seg(i), 0, 0)), BS((None, 1, d), lambda i: (seg(i), 0, 0))],
        out_specs=BS((tr, d), lambda i: (i, 0)),
        compiler_params=_params("parallel"), name="norm_mod")(main, tail, g, shift, scale)


def norm_router(x, g, shift, scale, w_router, rows, seg_rows):
    d = x.shape[1]
    tr = ROW_TILE // 2
    seg = _seg_map(seg_rows, tr, shift.shape[0])
    return pl.pallas_call(
        _norm_router_kernel,
        out_shape=(SDS((rows, d // 2), jnp.uint32), SDS((rows, ROUTER_LANES), F32),
                   SDS((1, ROUTER_LANES), F32)),
        grid=(rows // tr,),
        in_specs=[BS((tr, d), lambda i: (i, 0)), BS((1, d), lambda i: (0, 0)),
                  BS((None, 1, d), lambda i: (seg(i), 0, 0)), BS((None, 1, d), lambda i: (seg(i), 0, 0)),
                  BS((d, ROUTER_LANES), lambda i: (0, 0))],
        out_specs=(BS((tr, d // 2), lambda i: (i, 0)), BS((tr, ROUTER_LANES), lambda i: (i, 0)),
                   BS((1, ROUTER_LANES), lambda i: (0, 0))),
        scratch_shapes=[pltpu.VMEM((1, ROUTER_LANES), F32)],
        compiler_params=_params("arbitrary"), name="norm_router")(x, g, shift, scale, w_router)


def _ada_kernel(a_ref, w_ref, b_ref, o_ref):
    o_ref[...] = _dot(a_ref[...], w_ref[...].astype(BF16)) + b_ref[...]


def ada_modulation(act, ada_w, ada_b):
    depth, d, n = ada_w.shape
    tn = ROW_TILE
    return pl.pallas_call(
        _ada_kernel, out_shape=SDS((depth, act.shape[0], n), F32), grid=(depth, n // tn),
        in_specs=[BS(act.shape, lambda l, j: (0, 0)), BS((None, d, tn), lambda l, j: (l, 0, j)),
                  BS((None, 1, tn), lambda l, j: (l, 0, j))],
        out_specs=BS((None, act.shape[0], tn), lambda l, j: (l, 0, j)),
        compiler_params=_params("parallel", "parallel"), name="ada_modulation")(
            act, ada_w, ada_b.reshape(depth, 1, n))


def _in_proj_kernel(a_ref, b_ref, *refs, n_side):
    side_in, o_ref, side_out, bq_ref = refs[:n_side], refs[n_side], refs[n_side + 1:-1], refs[-1]
    @pl.when(pl.program_id(1) == 0)
    def _():
        bq_ref[...] = b_ref[...].astype(BF16)
    o_ref[...] = _dot(a_ref[...], bq_ref[...]).astype(o_ref.dtype)
    for s_in, s_out in zip(side_in, side_out):
        s_out[...] = s_in[...].astype(BF16)


BF16_SUBLANES = 16


def in_proj(a, w, layer, rows, out_dtype, side):
    _, k, n = w.shape
    tm, tn = ROW_TILE, COL_TILE
    nj, ni = n // tn, rows // tm
    n_steps = nj * ni
    in_specs, out_specs, out_shapes = [], [], []
    for s in side:
        r, c = s.shape[-2:]
        n_mat = int(np.prod(s.shape[1:-2]))
        rps = BF16_SUBLANES
        while n_mat * (r // rps) > n_steps:
            rps *= 2
        assert r % rps == 0
        bpm = r // rps
        blk = lambda j, i, last=n_mat * bpm - 1: jnp.minimum(j * ni + i, last)
        if s.ndim == 4:
            in_specs.append(BS((None, None, rps, c),
                               lambda j, i, blk=blk, bpm=bpm: (layer, blk(j, i) // bpm, blk(j, i) % bpm, 0)))
            out_specs.append(BS((None, rps, c), lambda j, i, blk=blk, bpm=bpm: (blk(j, i) // bpm, blk(j, i) % bpm, 0)))
        else:
            in_specs.append(BS((None, rps, c), lambda j, i, blk=blk: (layer, blk(j, i), 0)))
            out_specs.append(BS((rps, c), lambda j, i, blk=blk: (blk(j, i), 0)))
        out_shapes.append(SDS(s.shape[1:], BF16))
    res = pl.pallas_call(
        functools.partial(_in_proj_kernel, n_side=len(side)),
        out_shape=[SDS((rows, n), out_dtype)] + out_shapes, grid=(nj, ni),
        in_specs=[BS((tm, k), lambda j, i: (i, 0)),
                  BS((None, k, tn), lambda j, i: (layer, 0, j), pipeline_mode=pl.Buffered(1))] + in_specs,
        out_specs=[BS((tm, tn), lambda j, i: (i, j))] + out_specs,
        scratch_shapes=[pltpu.VMEM((k, tn), BF16)],
        compiler_params=_params("arbitrary", "arbitrary"), name="in_proj")(a, w, *side)
    return res[0], list(res[1:])


def _mm_res_kernel(a_ref, b_ref, x_ref, xt_ref, g_ref, o_ref, *, n_main):
    upd = g_ref[...] * _dot(a_ref[...], b_ref[...])

    @pl.when(pl.program_id(1) < n_main)
    def _():
        o_ref[...] = x_ref[...] + upd

    @pl.when(pl.program_id(1) >= n_main)
    def _():
        o_ref[...] = xt_ref[...] + upd


def matmul_residual(a, b, stream, gate, rows, seg_rows):
    k, n = b.shape
    tm, tn = ROW_TILE, COL_TILE
    seg = _seg_map(seg_rows, tm, gate.shape[0])
    main, tail = stream
    n_main = main.shape[0] // tm
    return pl.pallas_call(
        functools.partial(_mm_res_kernel, n_main=n_main), out_shape=SDS((rows, n), F32),
        grid=(n // tn, rows // tm),
        in_specs=[BS((tm, k), lambda j, i: (i, 0)), BS((k, tn), lambda j, i: (0, j)),
                  BS((tm, tn), lambda j, i: (jnp.minimum(i, n_main - 1), j)),
                  BS((tm, tn), lambda j, i: (jnp.maximum(i - n_main, 0), j)),
                  BS((None, 1, tn), lambda j, i: (seg(i), 0, j))],
        out_specs=BS((tm, tn), lambda j, i: (i, j)),
        compiler_params=_params("parallel", "parallel"), name="out_proj")(a, b, main, tail, gate)


def _merge_kernel(a_ref, b_ref, r_ref, ga_ref, gb_ref, gc_ref, wa_ref, wb_ref, wc_ref, o_ref):
    def branch(x_ref, w_ref, g_ref):
        return jax.nn.sigmoid(g_ref[...].astype(F32)) * _dot(x_ref[...], w_ref[...])
    y = branch(a_ref, wa_ref, ga_ref) + branch(b_ref, wb_ref, gb_ref) + branch(r_ref, wc_ref, gc_ref)
    o_ref[...] = y.astype(o_ref.dtype)


def merge_branches(z, a, bb, r, w_a, w_b, w_c, rows):
    d = w_a.shape[1]
    tm, tn = ROW_TILE, COL_TILE
    nb = d // tn
    row_spec = lambda arr: BS((tm, arr.shape[1]), lambda j, i: (i, 0))
    w_spec = lambda arr: BS((arr.shape[0], tn), lambda j, i: (0, j))
    gate_spec = lambda g: BS((tm, tn), lambda j, i: (i, g * nb + j))
    return pl.pallas_call(
        _merge_kernel, out_shape=SDS((rows, d), BF16), grid=(nb, rows // tm),
        in_specs=[row_spec(a), row_spec(bb), row_spec(r), gate_spec(0), gate_spec(1), gate_spec(2),
                  w_spec(w_a), w_spec(w_b), w_spec(w_c)],
        out_specs=BS((tm, tn), lambda j, i: (i, j)),
        compiler_params=_params("parallel", "parallel"), name="merge_branches")(
            a, bb, r, z, z, z, w_a, w_b, w_c)


GM_CHUNKS_PER_STEP = 4


def _gmlp_kernel(u_ref, v_ref, ng_ref, ws_ref, bs_ref, o_ref):
    for c in range(GM_CHUNKS_PER_STEP):
        rows = slice(c * CHUNK, (c + 1) * CHUNK)
        v = jax.nn.gelu(v_ref[rows, :].astype(F32))
        mu = jnp.mean(v, axis=-1, keepdims=True)
        var = jnp.mean(jnp.square(v - mu), axis=-1, keepdims=True)
        vn = ((v - mu) * lax.rsqrt(var + EPS) * ng_ref[...]).astype(BF16)
        for g in range(N_MIX_HEADS):
            cols = slice(g * HEAD_DIM, (g + 1) * HEAD_DIM)
            mixed = _dot(ws_ref[g], vn[:, cols]) + bs_ref[g]
            u = jax.nn.gelu(u_ref[rows, cols].astype(F32))
            o_ref[rows, cols] = (u * mixed).astype(o_ref.dtype)


def chunk_gmlp(z, cols, norm_g, ws, bs, rows):
    tr = GM_CHUNKS_PER_STEP * CHUNK
    ub, vb = cols["gm_u"] // GM_WIDTH, cols["gm_v"] // GM_WIDTH
    bs_b = jnp.broadcast_to(bs[:, :, None], bs.shape + (HEAD_DIM,)).astype(F32)
    return pl.pallas_call(
        _gmlp_kernel, out_shape=SDS((rows, GM_WIDTH), BF16), grid=(rows // tr,),
        in_specs=[BS((tr, GM_WIDTH), lambda i: (i, ub)), BS((tr, GM_WIDTH), lambda i: (i, vb)),
                  BS((1, GM_WIDTH), lambda i: (0, 0)), BS(ws.shape, lambda i: (0, 0, 0)),
                  BS(bs_b.shape, lambda i: (0, 0, 0))],
        out_specs=BS((tr, GM_WIDTH), lambda i: (i, 0)),
        compiler_params=_params("parallel"), name="chunk_gmlp")(
            z, z, norm_g.reshape(1, GM_WIDTH), ws.astype(BF16), bs_b)


def _na_bias_tables(rpb, grid_rows):
    nq = grid_rows // NA_QROWS
    depth, n_heads = rpb.shape[:2]
    n_dr = 2 * NA_ROWS - 1
    i = np.arange(NA_QROWS)[:, None]
    m = np.arange(NA_KROWS)[None, :]
    c = np.arange(GRID_W)[:, None]
    w = np.arange(GRID_W)[None, :]
    c_start = np.clip(c - NA_COLS // 2, 0, GRID_W - NA_COLS)
    col_ok = (w >= c_start) & (w < c_start + NA_COLS)
    dc = np.clip(w - c + NA_COLS - 1, 0, 2 * NA_COLS - 2)
    col_sel = (dc.reshape(-1)[None, :] == np.arange(2 * NA_COLS - 1)[:, None]).astype(np.float32)
    band = jnp.einsum("lhab,bq->lhaq", rpb.astype(F32), col_sel, precision=lax.Precision.HIGHEST)
    band = jnp.where(col_ok[None, None, None], band.reshape(depth, n_heads, n_dr, GRID_W, GRID_W), NEG_INF)
    band = jnp.concatenate([band, jnp.full((depth, n_heads, 1, GRID_W, GRID_W), NEG_INF, F32)], axis=2)
    slots, offsets = [], []
    for j in (0, 1, nq - 1):
        base = int(np.clip(NA_QROWS * j - NA_ROWS // 2, 0, grid_rows - NA_KROWS))
        r = NA_QROWS * j + i
        kr = base + m
        r_start = np.clip(r - NA_ROWS // 2, 0, grid_rows - NA_ROWS)
        row_ok = (kr >= r_start) & (kr < r_start + NA_ROWS)
        slot = np.where(row_ok, kr - r + NA_ROWS - 1, n_dr)
        for s in range(NA_QROWS // NA_SUB_QROWS):
            sub = slice(s * NA_SUB_QROWS, (s + 1) * NA_SUB_QROWS)
            off = min(int(np.argmax(row_ok[sub].any(axis=0))), NA_KROWS - NA_SUB_KROWS)
            assert not row_ok[sub, :off].any() and not row_ok[sub, off + NA_SUB_KROWS:].any()
            slots.append(slot[sub, off:off + NA_SUB_KROWS])
            offsets.append(off)
    n_sub = NA_QROWS // NA_SUB_QROWS
    slots = jnp.asarray(np.stack(slots).reshape(-1), jnp.int32)
    out_block = (n_sub, NA_SUB_QROWS * GRID_W, NA_SUB_KROWS * GRID_W)
    grid_spec = pltpu.PrefetchScalarGridSpec(
        num_scalar_prefetch=1, grid=(depth, 3, n_heads),
        in_specs=[BS((None, None, n_dr + 1, GRID_W, GRID_W), lambda l, k, h, s: (l, h, 0, 0, 0))],
        out_specs=BS((None, None, None) + out_block, lambda l, k, h, s: (l, k, h, 0, 0, 0)))
    tables = pl.pallas_call(
        _na_bias_kernel, grid_spec=grid_spec, out_shape=SDS((depth, 3, n_heads) + out_block, F32),
        compiler_params=_params("parallel", "parallel", "parallel"), name="na_bias_tables")(slots, band)
    return tables, np.asarray(offsets).reshape(3, n_sub)


def _na_bias_kernel(slot_ref, band_ref, o_ref):
    kind = pl.program_id(1)
    for s in range(NA_QROWS // NA_SUB_QROWS):
        for i in range(NA_SUB_QROWS):
            row = ((kind * (NA_QROWS // NA_SUB_QROWS) + s) * NA_SUB_QROWS + i) * NA_SUB_KROWS
            tiles = [band_ref[slot_ref[row + m]] for m in range(NA_SUB_KROWS)]
            o_ref[s, i * GRID_W:(i + 1) * GRID_W, :] = jnp.concatenate(tiles, axis=1)


def _na_kernel(q_ref, k_ref, v_ref, kc_ref, vc_ref, bias_ref, o_ref, *, grid_rows, offsets):
    j = pl.program_id(2)
    nq = grid_rows // NA_QROWS
    kind = jnp.where(j == 0, 0, jnp.where(j == nq - 1, 2, 1))
    base_row = jnp.clip(NA_QROWS * j - NA_ROWS // 2, 0, grid_rows - NA_KROWS)
    nk = NA_SUB_KROWS * GRID_W
    nsq = NA_SUB_QROWS * GRID_W
    for s in range(NA_QROWS // NA_SUB_QROWS):
        off = jnp.where(j == 0, int(offsets[0, s]), jnp.where(j == nq - 1, int(offsets[2, s]), int(offsets[1, s])))
        start = pl.multiple_of((base_row + off) * GRID_W, (NA_ROWS // 2) * GRID_W)
        rows = slice(s * nsq, (s + 1) * nsq)
        q = (q_ref[rows, :].astype(F32) * (HEAD_DIM ** -0.5)).astype(BF16)
        s_loc = _dot_nt(q, k_ref[pl.ds(start, nk), :]) + bias_ref[kind, s]
        s_ctx = _dot_nt(q, kc_ref[...])
        mx = jnp.maximum(jnp.max(s_loc, axis=-1, keepdims=True), jnp.max(s_ctx, axis=-1, keepdims=True))
        e_loc = jnp.exp(s_loc - mx)
        e_ctx = jnp.exp(s_ctx - mx)
        den = jnp.sum(e_loc, axis=-1, keepdims=True) + jnp.sum(e_ctx, axis=-1, keepdims=True)
        o = _dot(e_loc.astype(BF16), v_ref[pl.ds(start, nk), :]) + _dot(e_ctx.astype(BF16), vc_ref[...])
        o_ref[rows, :] = (o / den).astype(o_ref.dtype)


def neighbourhood_attention(z, cols, bias_tables, layer, batch, seq, n_ctx):
    bias, offsets = bias_tables
    assert np.all(offsets % (NA_ROWS // 2) == 0)
    grid_rows = seq // GRID_W
    assert grid_rows % NA_QROWS == 0 and grid_rows >= 2 * NA_KROWS
    nq = grid_rows // NA_QROWS
    qb = NA_QROWS * GRID_W
    qc, kc, vc = (cols[n] // HEAD_DIM for n in ("na_q", "na_k", "na_v"))
    ctx_blk0 = batch * seq // n_ctx
    return pl.pallas_call(
        functools.partial(_na_kernel, grid_rows=grid_rows, offsets=offsets),
        out_shape=SDS((batch * seq, NA_WIDTH), BF16), grid=(N_MIX_HEADS, batch, nq),
        in_specs=[BS((qb, HEAD_DIM), lambda h, b, j: (b * nq + j, qc + h)),
                  BS((seq, HEAD_DIM), lambda h, b, j: (b, kc + h)),
                  BS((seq, HEAD_DIM), lambda h, b, j: (b, vc + h)),
                  BS((n_ctx, HEAD_DIM), lambda h, b, j: (ctx_blk0 + b, kc + h)),
                  BS((n_ctx, HEAD_DIM), lambda h, b, j: (ctx_blk0 + b, vc + h)),
                  BS((None, 3, None) + bias.shape[3:], lambda h, b, j: (layer, 0, h, 0, 0, 0))],
        out_specs=BS((qb, HEAD_DIM), lambda h, b, j: (b * nq + j, h)),
        compiler_params=_params("parallel", "parallel", "parallel"), name="neighbourhood_attention")(
            z, z, z, z, z, bias)


def _ctx_attn_kernel(q_ref, k_ref, v_ref, o_ref):
    q = (q_ref[...].astype(F32) * (HEAD_DIM ** -0.5)).astype(BF16)
    s = _dot_nt(q, k_ref[...])
    e = jnp.exp(s - jnp.max(s, axis=-1, keepdims=True))
    o = _dot(e.astype(BF16), v_ref[...]) / jnp.sum(e, axis=-1, keepdims=True)
    o_ref[...] = o.astype(o_ref.dtype)


def context_attention(z, cols, batch, seq, n_ctx):
    qc, kc, vc = (cols[n] // HEAD_DIM for n in ("na_q", "na_k", "na_v"))
    blk0 = batch * seq // n_ctx
    spec = lambda col: BS((n_ctx, HEAD_DIM), lambda b, h: (blk0 + b, col + h))
    return pl.pallas_call(
        _ctx_attn_kernel, out_shape=SDS((batch * n_ctx, NA_WIDTH), BF16), grid=(batch, N_MIX_HEADS),
        in_specs=[spec(qc), spec(kc), spec(vc)],
        out_specs=BS((n_ctx, HEAD_DIM), lambda b, h: (b, h)),
        compiler_params=_params("parallel", "parallel"), name="context_attention")(z, z, z)


def _rope(t_ref, cos_ref, sin_ref):
    t = t_ref[...].astype(F32)
    lane = lax.broadcasted_iota(jnp.int32, t.shape, 1)
    quarter = HEAD_DIM // 4
    partner = jnp.where((lane % (2 * quarter)) < quarter,
                        pltpu.roll(t, HEAD_DIM - quarter, 1), pltpu.roll(t, quarter, 1))
    return t * cos_ref[...] + partner * sin_ref[...]


def _rope_tables(n, identity):
    if identity:
        return jnp.ones((n, HEAD_DIM), F32), jnp.zeros((n, HEAD_DIM), F32)
    nf = HEAD_DIM // 4
    pos = jnp.arange(n)
    p_row = (pos // GRID_W).astype(F32)
    p_col = (pos % GRID_W).astype(F32)
    inv = ROPE_BASE ** (-jnp.arange(nf, dtype=F32) / nf)
    a_row = p_row[:, None] * inv[None, :]
    a_col = p_col[:, None] * inv[None, :]
    cos = jnp.concatenate([jnp.cos(a_row), jnp.cos(a_row), jnp.cos(a_col), jnp.cos(a_col)], axis=-1)
    sin = jnp.concatenate([-jnp.sin(a_row), jnp.sin(a_row), -jnp.sin(a_col), jnp.sin(a_col)], axis=-1)
    return cos, sin


def _decay_tables(dec_f, dec_b):
    log_gf = jnp.log1p(-jnp.exp2(dec_f.astype(F32)))
    log_gb = jnp.log1p(-jnp.exp2(dec_b.astype(F32)))
    idx = jnp.arange(CHUNK, dtype=F32)
    rel = idx[:, None] - idx[None, :]
    d_f = jnp.where(rel >= 0, jnp.exp(log_gf[:, None, None] * jnp.maximum(rel, 0.0)), 0.0)
    d_b = jnp.where(rel < 0, jnp.exp(log_gb[:, None, None] * jnp.maximum(-rel, 0.0)), 0.0)
    bcast = lambda v: jnp.broadcast_to(v[:, :, None], v.shape + (RET_DK,))
    return dict(
        log_gf=log_gf, log_gb=log_gb,
        dfb=d_f + d_b,
        qdf=bcast(jnp.exp(log_gf[:, None] * (idx + 1.0))),
        qdb=bcast(jnp.exp(log_gb[:, None] * (CHUNK - idx))),
        kdf=bcast(jnp.exp(log_gf[:, None] * (CHUNK - 1.0 - idx))),
        kdb=bcast(jnp.exp(log_gb[:, None] * idx)),
        cdf=jnp.broadcast_to(jnp.exp(log_gf * CHUNK)[:, None, None], (log_gf.shape[0], 1, RET_DV)),
        cdb=jnp.broadcast_to(jnp.exp(log_gb * CHUNK)[:, None, None], (log_gb.shape[0], 1, RET_DV)),
    )


def _ctx_state_kernel(k_ref, v_ref, wf_ref, wb_ref, sf_ref, sb_ref):
    k = k_ref[...].astype(F32) * (RET_DK ** -0.5)
    v = v_ref[...]
    sf_ref[...] = _dot((k * wf_ref[...]).T.astype(BF16), v)
    sb_ref[...] = _dot((k * wb_ref[...]).T.astype(BF16), v)


def context_state(z, cols, tabs, batch, seq, n_ctx):
    pos = jnp.arange(n_ctx, dtype=F32)
    bcast = lambda v: jnp.broadcast_to(v[:, :, None], v.shape + (RET_DK,))
    w_f = bcast(jnp.exp(tabs["log_gf"][:, None] * ((n_ctx - 1.0) - pos)))
    w_b = bcast(jnp.exp(tabs["log_gb"][:, None] * pos))
    kc, vc = cols["ret_k"] // RET_DK, cols["ret_v"] // RET_DV
    blk0 = batch * seq // n_ctx
    out = SDS((batch, N_MIX_HEADS, RET_DK, RET_DV), F32)
    tab_spec = BS((None, n_ctx, RET_DK), lambda b, h: (h, 0, 0))
    out_spec = BS((None, None, RET_DK, RET_DV), lambda b, h: (b, h, 0, 0))
    return pl.pallas_call(
        _ctx_state_kernel, out_shape=(out, out), grid=(batch, N_MIX_HEADS),
        in_specs=[BS((n_ctx, RET_DK), lambda b, h: (blk0 + b, kc + h)),
                  BS((n_ctx, RET_DV), lambda b, h: (blk0 + b, vc + h)), tab_spec, tab_spec],
        out_specs=(out_spec, out_spec),
        compiler_params=_params("parallel", "parallel"), name="context_state")(z, z, w_f, w_b)


def _ret_state_kernel(kf_ref, vf_ref, cosf_ref, sinf_ref, kb_ref, vb_ref, cosb_ref, sinb_ref,
                      kdf_ref, kdb_ref, cdf_ref, cdb_ref, s0f_ref, s0b_ref, sf_ref, sb_ref, st_ref, *, cpg):
    seq_id = pl.program_id(1) * pl.num_programs(2) + pl.program_id(2)

    @pl.when(pl.program_id(0) == 0)
    def _():
        st_ref[2 * seq_id] = s0f_ref[...]
        st_ref[2 * seq_id + 1] = s0b_ref[...]

    def scan(k_ref, v_ref, cos_ref, sin_ref, kd_ref, cd_ref, out_ref, direction, order):
        kr = _rope(k_ref, cos_ref, sin_ref) * (RET_DK ** -0.5)
        slot = 2 * seq_id + direction
        s = st_ref[slot]
        for c in order:
            rows = slice(c * CHUNK, (c + 1) * CHUNK)
            out_ref[c] = s.astype(out_ref.dtype)
            kc = (kr[rows, :] * kd_ref[...]).T.astype(BF16)
            s = s * cd_ref[...] + _dot(kc, v_ref[rows, :])
        st_ref[slot] = s

    scan(kf_ref, vf_ref, cosf_ref, sinf_ref, kdf_ref, cdf_ref, sf_ref, 0, range(cpg))
    scan(kb_ref, vb_ref, cosb_ref, sinb_ref, kdb_ref, cdb_ref, sb_ref, 1, reversed(range(cpg)))


def _ret_out_kernel(q_ref, k_ref, v_ref, g_ref, cos_ref, sin_ref, sf_ref, sb_ref, dfb_ref, qdf_ref, qdb_ref,
                    o_ref, *, cpg):
    qr = _rope(q_ref, cos_ref, sin_ref)
    kr = (_rope(k_ref, cos_ref, sin_ref) * (RET_DK ** -0.5)).astype(BF16)
    for c in range(cpg):
        rows = slice(c * CHUNK, (c + 1) * CHUNK)
        qc = qr[rows, :]
        p = (_dot_nt(qc.astype(BF16), kr[rows, :]) * dfb_ref[...]).astype(BF16)
        o = (_dot(p, v_ref[rows, :]) + _dot((qc * qdf_ref[...]).astype(BF16), sf_ref[c])
             + _dot((qc * qdb_ref[...]).astype(BF16), sb_ref[c]))
        mu = jnp.mean(o, axis=-1, keepdims=True)
        var = jnp.mean(jnp.square(o - mu), axis=-1, keepdims=True)
        on = (o - mu) * lax.rsqrt(var + EPS)
        g = g_ref[rows, :].astype(F32)
        o_ref[rows, :] = (on * (g * jax.nn.sigmoid(g))).astype(o_ref.dtype)


def retention(z, cols, tabs, s0f, s0b, row0, n, batch, rope_identity):
    nc = n // CHUNK
    cpg = min(RET_CHUNKS_PER_STEP, nc)
    ng = nc // cpg
    gr = cpg * CHUNK
    blk0 = row0 // gr
    qc, kc = cols["ret_q"] // RET_DK, cols["ret_k"] // RET_DK
    vc, gc = cols["ret_v"] // RET_DV, cols["ret_g"] // RET_DV
    cos, sin = _rope_tables(n, rope_identity)
    BSg = lambda shape, f: BS(shape, lambda g, b, h: f(b, h, g))
    fwd = lambda b, h, g: blk0 + b * ng + g
    bwd = lambda b, h, g: blk0 + b * ng + (ng - 1 - g)
    head_tab = lambda w: BSg((None, CHUNK, w), lambda b, h, g: (h, 0, 0))
    c_tab = BSg((None, 1, RET_DV), lambda b, h, g: (h, 0, 0))
    s0_spec = BSg((None, None, RET_DK, RET_DV), lambda b, h, g: (b, h, 0, 0))
    st_shape = SDS((batch, N_MIX_HEADS, nc, RET_DK, RET_DV), BF16)
    st_spec = lambda rev: BSg((None, None, cpg, RET_DK, RET_DV),
                              lambda b, h, g: (b, h, (ng - 1 - g) if rev else g, 0, 0))
    sf, sb = pl.pallas_call(
        functools.partial(_ret_state_kernel, cpg=cpg), out_shape=(st_shape, st_shape),
        grid=(ng, batch, N_MIX_HEADS),
        in_specs=[BSg((gr, RET_DK), lambda b, h, g: (fwd(b, h, g), kc + h)),
                  BSg((gr, RET_DV), lambda b, h, g: (fwd(b, h, g), vc + h)),
                  BSg((gr, RET_DK), lambda b, h, g: (g, 0)), BSg((gr, RET_DK), lambda b, h, g: (g, 0)),
                  BSg((gr, RET_DK), lambda b, h, g: (bwd(b, h, g), kc + h)),
                  BSg((gr, RET_DV), lambda b, h, g: (bwd(b, h, g), vc + h)),
                  BSg((gr, RET_DK), lambda b, h, g: (ng - 1 - g, 0)),
                  BSg((gr, RET_DK), lambda b, h, g: (ng - 1 - g, 0)),
                  head_tab(RET_DK), head_tab(RET_DK), c_tab, c_tab, s0_spec, s0_spec],
        out_specs=(st_spec(False), st_spec(True)),
        scratch_shapes=[pltpu.VMEM((2 * batch * N_MIX_HEADS, RET_DK, RET_DV), F32)],
        compiler_params=_params("arbitrary", "arbitrary", "arbitrary"), name="retention_state")(
            z, z, cos, sin, z, z, cos, sin, tabs["kdf"], tabs["kdb"], tabs["cdf"], tabs["cdb"], s0f, s0b)
    return pl.pallas_call(
        functools.partial(_ret_out_kernel, cpg=cpg), out_shape=SDS((batch * n, RET_V_WIDTH), BF16),
        grid=(ng, batch, N_MIX_HEADS),
        in_specs=[BSg((gr, RET_DK), lambda b, h, g: (fwd(b, h, g), qc + h)),
                  BSg((gr, RET_DK), lambda b, h, g: (fwd(b, h, g), kc + h)),
                  BSg((gr, RET_DV), lambda b, h, g: (fwd(b, h, g), vc + h)),
                  BSg((gr, RET_DV), lambda b, h, g: (fwd(b, h, g), gc + h)),
                  BSg((gr, RET_DK), lambda b, h, g: (g, 0)), BSg((gr, RET_DK), lambda b, h, g: (g, 0)),
                  st_spec(False), st_spec(False),
                  head_tab(CHUNK), head_tab(RET_DK), head_tab(RET_DK)],
        out_specs=BSg((gr, RET_DV), lambda b, h, g: (b * ng + g, h)),
        compiler_params=_params("parallel", "parallel", "parallel"), name="retention_out")(
            z, z, z, z, cos, sin, sf, sb, tabs["dfb"], tabs["qdf"], tabs["qdb"])


def _dispatch_tables(routed, counts):
    t = routed.shape[0]
    n_assign = t * MOE_TOPK
    flat_e = routed[:, 0:MOE_TOPK].astype(jnp.int32).reshape(-1)
    rank = routed[:, MOE_TOPK:2 * MOE_TOPK].astype(jnp.int32).reshape(-1)
    flat_t = jnp.repeat(jnp.arange(t, dtype=jnp.int32), MOE_TOPK)
    counts = counts[0, :MOE_EXPERTS].astype(jnp.int32)
    padded = (counts + MOE_ROWS - 1) // MOE_ROWS * MOE_ROWS
    p_end = jnp.cumsum(padded)
    dest = (p_end - padded)[flat_e] + rank
    n_blocks = (n_assign + MOE_EXPERTS * (MOE_ROWS - 1)) // MOE_ROWS
    row_tok = jnp.zeros((n_blocks * MOE_ROWS,), jnp.int32).at[dest].set(
        flat_t, unique_indices=True, mode="promise_in_bounds")
    blk_start = jnp.arange(n_blocks, dtype=jnp.int32) * MOE_ROWS
    block_e = jnp.minimum(jnp.sum((p_end[None, :] <= blk_start[:, None]).astype(jnp.int32), axis=1),
                          MOE_EXPERTS - 1)
    n_used = (p_end[-1] // MOE_ROWS).astype(jnp.int32).reshape(1)
    return block_e, n_used, row_tok.reshape(n_blocks, 1, MOE_ROWS), dest.reshape(t, MOE_TOPK).astype(jnp.int32)


def _start_row_gather(src_hbm, idx_ref, idx_of_row, dst, sem, n):
    def issue(r, carry):
        pltpu.make_async_copy(src_hbm.at[pl.ds(idx_ref[0, idx_of_row(r)], 1)], dst.at[pl.ds(r, 1)], sem).start()
        return carry
    lax.fori_loop(0, n, issue, 0, unroll=8)


def _wait_row_gather(src_hbm, dst, sem):
    pltpu.make_async_copy(src_hbm.at[pl.ds(0, dst.shape[0])], dst, sem).wait()


MOE_GATHER_AHEAD = 2


def _moe_kernel(be_ref, nu_ref, *refs):
    ahead = MOE_GATHER_AHEAD
    tok_refs = refs[:ahead + 1]
    h_hbm, w1_ref, w3_ref, w2_ref, y_ref, xb, sem = refs[ahead + 1:]
    i = pl.program_id(0)
    n_used = nu_ref[0]
    n_slots = ahead + 1
    slot = i % n_slots
    row = lambda r: r

    for a in range(ahead):
        @pl.when((i == 0) & (a < n_used))
        def _(a=a):
            _start_row_gather(h_hbm, tok_refs[a], row, xb.at[a], sem.at[a], MOE_ROWS)

    @pl.when(i + ahead < n_used)
    def _():
        nxt = (i + ahead) % n_slots
        _start_row_gather(h_hbm, tok_refs[ahead], row, xb.at[nxt], sem.at[nxt], MOE_ROWS)

    @pl.when(i < n_used)
    def _():
        _wait_row_gather(h_hbm, xb.at[slot], sem.at[slot])
        half = xb.shape[2]
        x_lo, x_hi = (v.astype(BF16) for v in _unpack_halves(xb[slot]))
        up = lambda w_ref: _dot(x_lo, w_ref[0:half, :]) + _dot(x_hi, w_ref[half:2 * half, :])
        h1 = up(w1_ref)
        h3 = up(w3_ref)
        hm = (h1 * jax.nn.sigmoid(h1) * h3).astype(BF16)
        y_ref[...] = _pack_halves(_dot(hm, w2_ref[...]))

    @pl.when(i >= n_used)
    def _():
        y_ref[...] = jnp.zeros_like(y_ref)


def moe_experts(h2, tables, w1, w3, w2):
    block_e, n_used, row_tok, _ = tables
    n_blocks = row_tok.shape[0]
    d, hid = w1.shape[1], w1.shape[2]
    assert h2.shape[1] == d // 2 and h2.dtype == jnp.uint32
    grid_spec = pltpu.PrefetchScalarGridSpec(
        num_scalar_prefetch=2, grid=(n_blocks,),
        in_specs=[BS((None, 1, MOE_ROWS), lambda i, be, nu, a=a: (jnp.minimum(i + a, n_blocks - 1), 0, 0),
                     memory_space=pltpu.SMEM) for a in range(MOE_GATHER_AHEAD + 1)] + [
                  BS(memory_space=pl.ANY),
                  BS((None, d, hid), lambda i, be, nu: (be[i], 0, 0)),
                  BS((None, d, hid), lambda i, be, nu: (be[i], 0, 0)),
                  BS((None, hid, d), lambda i, be, nu: (be[i], 0, 0))],
        out_specs=BS((MOE_ROWS, d // 2), lambda i, be, nu: (i, 0)),
        scratch_shapes=[pltpu.VMEM((MOE_GATHER_AHEAD + 1, MOE_ROWS, d // 2), jnp.uint32),
                        pltpu.SemaphoreType.DMA((MOE_GATHER_AHEAD + 1,))])
    return pl.pallas_call(
        _moe_kernel, out_shape=SDS((n_blocks * MOE_ROWS, d // 2), jnp.uint32), grid_spec=grid_spec,
        compiler_params=_params("arbitrary"), name="moe_experts")(
            block_e, n_used, *([row_tok] * (MOE_GATHER_AHEAD + 1)), h2, w1, w3, w2)


def _combine_kernel(pos_ref, pos_next_ref, y_hbm, x_ref, w_ref, g_ref, ng_ref, nsh_ref, nsc_ref, *refs,
                    keep_stream):
    outs, (yb, sem) = refs[:-2], refs[-2:]
    i = pl.program_id(0)
    tt = COMBINE_ROWS
    slot = i % 2
    idx = lambda r: r

    @pl.when(i == 0)
    def _():
        _start_row_gather(y_hbm, pos_ref, idx, yb.at[0], sem.at[0], MOE_TOPK * tt)

    @pl.when(i + 1 < pl.num_programs(0))
    def _():
        _start_row_gather(y_hbm, pos_next_ref, idx, yb.at[1 - slot], sem.at[1 - slot], MOE_TOPK * tt)

    _wait_row_gather(y_hbm, yb.at[slot], sem.at[slot])
    w = w_ref[...]
    w0, w1 = (w[:, 2 * MOE_TOPK + k:2 * MOE_TOPK + k + 1] for k in range(MOE_TOPK))
    half = yb.shape[2]
    halves = (slice(0, half), slice(half, 2 * half))
    ya = _unpack_halves(yb[slot, 0:tt, :])
    yb_ = _unpack_halves(yb[slot, tt:2 * tt, :])
    xn = [x_ref[:, s] + g_ref[:, s] * (a * w0 + b * w1) for s, a, b in zip(halves, ya, yb_)]
    if keep_stream:
        for s, v in zip(halves, xn):
            outs[0][:, s] = v
    h_ref = outs[-1]
    ms = sum(jnp.sum(v * v, axis=-1, keepdims=True) for v in xn) / (2 * half)
    inv = lax.rsqrt(ms + EPS)
    for s, v in zip(halves, xn):
        h_ref[:, s] = ((v * inv * ng_ref[:, s]) * (1.0 + nsc_ref[:, s]) + nsh_ref[:, s]).astype(h_ref.dtype)


def moe_combine(x1, y, pos, routed, gate, rows, seg_rows, norm_g, norm_shift, norm_scale, out_dtype, keep_stream):
    d = x1.shape[1]
    tt = COMBINE_ROWS
    nt = rows // tt
    seg = _seg_map(seg_rows, tt, gate.shape[0])
    nseg = _seg_map(seg_rows, tt, norm_shift.shape[0])
    pos3 = pos.reshape(nt, tt, MOE_TOPK).transpose(0, 2, 1).reshape(nt, 1, MOE_TOPK * tt)
    row_spec = BS((tt, d), lambda i: (i, 0))
    out_shape = ((SDS((rows, d), F32),) if keep_stream else ()) + (SDS((rows, d), out_dtype),)
    res = pl.pallas_call(
        functools.partial(_combine_kernel, keep_stream=keep_stream), out_shape=out_shape, grid=(nt,),
        in_specs=[BS((None, 1, MOE_TOPK * tt), lambda i: (i, 0, 0), memory_space=pltpu.SMEM),
                  BS((None, 1, MOE_TOPK * tt), lambda i: (jnp.minimum(i + 1, nt - 1), 0, 0),
                     memory_space=pltpu.SMEM),
                  BS(memory_space=pl.ANY), row_spec,
                  BS((tt, ROUTER_LANES), lambda i: (i, 0)), BS((None, 1, d), lambda i: (seg(i), 0, 0)),
                  BS((1, d), lambda i: (0, 0)), BS((None, 1, d), lambda i: (nseg(i), 0, 0)),
                  BS((None, 1, d), lambda i: (nseg(i), 0, 0))],
        out_specs=tuple(row_spec for _ in out_shape),
        scratch_shapes=[pltpu.VMEM((2, MOE_TOPK * tt, d // 2), jnp.uint32), pltpu.SemaphoreType.DMA((2,))],
        compiler_params=_params("arbitrary"), name="moe_combine")(
            pos3, pos3, y, x1, routed, gate, norm_g, norm_shift, norm_scale)
    return res if keep_stream else res[0]


def kernel(x, c, ctx, c_ctx, ada_w, ada_b, norm1_g, w_in, gm_norm_g, gm_ws, gm_bs, na_rpb, ret_decay_fwd,
           ret_decay_bwd, w_branch_a, w_branch_b, w_branch_c, w_out, norm2_g, moe_w_group, moe_w_expert,
           moe_w1, moe_w3, moe_w2, final_norm_g):
    batch, seq, d = x.shape
    n_ctx = ctx.shape[1]
    depth = ada_w.shape[0]
    t_lat, t_ctx = batch * seq, batch * n_ctx
    t_all = t_lat + t_ctx
    cols, in_total = _col_offsets(d)
    assert w_in.shape[2] == in_total and seq % ROW_TILE == 0 and t_ctx % ROW_TILE == 0
    assert 8 >= batch + 1

    cond = jnp.concatenate([c, c_ctx[None, :], jnp.zeros((8 - batch - 1, d), c.dtype)], axis=0)
    mod = ada_modulation(jax.nn.silu(cond).astype(BF16), ada_w, ada_b)[:, :batch + 1]
    mod = mod.reshape(depth, batch + 1, 6, 1, d)

    na_bias = _na_bias_tables(na_rpb, seq // GRID_W)
    stream = (x.reshape(t_lat, d), ctx.reshape(t_ctx, d))
    h = norm_mod(stream, norm1_g[0].reshape(1, d), mod[0, :, 0], mod[0, :, 1], seq, BF16)
    for l in range(depth):
        need_ctx = l < depth - 1
        rows = t_all if need_ctx else t_lat
        sh1, sc1, g1, sh2, sc2, g2 = (mod[l, :, k] for k in range(6))

        z, (w_a, w_b, w_c, w_o, w1, w3, w2) = in_proj(
            h, w_in, l, t_all, BF16, [w_branch_a, w_branch_b, w_branch_c, w_out, moe_w1, moe_w3, moe_w2])

        a = chunk_gmlp(z, cols, gm_norm_g[l], gm_ws[l], gm_bs[l], rows)
        bb = neighbourhood_attention(z, cols, na_bias, l, batch, seq, n_ctx)
        tabs = _decay_tables(ret_decay_fwd[l], ret_decay_bwd[l])
        s0f, s0b = context_state(z, cols, tabs, batch, seq, n_ctx)
        r = retention(z, cols, tabs, s0f, s0b, 0, seq, batch, False)
        if need_ctx:
            bb = jnp.concatenate([bb, context_attention(z, cols, batch, seq, n_ctx)], axis=0)
            zeros = jnp.zeros_like(s0f)
            r = jnp.concatenate([r, retention(z, cols, tabs, zeros, zeros, t_lat, n_ctx, batch, True)], axis=0)

        y = merge_branches(z, a, bb, r, w_a, w_b, w_c, rows)
        x1 = matmul_residual(y, w_o, stream, g1, rows, seq)

        w_router = jnp.concatenate(
            [moe_w_group[l], moe_w_expert[l],
             jnp.zeros((d, ROUTER_LANES - MOE_GROUPS - MOE_EXPERTS), F32)], axis=1).astype(BF16)
        h2, routed, counts = norm_router(x1, norm2_g[l].reshape(1, d), sh2, sc2, w_router, rows, seq)
        tables = _dispatch_tables(routed, counts)
        y_moe = moe_experts(h2, tables, w1, w3, w2)
        if need_ctx:
            xs, h = moe_combine(x1, y_moe, tables[3], routed, g2, rows, seq, norm1_g[l + 1].reshape(1, d),
                                mod[l + 1, :, 0], mod[l + 1, :, 1], BF16, True)
            stream = (xs, xs)
        else:
            zero = jnp.zeros((1, 1, d), F32)
            out = moe_combine(x1, y_moe, tables[3], routed, g2, rows, seq, final_norm_g.reshape(1, d),
                              zero, zero, F32, False)
    return out.reshape(batch, seq, d)
```

```python
import functools

import jax
import jax.numpy as jnp
import numpy as np
from jax import lax
from jax.experimental import pallas as pl
from jax.experimental.pallas import tpu as pltpu

F32 = jnp.float32
BF16 = jnp.bfloat16
SDS = jax.ShapeDtypeStruct
BS = pl.BlockSpec

EPS = 1e-6
NEG_INF = -1e30
HEAD_DIM = 128
GRID_W = 64
NA_ROWS = 8
NA_COLS = 16
NA_QROWS = 16
NA_KROWS = NA_QROWS + NA_ROWS
NA_SUB_QROWS = 4
NA_SUB_KROWS = NA_SUB_QROWS + NA_ROWS
RET_DK = 128
RET_DV = 256
CHUNK = 128
RET_CHUNKS_PER_STEP = 16
ROPE_BASE = 10000.0
MOE_GROUPS = 4
MOE_EPG = 8
MOE_EXPERTS = MOE_GROUPS * MOE_EPG
MOE_TOPK = 2
ROUTER_LANES = 128
N_MIX_HEADS = 8
GM_WIDTH = N_MIX_HEADS * HEAD_DIM
NA_WIDTH = N_MIX_HEADS * HEAD_DIM
RET_QK_WIDTH = N_MIX_HEADS * RET_DK
RET_V_WIDTH = N_MIX_HEADS * RET_DV

V7X_VMEM_BYTES = 64 * 1024 * 1024
VMEM_LIMIT = V7X_VMEM_BYTES - 8 * 1024 * 1024

ROW_TILE = 512
COL_TILE = 1024
MOE_ROWS = 256
COMBINE_ROWS = 128


def _params(*sem):
    return pltpu.CompilerParams(dimension_semantics=sem, vmem_limit_bytes=VMEM_LIMIT)


def _col_offsets(d):
    off, out = 0, {}
    for name, width in (("gate_a", d), ("gate_b", d), ("gate_c", d), ("gm_u", GM_WIDTH), ("gm_v", GM_WIDTH),
                        ("na_q", NA_WIDTH), ("na_k", NA_WIDTH), ("na_v", NA_WIDTH),
                        ("ret_q", RET_QK_WIDTH), ("ret_k", RET_QK_WIDTH), ("ret_v", RET_V_WIDTH),
                        ("ret_g", RET_V_WIDTH)):
        out[name] = off
        off += width
    return out, off


def _dot(a, b):
    return jnp.dot(a, b, preferred_element_type=F32)


def _dot_nt(a, b):
    return lax.dot_general(a, b, (((1,), (1,)), ((), ())), preferred_element_type=F32)


_HI16 = 0xFFFF0000


def _pack_halves(v):
    n = v.shape[1] // 2
    bits = lax.bitcast_convert_type(v.astype(BF16).astype(F32), jnp.uint32)
    return lax.shift_right_logical(bits[:, :n], jnp.uint32(16)) | (bits[:, n:] & jnp.uint32(_HI16))


def _unpack_halves(u):
    lo = lax.bitcast_convert_type(lax.shift_left(u, jnp.uint32(16)), F32)
    hi = lax.bitcast_convert_type(u & jnp.uint32(_HI16), F32)
    return lo, hi


def _norm_mod(x_ref, g_ref, sh_ref, sc_ref):
    x = x_ref[...]
    ms = jnp.mean(x * x, axis=-1, keepdims=True)
    y = x * lax.rsqrt(ms + EPS) * g_ref[...]
    return y * (1.0 + sc_ref[...]) + sh_ref[...]


def _norm_kernel(x_ref, xt_ref, g_ref, sh_ref, sc_ref, o_ref, *, n_main):
    @pl.when(pl.program_id(0) < n_main)
    def _():
        o_ref[...] = _norm_mod(x_ref, g_ref, sh_ref, sc_ref).astype(o_ref.dtype)

    @pl.when(pl.program_id(0) >= n_main)
    def _():
        o_ref[...] = _norm_mod(xt_ref, g_ref, sh_ref, sc_ref).astype(o_ref.dtype)


def _first_lane(mask, lane):
    return jnp.min(jnp.where(mask, lane, float(ROUTER_LANES)), axis=-1, keepdims=True)


def _route(logits, run_ref):
    n = logits.shape[0]
    lane = lax.broadcasted_iota(jnp.int32, logits.shape, 1).astype(F32)
    neg = -jnp.inf
    is_g = lane < MOE_GROUPS
    gl = jnp.where(is_g, logits, neg)
    eg = jnp.exp(gl - jnp.max(gl, axis=-1, keepdims=True))
    pg = eg / jnp.sum(eg, axis=-1, keepdims=True)
    p_sel = jnp.max(pg, axis=-1, keepdims=True)
    grp = _first_lane((pg == p_sel) & is_g, lane)
    lo = MOE_GROUPS + MOE_EPG * grp
    is_e = (lane >= lo) & (lane < lo + MOE_EPG)
    el = jnp.where(is_e, logits, neg)
    ee = jnp.exp(el - jnp.max(el, axis=-1, keepdims=True))
    pe = ee / jnp.sum(ee, axis=-1, keepdims=True)
    t1 = jnp.max(pe, axis=-1, keepdims=True)
    i1 = _first_lane((pe == t1) & is_e, lane)
    rest = jnp.where(is_e & (lane != i1), pe, -1.0)
    t2 = jnp.max(rest, axis=-1, keepdims=True)
    i2 = _first_lane(rest == t2, lane)
    tsum = t1 + t2
    w1, w2 = p_sel * t1 / tsum, p_sel * t2 / tsum
    e1, e2 = i1 - MOE_GROUPS, i2 - MOE_GROUPS
    oh1, oh2 = lane == e1, lane == e2
    cnt = jnp.where(oh1 | oh2, 1.0, 0.0)
    r_i = lax.broadcasted_iota(jnp.int32, (n, n), 0)
    c_i = lax.broadcasted_iota(jnp.int32, (n, n), 1)
    tri = jnp.where(c_i < r_i, 1.0, 0.0).astype(BF16)
    before = _dot(tri, cnt.astype(BF16)) + run_ref[...]
    r1 = jnp.sum(jnp.where(oh1, before, 0.0), axis=-1, keepdims=True)
    r2 = jnp.sum(jnp.where(oh2, before, 0.0), axis=-1, keepdims=True)
    run_ref[...] = run_ref[...] + jnp.sum(cnt, axis=0, keepdims=True)
    slab = jnp.zeros(logits.shape, F32)
    for k, v in enumerate((e1, e2, r1, r2, w1, w2)):
        slab = jnp.where(lane == float(k), v, slab)
    return slab


def _norm_router_kernel(x_ref, g_ref, sh_ref, sc_ref, wr_ref, h_ref, rt_ref, cnt_ref, run_ref):
    @pl.when(pl.program_id(0) == 0)
    def _():
        run_ref[...] = jnp.zeros_like(run_ref)
    h = _norm_mod(x_ref, g_ref, sh_ref, sc_ref)
    h_ref[...] = _pack_halves(h)
    rt_ref[...] = _route(_dot(h.astype(BF16), wr_ref[...]), run_ref)
    cnt_ref[...] = run_ref[...]


def _seg_map(seg_rows, tile, nseg):
    per = seg_rows // tile
    return lambda i: jnp.minimum(i // per, nseg - 1)


def norm_mod(stream, g, shift, scale, seg_rows, out_dtype):
    main, tail = stream
    d = main.shape[1]
    tr = ROW_TILE // 2
    seg = _seg_map(seg_rows, tr, shift.shape[0])
    n_main = main.shape[0] // tr
    rows = main.shape[0] + tail.shape[0]
    return pl.pallas_call(
        functools.partial(_norm_kernel, n_main=n_main), out_shape=SDS((rows, d), out_dtype), grid=(rows // tr,),
        in_specs=[BS((tr, d), lambda i: (jnp.minimum(i, n_main - 1), 0)),
                  BS((tr, d), lambda i: (jnp.maximum(i - n_main, 0), 0)), BS((1, d), lambda i: (0, 0)),
                  BS((None, 1, d), lambda i: (seg(i), 0, 0)), BS((None, 1, d), lambda i: (seg(i), 0, 0))],
        out_specs=BS((tr, d), lambda i: (i, 0)),
        compiler_params=_params("parallel"), name="norm_mod")(main, tail, g, shift, scale)


def norm_router(x, g, shift, scale, w_router, rows, seg_rows):
    d = x.shape[1]
    tr = ROW_TILE // 2
    seg = _seg_map(seg_rows, tr, shift.shape[0])
    return pl.pallas_call(
        _norm_router_kernel,
        out_shape=(SDS((rows, d // 2), jnp.uint32), SDS((rows, ROUTER_LANES), F32),
                   SDS((1, ROUTER_LANES), F32)),
        grid=(rows // tr,),
        in_specs=[BS((tr, d), lambda i: (i, 0)), BS((1, d), lambda i: (0, 0)),
                  BS((None, 1, d), lambda i: (seg(i), 0, 0)), BS((None, 1, d), lambda i: (seg(i), 0, 0)),
                  BS((d, ROUTER_LANES), lambda i: (0, 0))],
        out_specs=(BS((tr, d // 2), lambda i: (i, 0)), BS((tr, ROUTER_LANES), lambda i: (i, 0)),
                   BS((1, ROUTER_LANES), lambda i: (0, 0))),
        scratch_shapes=[pltpu.VMEM((1, ROUTER_LANES), F32)],
        compiler_params=_params("arbitrary"), name="norm_router")(x, g, shift, scale, w_router)


def _ada_kernel(a_ref, w_ref, b_ref, o_ref):
    o_ref[...] = _dot(a_ref[...], w_ref[...].astype(BF16)) + b_ref[...]


def ada_modulation(act, ada_w, ada_b):
    depth, d, n = ada_w.shape
    tn = ROW_TILE
    return pl.pallas_call(
        _ada_kernel, out_shape=SDS((depth, act.shape[0], n), F32), grid=(depth, n // tn),
        in_specs=[BS(act.shape, lambda l, j: (0, 0)), BS((None, d, tn), lambda l, j: (l, 0, j)),
                  BS((None, 1, tn), lambda l, j: (l, 0, j))],
        out_specs=BS((None, act.shape[0], tn), lambda l, j: (l, 0, j)),
        compiler_params=_params("parallel", "parallel"), name="ada_modulation")(
            act, ada_w, ada_b.reshape(depth, 1, n))


def _in_proj_kernel(a_ref, b_ref, *refs, n_side, n_full, tail_cols):
    side_in, o_ref, side_out, bq_ref = refs[:n_side], refs[n_side], refs[n_side + 1:-1], refs[-1]
    j, i = pl.program_id(0), pl.program_id(1)
    @pl.when(i == 0)
    def _():
        bq_ref[...] = b_ref[...].astype(BF16)
    wanted = i < n_full
    for c in tail_cols:
        wanted = wanted | (j == c)

    @pl.when(wanted)
    def _():
        o_ref[...] = _dot(a_ref[...], bq_ref[...]).astype(o_ref.dtype)

    @pl.when(jnp.logical_not(wanted))
    def _():
        o_ref[...] = jnp.zeros_like(o_ref)
    for s_in, s_out in zip(side_in, side_out):
        s_out[...] = s_in[...].astype(BF16)


BF16_SUBLANES = 16


def in_proj(a, w, layer, rows, out_dtype, side, full_rows, tail_col_ranges):
    _, k, n = w.shape
    tm, tn = ROW_TILE, COL_TILE
    nj, ni = n // tn, rows // tm
    assert full_rows % tm == 0 and all(lo % tn == 0 and hi % tn == 0 for lo, hi in tail_col_ranges)
    tail_cols = tuple(sorted({c for lo, hi in tail_col_ranges for c in range(lo // tn, hi // tn)}))
    n_steps = nj * ni
    in_specs, out_specs, out_shapes = [], [], []
    for s in side:
        r, c = s.shape[-2:]
        n_mat = int(np.prod(s.shape[1:-2]))
        rps = BF16_SUBLANES
        while n_mat * (r // rps) > n_steps:
            rps *= 2
        assert r % rps == 0
        bpm = r // rps
        blk = lambda j, i, last=n_mat * bpm - 1: jnp.minimum(j * ni + i, last)
        if s.ndim == 4:
            in_specs.append(BS((None, None, rps, c),
                               lambda j, i, blk=blk, bpm=bpm: (layer, blk(j, i) // bpm, blk(j, i) % bpm, 0)))
            out_specs.append(BS((None, rps, c), lambda j, i, blk=blk, bpm=bpm: (blk(j, i) // bpm, blk(j, i) % bpm, 0)))
        else:
            in_specs.append(BS((None, rps, c), lambda j, i, blk=blk: (layer, blk(j, i), 0)))
            out_specs.append(BS((rps, c), lambda j, i, blk=blk: (blk(j, i), 0)))
        out_shapes.append(SDS(s.shape[1:], BF16))
    res = pl.pallas_call(
        functools.partial(_in_proj_kernel, n_side=len(side), n_full=full_rows // tm, tail_cols=tail_cols),
        out_shape=[SDS((rows, n), out_dtype)] + out_shapes, grid=(nj, ni),
        in_specs=[BS((tm, k), lambda j, i: (i, 0)),
                  BS((None, k, tn), lambda j, i: (layer, 0, j), pipeline_mode=pl.Buffered(1))] + in_specs,
        out_specs=[BS((tm, tn), lambda j, i: (i, j))] + out_specs,
        scratch_shapes=[pltpu.VMEM((k, tn), BF16)],
        compiler_params=_params("arbitrary", "arbitrary"), name="in_proj")(a, w, *side)
    return res[0], list(res[1:])


def _mm_res_kernel(a_ref, b_ref, x_ref, xt_ref, g_ref, o_ref, *, n_main):
    upd = g_ref[...] * _dot(a_ref[...], b_ref[...])

    @pl.when(pl.program_id(1) < n_main)
    def _():
        o_ref[...] = x_ref[...] + upd

    @pl.when(pl.program_id(1) >= n_main)
    def _():
        o_ref[...] = xt_ref[...] + upd


def matmul_residual(a, b, stream, gate, rows, seg_rows):
    k, n = b.shape
    tm, tn = ROW_TILE, COL_TILE
    seg = _seg_map(seg_rows, tm, gate.shape[0])
    main, tail = stream
    n_main = main.shape[0] // tm
    return pl.pallas_call(
        functools.partial(_mm_res_kernel, n_main=n_main), out_shape=SDS((rows, n), F32),
        grid=(n // tn, rows // tm),
        in_specs=[BS((tm, k), lambda j, i: (i, 0)), BS((k, tn), lambda j, i: (0, j)),
                  BS((tm, tn), lambda j, i: (jnp.minimum(i, n_main - 1), j)),
                  BS((tm, tn), lambda j, i: (jnp.maximum(i - n_main, 0), j)),
                  BS((None, 1, tn), lambda j, i: (seg(i), 0, j))],
        out_specs=BS((tm, tn), lambda j, i: (i, j)),
        compiler_params=_params("parallel", "parallel"), name="out_proj")(a, b, main, tail, gate)


def _merge_kernel(a_ref, b_ref, bt_ref, r_ref, rt_ref, ga_ref, gb_ref, gc_ref, wa_ref, wb_ref, wc_ref, o_ref, *,
                  n_main):
    is_tail = pl.program_id(1) >= n_main

    def branch(x, w_ref, g_ref):
        return jax.nn.sigmoid(g_ref[...].astype(F32)) * _dot(x, w_ref[...])
    y = (branch(a_ref[...], wa_ref, ga_ref)
         + branch(jnp.where(is_tail, bt_ref[...], b_ref[...]), wb_ref, gb_ref)
         + branch(jnp.where(is_tail, rt_ref[...], r_ref[...]), wc_ref, gc_ref))
    o_ref[...] = y.astype(o_ref.dtype)


def merge_branches(z, a, bb, r, w_a, w_b, w_c, rows):
    d = w_a.shape[1]
    tm, tn = ROW_TILE, COL_TILE
    nb = d // tn
    n_main = bb[0].shape[0] // tm
    row_spec = lambda arr: BS((tm, arr.shape[1]), lambda j, i: (i, 0))
    main_spec = lambda arr: BS((tm, arr.shape[1]), lambda j, i: (jnp.minimum(i, n_main - 1), 0))
    tail_spec = lambda arr: BS((tm, arr.shape[1]), lambda j, i: (jnp.maximum(i - n_main, 0), 0))
    w_spec = lambda arr: BS((arr.shape[0], tn), lambda j, i: (0, j))
    gate_spec = lambda g: BS((tm, tn), lambda j, i: (i, g * nb + j))
    return pl.pallas_call(
        functools.partial(_merge_kernel, n_main=n_main), out_shape=SDS((rows, d), BF16), grid=(nb, rows // tm),
        in_specs=[row_spec(a), main_spec(bb[0]), tail_spec(bb[1]), main_spec(r[0]), tail_spec(r[1]),
                  gate_spec(0), gate_spec(1), gate_spec(2), w_spec(w_a), w_spec(w_b), w_spec(w_c)],
        out_specs=BS((tm, tn), lambda j, i: (i, j)),
        compiler_params=_params("parallel", "parallel"), name="merge_branches")(
            a, bb[0], bb[1], r[0], r[1], z, z, z, w_a, w_b, w_c)


GM_CHUNKS_PER_STEP = 4


def _gmlp_kernel(u_ref, v_ref, ng_ref, ws_ref, bs_ref, o_ref):
    for c in range(GM_CHUNKS_PER_STEP):
        rows = slice(c * CHUNK, (c + 1) * CHUNK)
        v = jax.nn.gelu(v_ref[rows, :].astype(F32))
        mu = jnp.mean(v, axis=-1, keepdims=True)
        var = jnp.mean(jnp.square(v - mu), axis=-1, keepdims=True)
        vn = ((v - mu) * lax.rsqrt(var + EPS) * ng_ref[...]).astype(BF16)
        for g in range(N_MIX_HEADS):
            cols = slice(g * HEAD_DIM, (g + 1) * HEAD_DIM)
            mixed = _dot(ws_ref[g], vn[:, cols]) + bs_ref[g]
            u = jax.nn.gelu(u_ref[rows, cols].astype(F32))
            o_ref[rows, cols] = (u * mixed).astype(o_ref.dtype)


def chunk_gmlp(z, cols, norm_g, ws, bs, rows):
    tr = GM_CHUNKS_PER_STEP * CHUNK
    ub, vb = cols["gm_u"] // GM_WIDTH, cols["gm_v"] // GM_WIDTH
    bs_b = jnp.broadcast_to(bs[:, :, None], bs.shape + (HEAD_DIM,)).astype(F32)
    return pl.pallas_call(
        _gmlp_kernel, out_shape=SDS((rows, GM_WIDTH), BF16), grid=(rows // tr,),
        in_specs=[BS((tr, GM_WIDTH), lambda i: (i, ub)), BS((tr, GM_WIDTH), lambda i: (i, vb)),
                  BS((1, GM_WIDTH), lambda i: (0, 0)), BS(ws.shape, lambda i: (0, 0, 0)),
                  BS(bs_b.shape, lambda i: (0, 0, 0))],
        out_specs=BS((tr, GM_WIDTH), lambda i: (i, 0)),
        compiler_params=_params("parallel"), name="chunk_gmlp")(
            z, z, norm_g.reshape(1, GM_WIDTH), ws.astype(BF16), bs_b)


def _na_bias_tables(rpb, grid_rows):
    nq = grid_rows // NA_QROWS
    depth, n_heads = rpb.shape[:2]
    n_dr = 2 * NA_ROWS - 1
    i = np.arange(NA_QROWS)[:, None]
    m = np.arange(NA_KROWS)[None, :]
    c = np.arange(GRID_W)[:, None]
    w = np.arange(GRID_W)[None, :]
    c_start = np.clip(c - NA_COLS // 2, 0, GRID_W - NA_COLS)
    col_ok = (w >= c_start) & (w < c_start + NA_COLS)
    dc = np.clip(w - c + NA_COLS - 1, 0, 2 * NA_COLS - 2)
    col_sel = (dc.reshape(-1)[None, :] == np.arange(2 * NA_COLS - 1)[:, None]).astype(np.float32)
    band = jnp.einsum("lhab,bq->lhaq", rpb.astype(F32), col_sel, precision=lax.Precision.HIGHEST)
    band = jnp.where(col_ok[None, None, None], band.reshape(depth, n_heads, n_dr, GRID_W, GRID_W), NEG_INF)
    band = jnp.concatenate([band, jnp.full((depth, n_heads, 1, GRID_W, GRID_W), NEG_INF, F32)], axis=2)
    slots, offsets = [], []
    for j in (0, 1, nq - 1):
        base = int(np.clip(NA_QROWS * j - NA_ROWS // 2, 0, grid_rows - NA_KROWS))
        r = NA_QROWS * j + i
        kr = base + m
        r_start = np.clip(r - NA_ROWS // 2, 0, grid_rows - NA_ROWS)
        row_ok = (kr >= r_start) & (kr < r_start + NA_ROWS)
        slot = np.where(row_ok, kr - r + NA_ROWS - 1, n_dr)
        for s in range(NA_QROWS // NA_SUB_QROWS):
            sub = slice(s * NA_SUB_QROWS, (s + 1) * NA_SUB_QROWS)
            off = min(int(np.argmax(row_ok[sub].any(axis=0))), NA_KROWS - NA_SUB_KROWS)
            assert not row_ok[sub, :off].any() and not row_ok[sub, off + NA_SUB_KROWS:].any()
            slots.append(slot[sub, off:off + NA_SUB_KROWS])
            offsets.append(off)
    n_sub = NA_QROWS // NA_SUB_QROWS
    slots = jnp.asarray(np.stack(slots).reshape(-1), jnp.int32)
    out_block = (n_sub, NA_SUB_QROWS * GRID_W, NA_SUB_KROWS * GRID_W)
    grid_spec = pltpu.PrefetchScalarGridSpec(
        num_scalar_prefetch=1, grid=(depth, 3, n_heads),
        in_specs=[BS((None, None, n_dr + 1, GRID_W, GRID_W), lambda l, k, h, s: (l, h, 0, 0, 0))],
        out_specs=BS((None, None, None) + out_block, lambda l, k, h, s: (l, k, h, 0, 0, 0)))
    tables = pl.pallas_call(
        _na_bias_kernel, grid_spec=grid_spec, out_shape=SDS((depth, 3, n_heads) + out_block, F32),
        compiler_params=_params("parallel", "parallel", "parallel"), name="na_bias_tables")(slots, band)
    return tables, np.asarray(offsets).reshape(3, n_sub)


def _na_bias_kernel(slot_ref, band_ref, o_ref):
    kind = pl.program_id(1)
    for s in range(NA_QROWS // NA_SUB_QROWS):
        for i in range(NA_SUB_QROWS):
            row = ((kind * (NA_QROWS // NA_SUB_QROWS) + s) * NA_SUB_QROWS + i) * NA_SUB_KROWS
            tiles = [band_ref[slot_ref[row + m]] for m in range(NA_SUB_KROWS)]
            o_ref[s, i * GRID_W:(i + 1) * GRID_W, :] = jnp.concatenate(tiles, axis=1)


def _na_kernel(q_ref, k_ref, v_ref, kc_ref, vc_ref, bias_ref, o_ref, *, grid_rows, offsets):
    j = pl.program_id(2)
    nq = grid_rows // NA_QROWS
    kind = jnp.where(j == 0, 0, jnp.where(j == nq - 1, 2, 1))
    base_row = jnp.clip(NA_QROWS * j - NA_ROWS // 2, 0, grid_rows - NA_KROWS)
    nk = NA_SUB_KROWS * GRID_W
    nsq = NA_SUB_QROWS * GRID_W
    for s in range(NA_QROWS // NA_SUB_QROWS):
        off = jnp.where(j == 0, int(offsets[0, s]), jnp.where(j == nq - 1, int(offsets[2, s]), int(offsets[1, s])))
        start = pl.multiple_of((base_row + off) * GRID_W, (NA_ROWS // 2) * GRID_W)
        rows = slice(s * nsq, (s + 1) * nsq)
        q = (q_ref[rows, :].astype(F32) * (HEAD_DIM ** -0.5)).astype(BF16)
        s_loc = _dot_nt(q, k_ref[pl.ds(start, nk), :]) + bias_ref[kind, s]
        s_ctx = _dot_nt(q, kc_ref[...])
        mx = jnp.maximum(jnp.max(s_loc, axis=-1, keepdims=True), jnp.max(s_ctx, axis=-1, keepdims=True))
        e_loc = jnp.exp(s_loc - mx)
        e_ctx = jnp.exp(s_ctx - mx)
        den = jnp.sum(e_loc, axis=-1, keepdims=True) + jnp.sum(e_ctx, axis=-1, keepdims=True)
        o = _dot(e_loc.astype(BF16), v_ref[pl.ds(start, nk), :]) + _dot(e_ctx.astype(BF16), vc_ref[...])
        o_ref[rows, :] = (o / den).astype(o_ref.dtype)


def neighbourhood_attention(z, cols, bias_tables, layer, batch, seq, n_ctx):
    bias, offsets = bias_tables
    assert np.all(offsets % (NA_ROWS // 2) == 0)
    grid_rows = seq // GRID_W
    assert grid_rows % NA_QROWS == 0 and grid_rows >= 2 * NA_KROWS
    nq = grid_rows // NA_QROWS
    qb = NA_QROWS * GRID_W
    qc, kc, vc = (cols[n] // HEAD_DIM for n in ("na_q", "na_k", "na_v"))
    ctx_blk0 = batch * seq // n_ctx
    return pl.pallas_call(
        functools.partial(_na_kernel, grid_rows=grid_rows, offsets=offsets),
        out_shape=SDS((batch * seq, NA_WIDTH), BF16), grid=(N_MIX_HEADS, batch, nq),
        in_specs=[BS((qb, HEAD_DIM), lambda h, b, j: (b * nq + j, qc + h)),
                  BS((seq, HEAD_DIM), lambda h, b, j: (b, kc + h)),
                  BS((seq, HEAD_DIM), lambda h, b, j: (b, vc + h)),
                  BS((n_ctx, HEAD_DIM), lambda h, b, j: (ctx_blk0 + b, kc + h)),
                  BS((n_ctx, HEAD_DIM), lambda h, b, j: (ctx_blk0 + b, vc + h)),
                  BS((None, 3, None) + bias.shape[3:], lambda h, b, j: (layer, 0, h, 0, 0, 0))],
        out_specs=BS((qb, HEAD_DIM), lambda h, b, j: (b * nq + j, h)),
        compiler_params=_params("parallel", "parallel", "parallel"), name="neighbourhood_attention")(
            z, z, z, z, z, bias)


def _ctx_attn_kernel(q_ref, k_ref, v_ref, o_ref):
    q = (q_ref[...].astype(F32) * (HEAD_DIM ** -0.5)).astype(BF16)
    s = _dot_nt(q, k_ref[...])
    e = jnp.exp(s - jnp.max(s, axis=-1, keepdims=True))
    o = _dot(e.astype(BF16), v_ref[...]) / jnp.sum(e, axis=-1, keepdims=True)
    o_ref[...] = o.astype(o_ref.dtype)


def context_attention(z, cols, batch, seq, n_ctx):
    qc, kc, vc = (cols[n] // HEAD_DIM for n in ("na_q", "na_k", "na_v"))
    blk0 = batch * seq // n_ctx
    spec = lambda col: BS((n_ctx, HEAD_DIM), lambda b, h: (blk0 + b, col + h))
    return pl.pallas_call(
        _ctx_attn_kernel, out_shape=SDS((batch * n_ctx, NA_WIDTH), BF16), grid=(batch, N_MIX_HEADS),
        in_specs=[spec(qc), spec(kc), spec(vc)],
        out_specs=BS((n_ctx, HEAD_DIM), lambda b, h: (b, h)),
        compiler_params=_params("parallel", "parallel"), name="context_attention")(z, z, z)


def _rope(t_ref, cos_ref, sin_ref):
    t = t_ref[...].astype(F32)
    lane = lax.broadcasted_iota(jnp.int32, t.shape, 1)
    quarter = HEAD_DIM // 4
    partner = jnp.where((lane % (2 * quarter)) < quarter,
                        pltpu.roll(t, HEAD_DIM - quarter, 1), pltpu.roll(t, quarter, 1))
    return t * cos_ref[...] + partner * sin_ref[...]


def _rope_tables(n, identity):
    if identity:
        return jnp.ones((n, HEAD_DIM), F32), jnp.zeros((n, HEAD_DIM), F32)
    nf = HEAD_DIM // 4
    pos = jnp.arange(n)
    p_row = (pos // GRID_W).astype(F32)
    p_col = (pos % GRID_W).astype(F32)
    inv = ROPE_BASE ** (-jnp.arange(nf, dtype=F32) / nf)
    a_row = p_row[:, None] * inv[None, :]
    a_col = p_col[:, None] * inv[None, :]
    cos = jnp.concatenate([jnp.cos(a_row), jnp.cos(a_row), jnp.cos(a_col), jnp.cos(a_col)], axis=-1)
    sin = jnp.concatenate([-jnp.sin(a_row), jnp.sin(a_row), -jnp.sin(a_col), jnp.sin(a_col)], axis=-1)
    return cos, sin


def _decay_tables(dec_f, dec_b):
    log_gf = jnp.log1p(-jnp.exp2(dec_f.astype(F32)))
    log_gb = jnp.log1p(-jnp.exp2(dec_b.astype(F32)))
    idx = jnp.arange(CHUNK, dtype=F32)
    rel = idx[:, None] - idx[None, :]
    d_f = jnp.where(rel >= 0, jnp.exp(log_gf[:, None, None] * jnp.maximum(rel, 0.0)), 0.0)
    d_b = jnp.where(rel < 0, jnp.exp(log_gb[:, None, None] * jnp.maximum(-rel, 0.0)), 0.0)
    bcast = lambda v: jnp.broadcast_to(v[:, :, None], v.shape + (RET_DK,))
    return dict(
        log_gf=log_gf, log_gb=log_gb,
        dfb=d_f + d_b,
        qdf=bcast(jnp.exp(log_gf[:, None] * (idx + 1.0))),
        qdb=bcast(jnp.exp(log_gb[:, None] * (CHUNK - idx))),
        kdf=bcast(jnp.exp(log_gf[:, None] * (CHUNK - 1.0 - idx))),
        kdb=bcast(jnp.exp(log_gb[:, None] * idx)),
        cdf=jnp.broadcast_to(jnp.exp(log_gf * CHUNK)[:, None, None], (log_gf.shape[0], 1, RET_DV)),
        cdb=jnp.broadcast_to(jnp.exp(log_gb * CHUNK)[:, None, None], (log_gb.shape[0], 1, RET_DV)),
    )


def _ctx_state_kernel(k_ref, v_ref, wf_ref, wb_ref, sf_ref, sb_ref):
    k = k_ref[...].astype(F32) * (RET_DK ** -0.5)
    v = v_ref[...]
    sf_ref[...] = _dot((k * wf_ref[...]).T.astype(BF16), v)
    sb_ref[...] = _dot((k * wb_ref[...]).T.astype(BF16), v)


def context_state(z, cols, tabs, batch, seq, n_ctx):
    pos = jnp.arange(n_ctx, dtype=F32)
    bcast = lambda v: jnp.broadcast_to(v[:, :, None], v.shape + (RET_DK,))
    w_f = bcast(jnp.exp(tabs["log_gf"][:, None] * ((n_ctx - 1.0) - pos)))
    w_b = bcast(jnp.exp(tabs["log_gb"][:, None] * pos))
    kc, vc = cols["ret_k"] // RET_DK, cols["ret_v"] // RET_DV
    blk0 = batch * seq // n_ctx
    out = SDS((batch, N_MIX_HEADS, RET_DK, RET_DV), F32)
    tab_spec = BS((None, n_ctx, RET_DK), lambda b, h: (h, 0, 0))
    out_spec = BS((None, None, RET_DK, RET_DV), lambda b, h: (b, h, 0, 0))
    return pl.pallas_call(
        _ctx_state_kernel, out_shape=(out, out), grid=(batch, N_MIX_HEADS),
        in_specs=[BS((n_ctx, RET_DK), lambda b, h: (blk0 + b, kc + h)),
                  BS((n_ctx, RET_DV), lambda b, h: (blk0 + b, vc + h)), tab_spec, tab_spec],
        out_specs=(out_spec, out_spec),
        compiler_params=_params("parallel", "parallel"), name="context_state")(z, z, w_f, w_b)


def _ret_state_kernel(kf_ref, vf_ref, cosf_ref, sinf_ref, kb_ref, vb_ref, cosb_ref, sinb_ref,
                      kdf_ref, kdb_ref, cdf_ref, cdb_ref, s0f_ref, s0b_ref, sf_ref, sb_ref, st_ref, *, cpg):
    seq_id = pl.program_id(1) * pl.num_programs(2) + pl.program_id(2)

    @pl.when(pl.program_id(0) == 0)
    def _():
        st_ref[2 * seq_id] = s0f_ref[...]
        st_ref[2 * seq_id + 1] = s0b_ref[...]

    def scan(k_ref, v_ref, cos_ref, sin_ref, kd_ref, cd_ref, out_ref, direction, order):
        kr = _rope(k_ref, cos_ref, sin_ref) * (RET_DK ** -0.5)
        slot = 2 * seq_id + direction
        s = st_ref[slot]
        for c in order:
            rows = slice(c * CHUNK, (c + 1) * CHUNK)
            out_ref[c] = s.astype(out_ref.dtype)
            kc = (kr[rows, :] * kd_ref[...]).T.astype(BF16)
            s = s * cd_ref[...] + _dot(kc, v_ref[rows, :])
        st_ref[slot] = s

    scan(kf_ref, vf_ref, cosf_ref, sinf_ref, kdf_ref, cdf_ref, sf_ref, 0, range(cpg))
    scan(kb_ref, vb_ref, cosb_ref, sinb_ref, kdb_ref, cdb_ref, sb_ref, 1, reversed(range(cpg)))


def _ret_out_kernel(q_ref, k_ref, v_ref, g_ref, cos_ref, sin_ref, sf_ref, sb_ref, dfb_ref, qdf_ref, qdb_ref,
                    o_ref, *, cpg):
    qr = _rope(q_ref, cos_ref, sin_ref)
    kr = (_rope(k_ref, cos_ref, sin_ref) * (RET_DK ** -0.5)).astype(BF16)
    for c in range(cpg):
        rows = slice(c * CHUNK, (c + 1) * CHUNK)
        qc = qr[rows, :]
        p = (_dot_nt(qc.astype(BF16), kr[rows, :]) * dfb_ref[...]).astype(BF16)
        o = (_dot(p, v_ref[rows, :]) + _dot((qc * qdf_ref[...]).astype(BF16), sf_ref[c])
             + _dot((qc * qdb_ref[...]).astype(BF16), sb_ref[c]))
        mu = jnp.mean(o, axis=-1, keepdims=True)
        var = jnp.mean(jnp.square(o - mu), axis=-1, keepdims=True)
        on = (o - mu) * lax.rsqrt(var + EPS)
        g = g_ref[rows, :].astype(F32)
        o_ref[rows, :] = (on * (g * jax.nn.sigmoid(g))).astype(o_ref.dtype)


def retention(z, cols, tabs, s0f, s0b, row0, n, batch, rope_identity):
    nc = n // CHUNK
    cpg = min(RET_CHUNKS_PER_STEP, nc)
    ng = nc // cpg
    gr = cpg * CHUNK
    blk0 = row0 // gr
    qc, kc = cols["ret_q"] // RET_DK, cols["ret_k"] // RET_DK
    vc, gc = cols["ret_v"] // RET_DV, cols["ret_g"] // RET_DV
    cos, sin = _rope_tables(n, rope_identity)
    BSg = lambda shape, f: BS(shape, lambda g, b, h: f(b, h, g))
    fwd = lambda b, h, g: blk0 + b * ng + g
    bwd = lambda b, h, g: blk0 + b * ng + (ng - 1 - g)
    head_tab = lambda w: BSg((None, CHUNK, w), lambda b, h, g: (h, 0, 0))
    c_tab = BSg((None, 1, RET_DV), lambda b, h, g: (h, 0, 0))
    s0_spec = BSg((None, None, RET_DK, RET_DV), lambda b, h, g: (b, h, 0, 0))
    st_shape = SDS((batch, N_MIX_HEADS, nc, RET_DK, RET_DV), BF16)
    st_spec = lambda rev: BSg((None, None, cpg, RET_DK, RET_DV),
                              lambda b, h, g: (b, h, (ng - 1 - g) if rev else g, 0, 0))
    sf, sb = pl.pallas_call(
        functools.partial(_ret_state_kernel, cpg=cpg), out_shape=(st_shape, st_shape),
        grid=(ng, batch, N_MIX_HEADS),
        in_specs=[BSg((gr, RET_DK), lambda b, h, g: (fwd(b, h, g), kc + h)),
                  BSg((gr, RET_DV), lambda b, h, g: (fwd(b, h, g), vc + h)),
                  BSg((gr, RET_DK), lambda b, h, g: (g, 0)), BSg((gr, RET_DK), lambda b, h, g: (g, 0)),
                  BSg((gr, RET_DK), lambda b, h, g: (bwd(b, h, g), kc + h)),
                  BSg((gr, RET_DV), lambda b, h, g: (bwd(b, h, g), vc + h)),
                  BSg((gr, RET_DK), lambda b, h, g: (ng - 1 - g, 0)),
                  BSg((gr, RET_DK), lambda b, h, g: (ng - 1 - g, 0)),
                  head_tab(RET_DK), head_tab(RET_DK), c_tab, c_tab, s0_spec, s0_spec],
        out_specs=(st_spec(False), st_spec(True)),
        scratch_shapes=[pltpu.VMEM((2 * batch * N_MIX_HEADS, RET_DK, RET_DV), F32)],
        compiler_params=_params("arbitrary", "arbitrary", "arbitrary"), name="retention_state")(
            z, z, cos, sin, z, z, cos, sin, tabs["kdf"], tabs["kdb"], tabs["cdf"], tabs["cdb"], s0f, s0b)
    return pl.pallas_call(
        functools.partial(_ret_out_kernel, cpg=cpg), out_shape=SDS((batch * n, RET_V_WIDTH), BF16),
        grid=(ng, batch, N_MIX_HEADS),
        in_specs=[BSg((gr, RET_DK), lambda b, h, g: (fwd(b, h, g), qc + h)),
                  BSg((gr, RET_DK), lambda b, h, g: (fwd(b, h, g), kc + h)),
                  BSg((gr, RET_DV), lambda b, h, g: (fwd(b, h, g), vc + h)),
                  BSg((gr, RET_DV), lambda b, h, g: (fwd(b, h, g), gc + h)),
                  BSg((gr, RET_DK), lambda b, h, g: (g, 0)), BSg((gr, RET_DK), lambda b, h, g: (g, 0)),
                  st_spec(False), st_spec(False),
                  head_tab(CHUNK), head_tab(RET_DK), head_tab(RET_DK)],
        out_specs=BSg((gr, RET_DV), lambda b, h, g: (b * ng + g, h)),
        compiler_params=_params("parallel", "parallel", "parallel"), name="retention_out")(
            z, z, z, z, cos, sin, sf, sb, tabs["dfb"], tabs["qdf"], tabs["qdb"])


def _dispatch_tables(routed, counts):
    t = routed.shape[0]
    n_assign = t * MOE_TOPK
    flat_e = routed[:, 0:MOE_TOPK].astype(jnp.int32).reshape(-1)
    rank = routed[:, MOE_TOPK:2 * MOE_TOPK].astype(jnp.int32).reshape(-1)
    flat_t = jnp.repeat(jnp.arange(t, dtype=jnp.int32), MOE_TOPK)
    counts = counts[0, :MOE_EXPERTS].astype(jnp.int32)
    padded = (counts + MOE_ROWS - 1) // MOE_ROWS * MOE_ROWS
    p_end = jnp.cumsum(padded)
    dest = (p_end - padded)[flat_e] + rank
    n_blocks = (n_assign + MOE_EXPERTS * (MOE_ROWS - 1)) // MOE_ROWS
    row_tok = jnp.zeros((n_blocks * MOE_ROWS,), jnp.int32).at[dest].set(
        flat_t, unique_indices=True, mode="promise_in_bounds")
    blk_start = jnp.arange(n_blocks, dtype=jnp.int32) * MOE_ROWS
    block_e = jnp.minimum(jnp.sum((p_end[None, :] <= blk_start[:, None]).astype(jnp.int32), axis=1),
                          MOE_EXPERTS - 1)
    n_used = (p_end[-1] // MOE_ROWS).astype(jnp.int32).reshape(1)
    return block_e, n_used, row_tok.reshape(n_blocks, 1, MOE_ROWS), dest.reshape(t, MOE_TOPK).astype(jnp.int32)


def _start_row_gather(src_hbm, idx_ref, idx_of_row, dst, sem, n):
    def issue(r, carry):
        pltpu.make_async_copy(src_hbm.at[pl.ds(idx_ref[0, idx_of_row(r)], 1)], dst.at[pl.ds(r, 1)], sem).start()
        return carry
    lax.fori_loop(0, n, issue, 0, unroll=8)


def _wait_row_gather(src_hbm, dst, sem):
    pltpu.make_async_copy(src_hbm.at[pl.ds(0, dst.shape[0])], dst, sem).wait()


MOE_GATHER_AHEAD = 3


def _moe_kernel(be_ref, nu_ref, *refs):
    ahead = MOE_GATHER_AHEAD
    tok_refs = refs[:ahead + 1]
    h_hbm, w1_ref, w3_ref, w2_ref, y_ref, xb, sem = refs[ahead + 1:]
    i = pl.program_id(0)
    n_used = nu_ref[0]
    n_slots = ahead + 1
    slot = i % n_slots
    row = lambda r: r

    for a in range(ahead):
        @pl.when((i == 0) & (a < n_used))
        def _(a=a):
            _start_row_gather(h_hbm, tok_refs[a], row, xb.at[a], sem.at[a], MOE_ROWS)

    @pl.when(i + ahead < n_used)
    def _():
        nxt = (i + ahead) % n_slots
        _start_row_gather(h_hbm, tok_refs[ahead], row, xb.at[nxt], sem.at[nxt], MOE_ROWS)

    @pl.when(i < n_used)
    def _():
        _wait_row_gather(h_hbm, xb.at[slot], sem.at[slot])
        half = xb.shape[2]
        x_lo, x_hi = (v.astype(BF16) for v in _unpack_halves(xb[slot]))
        up = lambda w_ref: _dot(x_lo, w_ref[0:half, :]) + _dot(x_hi, w_ref[half:2 * half, :])
        h1 = up(w1_ref)
        h3 = up(w3_ref)
        hm = (h1 * jax.nn.sigmoid(h1) * h3).astype(BF16)
        y_ref[...] = _pack_halves(_dot(hm, w2_ref[...]))

    @pl.when(i >= n_used)
    def _():
        y_ref[...] = jnp.zeros_like(y_ref)


def moe_experts(h2, tables, w1, w3, w2):
    block_e, n_used, row_tok, _ = tables
    n_blocks = row_tok.shape[0]
    d, hid = w1.shape[1], w1.shape[2]
    assert h2.shape[1] == d // 2 and h2.dtype == jnp.uint32
    grid_spec = pltpu.PrefetchScalarGridSpec(
        num_scalar_prefetch=2, grid=(n_blocks,),
        in_specs=[BS((None, 1, MOE_ROWS), lambda i, be, nu, a=a: (jnp.minimum(i + a, n_blocks - 1), 0, 0),
                     memory_space=pltpu.SMEM) for a in range(MOE_GATHER_AHEAD + 1)] + [
                  BS(memory_space=pl.ANY),
                  BS((None, d, hid), lambda i, be, nu: (be[i], 0, 0)),
                  BS((None, d, hid), lambda i, be, nu: (be[i], 0, 0)),
                  BS((None, hid, d), lambda i, be, nu: (be[i], 0, 0))],
        out_specs=BS((MOE_ROWS, d // 2), lambda i, be, nu: (i, 0)),
        scratch_shapes=[pltpu.VMEM((MOE_GATHER_AHEAD + 1, MOE_ROWS, d // 2), jnp.uint32),
                        pltpu.SemaphoreType.DMA((MOE_GATHER_AHEAD + 1,))])
    return pl.pallas_call(
        _moe_kernel, out_shape=SDS((n_blocks * MOE_ROWS, d // 2), jnp.uint32), grid_spec=grid_spec,
        compiler_params=_params("arbitrary"), name="moe_experts")(
            block_e, n_used, *([row_tok] * (MOE_GATHER_AHEAD + 1)), h2, w1, w3, w2)


def _combine_kernel(pos_ref, pos_next_ref, y_hbm, x_ref, w_ref, g_ref, ng_ref, nsh_ref, nsc_ref, *refs,
                    keep_stream):
    outs, (yb, sem) = refs[:-2], refs[-2:]
    i = pl.program_id(0)
    tt = COMBINE_ROWS
    slot = i % 2
    idx = lambda r: r

    @pl.when(i == 0)
    def _():
        _start_row_gather(y_hbm, pos_ref, idx, yb.at[0], sem.at[0], MOE_TOPK * tt)

    @pl.when(i + 1 < pl.num_programs(0))
    def _():
        _start_row_gather(y_hbm, pos_next_ref, idx, yb.at[1 - slot], sem.at[1 - slot], MOE_TOPK * tt)

    _wait_row_gather(y_hbm, yb.at[slot], sem.at[slot])
    w = w_ref[...]
    w0, w1 = (w[:, 2 * MOE_TOPK + k:2 * MOE_TOPK + k + 1] for k in range(MOE_TOPK))
    half = yb.shape[2]
    halves = (slice(0, half), slice(half, 2 * half))
    ya = _unpack_halves(yb[slot, 0:tt, :])
    yb_ = _unpack_halves(yb[slot, tt:2 * tt, :])
    xn = [x_ref[:, s] + g_ref[:, s] * (a * w0 + b * w1) for s, a, b in zip(halves, ya, yb_)]
    if keep_stream:
        for s, v in zip(halves, xn):
            outs[0][:, s] = v
    h_ref = outs[-1]
    ms = sum(jnp.sum(v * v, axis=-1, keepdims=True) for v in xn) / (2 * half)
    inv = lax.rsqrt(ms + EPS)
    for s, v in zip(halves, xn):
        h_ref[:, s] = ((v * inv * ng_ref[:, s]) * (1.0 + nsc_ref[:, s]) + nsh_ref[:, s]).astype(h_ref.dtype)


def moe_combine(x1, y, pos, routed, gate, rows, seg_rows, norm_g, norm_shift, norm_scale, out_dtype, keep_stream):
    d = x1.shape[1]
    tt = COMBINE_ROWS
    nt = rows // tt
    seg = _seg_map(seg_rows, tt, gate.shape[0])
    nseg = _seg_map(seg_rows, tt, norm_shift.shape[0])
    pos3 = pos.reshape(nt, tt, MOE_TOPK).transpose(0, 2, 1).reshape(nt, 1, MOE_TOPK * tt)
    row_spec = BS((tt, d), lambda i: (i, 0))
    out_shape = ((SDS((rows, d), F32),) if keep_stream else ()) + (SDS((rows, d), out_dtype),)
    res = pl.pallas_call(
        functools.partial(_combine_kernel, keep_stream=keep_stream), out_shape=out_shape, grid=(nt,),
        in_specs=[BS((None, 1, MOE_TOPK * tt), lambda i: (i, 0, 0), memory_space=pltpu.SMEM),
                  BS((None, 1, MOE_TOPK * tt), lambda i: (jnp.minimum(i + 1, nt - 1), 0, 0),
                     memory_space=pltpu.SMEM),
                  BS(memory_space=pl.ANY), row_spec,
                  BS((tt, ROUTER_LANES), lambda i: (i, 0)), BS((None, 1, d), lambda i: (seg(i), 0, 0)),
                  BS((1, d), lambda i: (0, 0)), BS((None, 1, d), lambda i: (nseg(i), 0, 0)),
                  BS((None, 1, d), lambda i: (nseg(i), 0, 0))],
        out_specs=tuple(row_spec for _ in out_shape),
        scratch_shapes=[pltpu.VMEM((2, MOE_TOPK * tt, d // 2), jnp.uint32), pltpu.SemaphoreType.DMA((2,))],
        compiler_params=_params("arbitrary"), name="moe_combine")(
            pos3, pos3, y, x1, routed, gate, norm_g, norm_shift, norm_scale)
    return res if keep_stream else res[0]


def kernel(x, c, ctx, c_ctx, ada_w, ada_b, norm1_g, w_in, gm_norm_g, gm_ws, gm_bs, na_rpb, ret_decay_fwd,
           ret_decay_bwd, w_branch_a, w_branch_b, w_branch_c, w_out, norm2_g, moe_w_group, moe_w_expert,
           moe_w1, moe_w3, moe_w2, final_norm_g):
    batch, seq, d = x.shape
    n_ctx = ctx.shape[1]
    depth = ada_w.shape[0]
    t_lat, t_ctx = batch * seq, batch * n_ctx
    t_all = t_lat + t_ctx
    cols, in_total = _col_offsets(d)
    assert w_in.shape[2] == in_total and seq % ROW_TILE == 0 and t_ctx % ROW_TILE == 0
    assert 8 >= batch + 1

    cond = jnp.concatenate([c, c_ctx[None, :], jnp.zeros((8 - batch - 1, d), c.dtype)], axis=0)
    mod = ada_modulation(jax.nn.silu(cond).astype(BF16), ada_w, ada_b)[:, :batch + 1]
    mod = mod.reshape(depth, batch + 1, 6, 1, d)

    na_bias = _na_bias_tables(na_rpb, seq // GRID_W)
    stream = (x.reshape(t_lat, d), ctx.reshape(t_ctx, d))
    h = norm_mod(stream, norm1_g[0].reshape(1, d), mod[0, :, 0], mod[0, :, 1], seq, BF16)
    for l in range(depth):
        need_ctx = l < depth - 1
        rows = t_all if need_ctx else t_lat
        sh1, sc1, g1, sh2, sc2, g2 = (mod[l, :, k] for k in range(6))

        ctx_cols = [(0, in_total)] if need_ctx else [
            (cols[name], cols[name] + width) for name, width in
            (("na_k", NA_WIDTH), ("na_v", NA_WIDTH), ("ret_k", RET_QK_WIDTH), ("ret_v", RET_V_WIDTH))]
        z, (w_a, w_b, w_c, w_o, w1, w3, w2) = in_proj(
            h, w_in, l, t_all, BF16, [w_branch_a, w_branch_b, w_branch_c, w_out, moe_w1, moe_w3, moe_w2],
            t_lat, ctx_cols)

        a = chunk_gmlp(z, cols, gm_norm_g[l], gm_ws[l], gm_bs[l], rows)
        bb = neighbourhood_attention(z, cols, na_bias, l, batch, seq, n_ctx)
        tabs = _decay_tables(ret_decay_fwd[l], ret_decay_bwd[l])
        s0f, s0b = context_state(z, cols, tabs, batch, seq, n_ctx)
        r = retention(z, cols, tabs, s0f, s0b, 0, seq, batch, False)
        if need_ctx:
            zeros = jnp.zeros_like(s0f)
            bb = (bb, context_attention(z, cols, batch, seq, n_ctx))
            r = (r, retention(z, cols, tabs, zeros, zeros, t_lat, n_ctx, batch, True))
        else:
            bb, r = (bb, bb), (r, r)

        y = merge_branches(z, a, bb, r, w_a, w_b, w_c, rows)
        x1 = matmul_residual(y, w_o, stream, g1, rows, seq)

        w_router = jnp.concatenate(
            [moe_w_group[l], moe_w_expert[l],
             jnp.zeros((d, ROUTER_LANES - MOE_GROUPS - MOE_EXPERTS), F32)], axis=1).astype(BF16)
        h2, routed, counts = norm_router(x1, norm2_g[l].reshape(1, d), sh2, sc2, w_router, rows, seq)
        tables = _dispatch_tables(routed, counts)
        y_moe = moe_experts(h2, tables, w1, w3, w2)
        if need_ctx:
            xs, h = moe_combine(x1, y_moe, tables[3], routed, g2, rows, seq, norm1_g[l + 1].reshape(1, d),
                                mod[l + 1, :, 0], mod[l + 1, :, 1], BF16, True)
            stream = (xs, xs)
        else:
            zero = jnp.zeros((1, 1, d), F32)
            out = moe_combine(x1, y_moe, tables[3], routed, g2, rows, seq, final_norm_g.reshape(1, d),
                              zero, zero, F32, False)
    return out.reshape(batch, seq, d)
```

```python
import functools

import jax
import jax.numpy as jnp
import numpy as np
from jax import lax
from jax.experimental import pallas as pl
from jax.experimental.pallas import tpu as pltpu

F32 = jnp.float32
BF16 = jnp.bfloat16
SDS = jax.ShapeDtypeStruct
BS = pl.BlockSpec

EPS = 1e-6
NEG_INF = -1e30
HEAD_DIM = 128
GRID_W = 64
NA_ROWS = 8
NA_COLS = 16
NA_QROWS = 16
NA_KROWS = NA_QROWS + NA_ROWS
NA_SUB_QROWS = 4
NA_SUB_KROWS = NA_SUB_QROWS + NA_ROWS
RET_DK = 128
RET_DV = 256
CHUNK = 128
RET_CHUNKS_PER_STEP = 16
ROPE_BASE = 10000.0
MOE_GROUPS = 4
MOE_EPG = 8
MOE_EXPERTS = MOE_GROUPS * MOE_EPG
MOE_TOPK = 2
ROUTER_LANES = 128
N_MIX_HEADS = 8
GM_WIDTH = N_MIX_HEADS * HEAD_DIM
NA_WIDTH = N_MIX_HEADS * HEAD_DIM
RET_QK_WIDTH = N_MIX_HEADS * RET_DK
RET_V_WIDTH = N_MIX_HEADS * RET_DV

V7X_VMEM_BYTES = 64 * 1024 * 1024
VMEM_LIMIT = V7X_VMEM_BYTES - 8 * 1024 * 1024

ROW_TILE = 512
COL_TILE = 1024
MOE_ROWS = 256
COMBINE_ROWS = 128


def _params(*sem):
    return pltpu.CompilerParams(dimension_semantics=sem, vmem_limit_bytes=VMEM_LIMIT)


def _col_offsets(d):
    off, out = 0, {}
    for name, width in (("gate_a", d), ("gate_b", d), ("gate_c", d), ("gm_u", GM_WIDTH), ("gm_v", GM_WIDTH),
                        ("na_q", NA_WIDTH), ("na_k", NA_WIDTH), ("na_v", NA_WIDTH),
                        ("ret_q", RET_QK_WIDTH), ("ret_k", RET_QK_WIDTH), ("ret_v", RET_V_WIDTH),
                        ("ret_g", RET_V_WIDTH)):
        out[name] = off
        off += width
    return out, off


def _dot(a, b):
    return jnp.dot(a, b, preferred_element_type=F32)


def _dot_nt(a, b):
    return lax.dot_general(a, b, (((1,), (1,)), ((), ())), preferred_element_type=F32)


_HI16 = 0xFFFF0000


def _pack_halves(v):
    n = v.shape[1] // 2
    bits = lax.bitcast_convert_type(v.astype(BF16).astype(F32), jnp.uint32)
    return lax.shift_right_logical(bits[:, :n], jnp.uint32(16)) | (bits[:, n:] & jnp.uint32(_HI16))


def _unpack_halves(u):
    lo = lax.bitcast_convert_type(lax.shift_left(u, jnp.uint32(16)), F32)
    hi = lax.bitcast_convert_type(u & jnp.uint32(_HI16), F32)
    return lo, hi


def _norm_mod(x_ref, g_ref, sh_ref, sc_ref):
    x = x_ref[...]
    ms = jnp.mean(x * x, axis=-1, keepdims=True)
    y = x * lax.rsqrt(ms + EPS) * g_ref[...]
    return y * (1.0 + sc_ref[...]) + sh_ref[...]


def _norm_kernel(x_ref, xt_ref, g_ref, sh_ref, sc_ref, o_ref, *, n_main):
    @pl.when(pl.program_id(0) < n_main)
    def _():
        o_ref[...] = _norm_mod(x_ref, g_ref, sh_ref, sc_ref).astype(o_ref.dtype)

    @pl.when(pl.program_id(0) >= n_main)
    def _():
        o_ref[...] = _norm_mod(xt_ref, g_ref, sh_ref, sc_ref).astype(o_ref.dtype)


def _first_lane(mask, lane):
    return jnp.min(jnp.where(mask, lane, float(ROUTER_LANES)), axis=-1, keepdims=True)


def _route(logits, run_ref):
    n = logits.shape[0]
    lane = lax.broadcasted_iota(jnp.int32, logits.shape, 1).astype(F32)
    neg = -jnp.inf
    is_g = lane < MOE_GROUPS
    gl = jnp.where(is_g, logits, neg)
    eg = jnp.exp(gl - jnp.max(gl, axis=-1, keepdims=True))
    pg = eg / jnp.sum(eg, axis=-1, keepdims=True)
    p_sel = jnp.max(pg, axis=-1, keepdims=True)
    grp = _first_lane((pg == p_sel) & is_g, lane)
    lo = MOE_GROUPS + MOE_EPG * grp
    is_e = (lane >= lo) & (lane < lo + MOE_EPG)
    el = jnp.where(is_e, logits, neg)
    ee = jnp.exp(el - jnp.max(el, axis=-1, keepdims=True))
    pe = ee / jnp.sum(ee, axis=-1, keepdims=True)
    t1 = jnp.max(pe, axis=-1, keepdims=True)
    i1 = _first_lane((pe == t1) & is_e, lane)
    rest = jnp.where(is_e & (lane != i1), pe, -1.0)
    t2 = jnp.max(rest, axis=-1, keepdims=True)
    i2 = _first_lane(rest == t2, lane)
    tsum = t1 + t2
    w1, w2 = p_sel * t1 / tsum, p_sel * t2 / tsum
    e1, e2 = i1 - MOE_GROUPS, i2 - MOE_GROUPS
    oh1, oh2 = lane == e1, lane == e2
    cnt = jnp.where(oh1 | oh2, 1.0, 0.0)
    r_i = lax.broadcasted_iota(jnp.int32, (n, n), 0)
    c_i = lax.broadcasted_iota(jnp.int32, (n, n), 1)
    tri = jnp.where(c_i < r_i, 1.0, 0.0).astype(BF16)
    before = _dot(tri, cnt.astype(BF16)) + run_ref[...]
    r1 = jnp.sum(jnp.where(oh1, before, 0.0), axis=-1, keepdims=True)
    r2 = jnp.sum(jnp.where(oh2, before, 0.0), axis=-1, keepdims=True)
    run_ref[...] = run_ref[...] + jnp.sum(cnt, axis=0, keepdims=True)
    slab = jnp.zeros(logits.shape, F32)
    for k, v in enumerate((e1, e2, r1, r2, w1, w2)):
        slab = jnp.where(lane == float(k), v, slab)
    return slab


def _norm_router_kernel(x_ref, g_ref, sh_ref, sc_ref, wr_ref, h_ref, rt_ref, cnt_ref, run_ref):
    @pl.when(pl.program_id(0) == 0)
    def _():
        run_ref[...] = jnp.zeros_like(run_ref)
    h = _norm_mod(x_ref, g_ref, sh_ref, sc_ref)
    h_ref[...] = _pack_halves(h)
    rt_ref[...] = _route(_dot(h.astype(BF16), wr_ref[...]), run_ref)
    cnt_ref[...] = run_ref[...]


def _seg_map(seg_rows, tile, nseg):
    per = seg_rows // tile
    return lambda i: jnp.minimum(i // per, nseg - 1)


def norm_mod(stream, g, shift, scale, seg_rows, out_dtype):
    main, tail = stream
    d = main.shape[1]
    tr = ROW_TILE // 2
    seg = _seg_map(seg_rows, tr, shift.shape[0])
    n_main = main.shape[0] // tr
    rows = main.shape[0] + tail.shape[0]
    return pl.pallas_call(
        functools.partial(_norm_kernel, n_main=n_main), out_shape=SDS((rows, d), out_dtype), grid=(rows // tr,),
        in_specs=[BS((tr, d), lambda i: (jnp.minimum(i, n_main - 1), 0)),
                  BS((tr, d), lambda i: (jnp.maximum(i - n_main, 0), 0)), BS((1, d), lambda i: (0, 0)),
                  BS((None, 1, d), lambda i: (seg(i), 0, 0)), BS((None, 1, d), lambda i: (seg(i), 0, 0))],
        out_specs=BS((tr, d), lambda i: (i, 0)),
        compiler_params=_params("parallel"), name="norm_mod")(main, tail, g, shift, scale)


def norm_router(x, g, shift, scale, w_router, rows, seg_rows):
    d = x.shape[1]
    tr = ROW_TILE // 2
    seg = _seg_map(seg_rows, tr, shift.shape[0])
    return pl.pallas_call(
        _norm_router_kernel,
        out_shape=(SDS((rows, d // 2), jnp.uint32), SDS((rows, ROUTER_LANES), F32),
                   SDS((1, ROUTER_LANES), F32)),
        grid=(rows // tr,),
        in_specs=[BS((tr, d), lambda i: (i, 0)), BS((1, d), lambda i: (0, 0)),
                  BS((None, 1, d), lambda i: (seg(i), 0, 0)), BS((None, 1, d), lambda i: (seg(i), 0, 0)),
                  BS((d, ROUTER_LANES), lambda i: (0, 0))],
        out_specs=(BS((tr, d // 2), lambda i: (i, 0)), BS((tr, ROUTER_LANES), lambda i: (i, 0)),
                   BS((1, ROUTER_LANES), lambda i: (0, 0))),
        scratch_shapes=[pltpu.VMEM((1, ROUTER_LANES), F32)],
        compiler_params=_params("arbitrary"), name="norm_router")(x, g, shift, scale, w_router)


def _ada_kernel(a_ref, w_ref, b_ref, o_ref):
    o_ref[...] = _dot(a_ref[...], w_ref[...].astype(BF16)) + b_ref[...]


def ada_modulation(act, ada_w, ada_b):
    depth, d, n = ada_w.shape
    tn = ROW_TILE
    return pl.pallas_call(
        _ada_kernel, out_shape=SDS((depth, act.shape[0], n), F32), grid=(depth, n // tn),
        in_specs=[BS(act.shape, lambda l, j: (0, 0)), BS((None, d, tn), lambda l, j: (l, 0, j)),
                  BS((None, 1, tn), lambda l, j: (l, 0, j))],
        out_specs=BS((None, act.shape[0], tn), lambda l, j: (l, 0, j)),
        compiler_params=_params("parallel", "parallel"), name="ada_modulation")(
            act, ada_w, ada_b.reshape(depth, 1, n))


def _in_proj_kernel(a_ref, b_ref, *refs, n_side):
    side_in, o_ref, side_out, bq_ref = refs[:n_side], refs[n_side], refs[n_side + 1:-1], refs[-1]
    @pl.when(pl.program_id(1) == 0)
    def _():
        bq_ref[...] = b_ref[...].astype(BF16)
    o_ref[...] = _dot(a_ref[...], bq_ref[...]).astype(o_ref.dtype)
    for s_in, s_out in zip(side_in, side_out):
        s_out[...] = s_in[...].astype(BF16)


BF16_SUBLANES = 16


def in_proj(a, w, layer, rows, out_dtype, side):
    _, k, n = w.shape
    tm, tn = ROW_TILE, COL_TILE
    nj, ni = n // tn, rows // tm
    n_steps = nj * ni
    in_specs, out_specs, out_shapes = [], [], []
    for s in side:
        r, c = s.shape[-2:]
        n_mat = int(np.prod(s.shape[1:-2]))
        rps = BF16_SUBLANES
        while n_mat * (r // rps) > n_steps:
            rps *= 2
        assert r % rps == 0
        bpm = r // rps
        blk = lambda j, i, last=n_mat * bpm - 1: jnp.minimum(j * ni + i, last)
        if s.ndim == 4:
            in_specs.append(BS((None, None, rps, c),
                               lambda j, i, blk=blk, bpm=bpm: (layer, blk(j, i) // bpm, blk(j, i) % bpm, 0)))
            out_specs.append(BS((None, rps, c), lambda j, i, blk=blk, bpm=bpm: (blk(j, i) // bpm, blk(j, i) % bpm, 0)))
        else:
            in_specs.append(BS((None, rps, c), lambda j, i, blk=blk: (layer, blk(j, i), 0)))
            out_specs.append(BS((rps, c), lambda j, i, blk=blk: (blk(j, i), 0)))
        out_shapes.append(SDS(s.shape[1:], BF16))
    res = pl.pallas_call(
        functools.partial(_in_proj_kernel, n_side=len(side)),
        out_shape=[SDS((rows, n), out_dtype)] + out_shapes, grid=(nj, ni),
        in_specs=[BS((tm, k), lambda j, i: (i, 0)),
                  BS((None, k, tn), lambda j, i: (layer, 0, j), pipeline_mode=pl.Buffered(1))] + in_specs,
        out_specs=[BS((tm, tn), lambda j, i: (i, j))] + out_specs,
        scratch_shapes=[pltpu.VMEM((k, tn), BF16)],
        compiler_params=_params("arbitrary", "arbitrary"), name="in_proj")(a, w, *side)
    return res[0], list(res[1:])


def _mm_res_kernel(a_ref, b_ref, x_ref, xt_ref, g_ref, o_ref, *, n_main):
    upd = g_ref[...] * _dot(a_ref[...], b_ref[...])

    @pl.when(pl.program_id(1) < n_main)
    def _():
        o_ref[...] = x_ref[...] + upd

    @pl.when(pl.program_id(1) >= n_main)
    def _():
        o_ref[...] = xt_ref[...] + upd


def matmul_residual(a, b, stream, gate, rows, seg_rows):
    k, n = b.shape
    tm, tn = ROW_TILE, COL_TILE
    seg = _seg_map(seg_rows, tm, gate.shape[0])
    main, tail = stream
    n_main = main.shape[0] // tm
    return pl.pallas_call(
        functools.partial(_mm_res_kernel, n_main=n_main), out_shape=SDS((rows, n), F32),
        grid=(n // tn, rows // tm),
        in_specs=[BS((tm, k), lambda j, i: (i, 0)), BS((k, tn), lambda j, i: (0, j)),
                  BS((tm, tn), lambda j, i: (jnp.minimum(i, n_main - 1), j)),
                  BS((tm, tn), lambda j, i: (jnp.maximum(i - n_main, 0), j)),
                  BS((None, 1, tn), lambda j, i: (seg(i), 0, j))],
        out_specs=BS((tm, tn), lambda j, i: (i, j)),
        compiler_params=_params("parallel", "parallel"), name="out_proj")(a, b, main, tail, gate)


def _merge_kernel(a_ref, b_ref, bt_ref, r_ref, rt_ref, ga_ref, gb_ref, gc_ref, wa_ref, wb_ref, wc_ref, o_ref, *,
                  n_main):
    is_tail = pl.program_id(1) >= n_main

    def branch(x, w_ref, g_ref):
        return jax.nn.sigmoid(g_ref[...].astype(F32)) * _dot(x, w_ref[...])
    y = (branch(a_ref[...], wa_ref, ga_ref)
         + branch(jnp.where(is_tail, bt_ref[...], b_ref[...]), wb_ref, gb_ref)
         + branch(jnp.where(is_tail, rt_ref[...], r_ref[...]), wc_ref, gc_ref))
    o_ref[...] = y.astype(o_ref.dtype)


def merge_branches(z, a, bb, r, w_a, w_b, w_c, rows):
    d = w_a.shape[1]
    tm, tn = ROW_TILE, COL_TILE
    nb = d // tn
    n_main = bb[0].shape[0] // tm
    row_spec = lambda arr: BS((tm, arr.shape[1]), lambda j, i: (i, 0))
    main_spec = lambda arr: BS((tm, arr.shape[1]), lambda j, i: (jnp.minimum(i, n_main - 1), 0))
    tail_spec = lambda arr: BS((tm, arr.shape[1]), lambda j, i: (jnp.maximum(i - n_main, 0), 0))
    w_spec = lambda arr: BS((arr.shape[0], tn), lambda j, i: (0, j))
    gate_spec = lambda g: BS((tm, tn), lambda j, i: (i, g * nb + j))
    return pl.pallas_call(
        functools.partial(_merge_kernel, n_main=n_main), out_shape=SDS((rows, d), BF16), grid=(nb, rows // tm),
        in_specs=[row_spec(a), main_spec(bb[0]), tail_spec(bb[1]), main_spec(r[0]), tail_spec(r[1]),
                  gate_spec(0), gate_spec(1), gate_spec(2), w_spec(w_a), w_spec(w_b), w_spec(w_c)],
        out_specs=BS((tm, tn), lambda j, i: (i, j)),
        compiler_params=_params("parallel", "parallel"), name="merge_branches")(
            a, bb[0], bb[1], r[0], r[1], z, z, z, w_a, w_b, w_c)


GM_CHUNKS_PER_STEP = 4


def _gmlp_kernel(u_ref, v_ref, ng_ref, ws_ref, bs_ref, o_ref):
    for c in range(GM_CHUNKS_PER_STEP):
        rows = slice(c * CHUNK, (c + 1) * CHUNK)
        v = jax.nn.gelu(v_ref[rows, :].astype(F32))
        mu = jnp.mean(v, axis=-1, keepdims=True)
        var = jnp.mean(jnp.square(v - mu), axis=-1, keepdims=True)
        vn = ((v - mu) * lax.rsqrt(var + EPS) * ng_ref[...]).astype(BF16)
        for g in range(N_MIX_HEADS):
            cols = slice(g * HEAD_DIM, (g + 1) * HEAD_DIM)
            mixed = _dot(ws_ref[g], vn[:, cols]) + bs_ref[g]
            u = jax.nn.gelu(u_ref[rows, cols].astype(F32))
            o_ref[rows, cols] = (u * mixed).astype(o_ref.dtype)


def chunk_gmlp(z, cols, norm_g, ws, bs, rows):
    tr = GM_CHUNKS_PER_STEP * CHUNK
    ub, vb = cols["gm_u"] // GM_WIDTH, cols["gm_v"] // GM_WIDTH
    bs_b = jnp.broadcast_to(bs[:, :, None], bs.shape + (HEAD_DIM,)).astype(F32)
    return pl.pallas_call(
        _gmlp_kernel, out_shape=SDS((rows, GM_WIDTH), BF16), grid=(rows // tr,),
        in_specs=[BS((tr, GM_WIDTH), lambda i: (i, ub)), BS((tr, GM_WIDTH), lambda i: (i, vb)),
                  BS((1, GM_WIDTH), lambda i: (0, 0)), BS(ws.shape, lambda i: (0, 0, 0)),
                  BS(bs_b.shape, lambda i: (0, 0, 0))],
        out_specs=BS((tr, GM_WIDTH), lambda i: (i, 0)),
        compiler_params=_params("parallel"), name="chunk_gmlp")(
            z, z, norm_g.reshape(1, GM_WIDTH), ws.astype(BF16), bs_b)


def _na_bias_tables(rpb, grid_rows):
    nq = grid_rows // NA_QROWS
    depth, n_heads = rpb.shape[:2]
    n_dr = 2 * NA_ROWS - 1
    i = np.arange(NA_QROWS)[:, None]
    m = np.arange(NA_KROWS)[None, :]
    c = np.arange(GRID_W)[:, None]
    w = np.arange(GRID_W)[None, :]
    c_start = np.clip(c - NA_COLS // 2, 0, GRID_W - NA_COLS)
    col_ok = (w >= c_start) & (w < c_start + NA_COLS)
    dc = np.clip(w - c + NA_COLS - 1, 0, 2 * NA_COLS - 2)
    col_sel = (dc.reshape(-1)[None, :] == np.arange(2 * NA_COLS - 1)[:, None]).astype(np.float32)
    band = jnp.einsum("lhab,bq->lhaq", rpb.astype(F32), col_sel, precision=lax.Precision.HIGHEST)
    band = jnp.where(col_ok[None, None, None], band.reshape(depth, n_heads, n_dr, GRID_W, GRID_W), NEG_INF)
    band = jnp.concatenate([band, jnp.full((depth, n_heads, 1, GRID_W, GRID_W), NEG_INF, F32)], axis=2)
    slots, offsets = [], []
    for j in (0, 1, nq - 1):
        base = int(np.clip(NA_QROWS * j - NA_ROWS // 2, 0, grid_rows - NA_KROWS))
        r = NA_QROWS * j + i
        kr = base + m
        r_start = np.clip(r - NA_ROWS // 2, 0, grid_rows - NA_ROWS)
        row_ok = (kr >= r_start) & (kr < r_start + NA_ROWS)
        slot = np.where(row_ok, kr - r + NA_ROWS - 1, n_dr)
        for s in range(NA_QROWS // NA_SUB_QROWS):
            sub = slice(s * NA_SUB_QROWS, (s + 1) * NA_SUB_QROWS)
            off = min(int(np.argmax(row_ok[sub].any(axis=0))), NA_KROWS - NA_SUB_KROWS)
            assert not row_ok[sub, :off].any() and not row_ok[sub, off + NA_SUB_KROWS:].any()
            slots.append(slot[sub, off:off + NA_SUB_KROWS])
            offsets.append(off)
    n_sub = NA_QROWS // NA_SUB_QROWS
    slots = jnp.asarray(np.stack(slots).reshape(-1), jnp.int32)
    out_block = (n_sub, NA_SUB_QROWS * GRID_W, NA_SUB_KROWS * GRID_W)
    grid_spec = pltpu.PrefetchScalarGridSpec(
        num_scalar_prefetch=1, grid=(depth, 3, n_heads),
        in_specs=[BS((None, None, n_dr + 1, GRID_W, GRID_W), lambda l, k, h, s: (l, h, 0, 0, 0))],
        out_specs=BS((None, None, None) + out_block, lambda l, k, h, s: (l, k, h, 0, 0, 0)))
    tables = pl.pallas_call(
        _na_bias_kernel, grid_spec=grid_spec, out_shape=SDS((depth, 3, n_heads) + out_block, F32),
        compiler_params=_params("parallel", "parallel", "parallel"), name="na_bias_tables")(slots, band)
    return tables, np.asarray(offsets).reshape(3, n_sub)


def _na_bias_kernel(slot_ref, band_ref, o_ref):
    kind = pl.program_id(1)
    for s in range(NA_QROWS // NA_SUB_QROWS):
        for i in range(NA_SUB_QROWS):
            row = ((kind * (NA_QROWS // NA_SUB_QROWS) + s) * NA_SUB_QROWS + i) * NA_SUB_KROWS
            tiles = [band_ref[slot_ref[row + m]] for m in range(NA_SUB_KROWS)]
            o_ref[s, i * GRID_W:(i + 1) * GRID_W, :] = jnp.concatenate(tiles, axis=1)


def _na_kernel(q_ref, k_ref, v_ref, kc_ref, vc_ref, bias_ref, o_ref, *, grid_rows, offsets):
    j = pl.program_id(2)
    nq = grid_rows // NA_QROWS
    kind = jnp.where(j == 0, 0, jnp.where(j == nq - 1, 2, 1))
    base_row = jnp.clip(NA_QROWS * j - NA_ROWS // 2, 0, grid_rows - NA_KROWS)
    nk = NA_SUB_KROWS * GRID_W
    nsq = NA_SUB_QROWS * GRID_W
    for s in range(NA_QROWS // NA_SUB_QROWS):
        off = jnp.where(j == 0, int(offsets[0, s]), jnp.where(j == nq - 1, int(offsets[2, s]), int(offsets[1, s])))
        start = pl.multiple_of((base_row + off) * GRID_W, (NA_ROWS // 2) * GRID_W)
        rows = slice(s * nsq, (s + 1) * nsq)
        q = (q_ref[rows, :].astype(F32) * (HEAD_DIM ** -0.5)).astype(BF16)
        s_loc = _dot_nt(q, k_ref[pl.ds(start, nk), :]) + bias_ref[kind, s]
        s_ctx = _dot_nt(q, kc_ref[...])
        mx = jnp.maximum(jnp.max(s_loc, axis=-1, keepdims=True), jnp.max(s_ctx, axis=-1, keepdims=True))
        e_loc = jnp.exp(s_loc - mx)
        e_ctx = jnp.exp(s_ctx - mx)
        den = jnp.sum(e_loc, axis=-1, keepdims=True) + jnp.sum(e_ctx, axis=-1, keepdims=True)
        o = _dot(e_loc.astype(BF16), v_ref[pl.ds(start, nk), :]) + _dot(e_ctx.astype(BF16), vc_ref[...])
        o_ref[rows, :] = (o / den).astype(o_ref.dtype)


def neighbourhood_attention(z, cols, bias_tables, layer, batch, seq, n_ctx):
    bias, offsets = bias_tables
    assert np.all(offsets % (NA_ROWS // 2) == 0)
    grid_rows = seq // GRID_W
    assert grid_rows % NA_QROWS == 0 and grid_rows >= 2 * NA_KROWS
    nq = grid_rows // NA_QROWS
    qb = NA_QROWS * GRID_W
    qc, kc, vc = (cols[n] // HEAD_DIM for n in ("na_q", "na_k", "na_v"))
    ctx_blk0 = batch * seq // n_ctx
    return pl.pallas_call(
        functools.partial(_na_kernel, grid_rows=grid_rows, offsets=offsets),
        out_shape=SDS((batch * seq, NA_WIDTH), BF16), grid=(N_MIX_HEADS, batch, nq),
        in_specs=[BS((qb, HEAD_DIM), lambda h, b, j: (b * nq + j, qc + h)),
                  BS((seq, HEAD_DIM), lambda h, b, j: (b, kc + h)),
                  BS((seq, HEAD_DIM), lambda h, b, j: (b, vc + h)),
                  BS((n_ctx, HEAD_DIM), lambda h, b, j: (ctx_blk0 + b, kc + h)),
                  BS((n_ctx, HEAD_DIM), lambda h, b, j: (ctx_blk0 + b, vc + h)),
                  BS((None, 3, None) + bias.shape[3:], lambda h, b, j: (layer, 0, h, 0, 0, 0))],
        out_specs=BS((qb, HEAD_DIM), lambda h, b, j: (b * nq + j, h)),
        compiler_params=_params("parallel", "parallel", "parallel"), name="neighbourhood_attention")(
            z, z, z, z, z, bias)


def _ctx_attn_kernel(q_ref, k_ref, v_ref, o_ref):
    q = (q_ref[...].astype(F32) * (HEAD_DIM ** -0.5)).astype(BF16)
    s = _dot_nt(q, k_ref[...])
    e = jnp.exp(s - jnp.max(s, axis=-1, keepdims=True))
    o = _dot(e.astype(BF16), v_ref[...]) / jnp.sum(e, axis=-1, keepdims=True)
    o_ref[...] = o.astype(o_ref.dtype)


def context_attention(z, cols, batch, seq, n_ctx):
    qc, kc, vc = (cols[n] // HEAD_DIM for n in ("na_q", "na_k", "na_v"))
    blk0 = batch * seq // n_ctx
    spec = lambda col: BS((n_ctx, HEAD_DIM), lambda b, h: (blk0 + b, col + h))
    return pl.pallas_call(
        _ctx_attn_kernel, out_shape=SDS((batch * n_ctx, NA_WIDTH), BF16), grid=(batch, N_MIX_HEADS),
        in_specs=[spec(qc), spec(kc), spec(vc)],
        out_specs=BS((n_ctx, HEAD_DIM), lambda b, h: (b, h)),
        compiler_params=_params("parallel", "parallel"), name="context_attention")(z, z, z)


def _rope(t_ref, cos_ref, sin_ref):
    t = t_ref[...].astype(F32)
    lane = lax.broadcasted_iota(jnp.int32, t.shape, 1)
    quarter = HEAD_DIM // 4
    partner = jnp.where((lane % (2 * quarter)) < quarter,
                        pltpu.roll(t, HEAD_DIM - quarter, 1), pltpu.roll(t, quarter, 1))
    return t * cos_ref[...] + partner * sin_ref[...]


def _rope_tables(n, identity):
    if identity:
        return jnp.ones((n, HEAD_DIM), F32), jnp.zeros((n, HEAD_DIM), F32)
    nf = HEAD_DIM // 4
    pos = jnp.arange(n)
    p_row = (pos // GRID_W).astype(F32)
    p_col = (pos % GRID_W).astype(F32)
    inv = ROPE_BASE ** (-jnp.arange(nf, dtype=F32) / nf)
    a_row = p_row[:, None] * inv[None, :]
    a_col = p_col[:, None] * inv[None, :]
    cos = jnp.concatenate([jnp.cos(a_row), jnp.cos(a_row), jnp.cos(a_col), jnp.cos(a_col)], axis=-1)
    sin = jnp.concatenate([-jnp.sin(a_row), jnp.sin(a_row), -jnp.sin(a_col), jnp.sin(a_col)], axis=-1)
    return cos, sin


def _decay_tables(dec_f, dec_b):
    log_gf = jnp.log1p(-jnp.exp2(dec_f.astype(F32)))
    log_gb = jnp.log1p(-jnp.exp2(dec_b.astype(F32)))
    idx = jnp.arange(CHUNK, dtype=F32)
    rel = idx[:, None] - idx[None, :]
    d_f = jnp.where(rel >= 0, jnp.exp(log_gf[:, None, None] * jnp.maximum(rel, 0.0)), 0.0)
    d_b = jnp.where(rel < 0, jnp.exp(log_gb[:, None, None] * jnp.maximum(-rel, 0.0)), 0.0)
    bcast = lambda v: jnp.broadcast_to(v[:, :, None], v.shape + (RET_DK,))
    return dict(
        log_gf=log_gf, log_gb=log_gb,
        dfb=d_f + d_b,
        qdf=bcast(jnp.exp(log_gf[:, None] * (idx + 1.0))),
        qdb=bcast(jnp.exp(log_gb[:, None] * (CHUNK - idx))),
        kdf=bcast(jnp.exp(log_gf[:, None] * (CHUNK - 1.0 - idx))),
        kdb=bcast(jnp.exp(log_gb[:, None] * idx)),
        cdf=jnp.broadcast_to(jnp.exp(log_gf * CHUNK)[:, None, None], (log_gf.shape[0], 1, RET_DV)),
        cdb=jnp.broadcast_to(jnp.exp(log_gb * CHUNK)[:, None, None], (log_gb.shape[0], 1, RET_DV)),
    )


def _ctx_state_kernel(k_ref, v_ref, wf_ref, wb_ref, sf_ref, sb_ref):
    k = k_ref[...].astype(F32) * (RET_DK ** -0.5)
    v = v_ref[...]
    sf_ref[...] = _dot((k * wf_ref[...]).T.astype(BF16), v)
    sb_ref[...] = _dot((k * wb_ref[...]).T.astype(BF16), v)


def context_state(z, cols, tabs, batch, seq, n_ctx):
    pos = jnp.arange(n_ctx, dtype=F32)
    bcast = lambda v: jnp.broadcast_to(v[:, :, None], v.shape + (RET_DK,))
    w_f = bcast(jnp.exp(tabs["log_gf"][:, None] * ((n_ctx - 1.0) - pos)))
    w_b = bcast(jnp.exp(tabs["log_gb"][:, None] * pos))
    kc, vc = cols["ret_k"] // RET_DK, cols["ret_v"] // RET_DV
    blk0 = batch * seq // n_ctx
    out = SDS((batch, N_MIX_HEADS, RET_DK, RET_DV), F32)
    tab_spec = BS((None, n_ctx, RET_DK), lambda b, h: (h, 0, 0))
    out_spec = BS((None, None, RET_DK, RET_DV), lambda b, h: (b, h, 0, 0))
    return pl.pallas_call(
        _ctx_state_kernel, out_shape=(out, out), grid=(batch, N_MIX_HEADS),
        in_specs=[BS((n_ctx, RET_DK), lambda b, h: (blk0 + b, kc + h)),
                  BS((n_ctx, RET_DV), lambda b, h: (blk0 + b, vc + h)), tab_spec, tab_spec],
        out_specs=(out_spec, out_spec),
        compiler_params=_params("parallel", "parallel"), name="context_state")(z, z, w_f, w_b)


def _ret_state_kernel(kf_ref, vf_ref, cosf_ref, sinf_ref, kb_ref, vb_ref, cosb_ref, sinb_ref,
                      kdf_ref, kdb_ref, cdf_ref, cdb_ref, s0f_ref, s0b_ref, sf_ref, sb_ref, st_ref, *, cpg):
    seq_id = pl.program_id(1) * pl.num_programs(2) + pl.program_id(2)

    @pl.when(pl.program_id(0) == 0)
    def _():
        st_ref[2 * seq_id] = s0f_ref[...]
        st_ref[2 * seq_id + 1] = s0b_ref[...]

    def scan(k_ref, v_ref, cos_ref, sin_ref, kd_ref, cd_ref, out_ref, direction, order):
        kr = _rope(k_ref, cos_ref, sin_ref) * (RET_DK ** -0.5)
        slot = 2 * seq_id + direction
        s = st_ref[slot]
        for c in order:
            rows = slice(c * CHUNK, (c + 1) * CHUNK)
            out_ref[c] = s.astype(out_ref.dtype)
            kc = (kr[rows, :] * kd_ref[...]).T.astype(BF16)
            s = s * cd_ref[...] + _dot(kc, v_ref[rows, :])
        st_ref[slot] = s

    scan(kf_ref, vf_ref, cosf_ref, sinf_ref, kdf_ref, cdf_ref, sf_ref, 0, range(cpg))
    scan(kb_ref, vb_ref, cosb_ref, sinb_ref, kdb_ref, cdb_ref, sb_ref, 1, reversed(range(cpg)))


def _ret_out_kernel(q_ref, k_ref, v_ref, g_ref, cos_ref, sin_ref, sf_ref, sb_ref, dfb_ref, qdf_ref, qdb_ref,
                    o_ref, *, cpg):
    qr = _rope(q_ref, cos_ref, sin_ref)
    kr = (_rope(k_ref, cos_ref, sin_ref) * (RET_DK ** -0.5)).astype(BF16)
    for c in range(cpg):
        rows = slice(c * CHUNK, (c + 1) * CHUNK)
        qc = qr[rows, :]
        p = (_dot_nt(qc.astype(BF16), kr[rows, :]) * dfb_ref[...]).astype(BF16)
        o = (_dot(p, v_ref[rows, :]) + _dot((qc * qdf_ref[...]).astype(BF16), sf_ref[c])
             + _dot((qc * qdb_ref[...]).astype(BF16), sb_ref[c]))
        mu = jnp.mean(o, axis=-1, keepdims=True)
        var = jnp.mean(jnp.square(o - mu), axis=-1, keepdims=True)
        on = (o - mu) * lax.rsqrt(var + EPS)
        g = g_ref[rows, :].astype(F32)
        o_ref[rows, :] = (on * (g * jax.nn.sigmoid(g))).astype(o_ref.dtype)


def retention(z, cols, tabs, s0f, s0b, row0, n, batch, rope_identity):
    nc = n // CHUNK
    cpg = min(RET_CHUNKS_PER_STEP, nc)
    ng = nc // cpg
    gr = cpg * CHUNK
    blk0 = row0 // gr
    qc, kc = cols["ret_q"] // RET_DK, cols["ret_k"] // RET_DK
    vc, gc = cols["ret_v"] // RET_DV, cols["ret_g"] // RET_DV
    cos, sin = _rope_tables(n, rope_identity)
    BSg = lambda shape, f: BS(shape, lambda g, b, h: f(b, h, g))
    fwd = lambda b, h, g: blk0 + b * ng + g
    bwd = lambda b, h, g: blk0 + b * ng + (ng - 1 - g)
    head_tab = lambda w: BSg((None, CHUNK, w), lambda b, h, g: (h, 0, 0))
    c_tab = BSg((None, 1, RET_DV), lambda b, h, g: (h, 0, 0))
    s0_spec = BSg((None, None, RET_DK, RET_DV), lambda b, h, g: (b, h, 0, 0))
    st_shape = SDS((batch, N_MIX_HEADS, nc, RET_DK, RET_DV), BF16)
    st_spec = lambda rev: BSg((None, None, cpg, RET_DK, RET_DV),
                              lambda b, h, g: (b, h, (ng - 1 - g) if rev else g, 0, 0))
    sf, sb = pl.pallas_call(
        functools.partial(_ret_state_kernel, cpg=cpg), out_shape=(st_shape, st_shape),
        grid=(ng, batch, N_MIX_HEADS),
        in_specs=[BSg((gr, RET_DK), lambda b, h, g: (fwd(b, h, g), kc + h)),
                  BSg((gr, RET_DV), lambda b, h, g: (fwd(b, h, g), vc + h)),
                  BSg((gr, RET_DK), lambda b, h, g: (g, 0)), BSg((gr, RET_DK), lambda b, h, g: (g, 0)),
                  BSg((gr, RET_DK), lambda b, h, g: (bwd(b, h, g), kc + h)),
                  BSg((gr, RET_DV), lambda b, h, g: (bwd(b, h, g), vc + h)),
                  BSg((gr, RET_DK), lambda b, h, g: (ng - 1 - g, 0)),
                  BSg((gr, RET_DK), lambda b, h, g: (ng - 1 - g, 0)),
                  head_tab(RET_DK), head_tab(RET_DK), c_tab, c_tab, s0_spec, s0_spec],
        out_specs=(st_spec(False), st_spec(True)),
        scratch_shapes=[pltpu.VMEM((2 * batch * N_MIX_HEADS, RET_DK, RET_DV), F32)],
        compiler_params=_params("arbitrary", "arbitrary", "arbitrary"), name="retention_state")(
            z, z, cos, sin, z, z, cos, sin, tabs["kdf"], tabs["kdb"], tabs["cdf"], tabs["cdb"], s0f, s0b)
    return pl.pallas_call(
        functools.partial(_ret_out_kernel, cpg=cpg), out_shape=SDS((batch * n, RET_V_WIDTH), BF16),
        grid=(ng, batch, N_MIX_HEADS),
        in_specs=[BSg((gr, RET_DK), lambda b, h, g: (fwd(b, h, g), qc + h)),
                  BSg((gr, RET_DK), lambda b, h, g: (fwd(b, h, g), kc + h)),
                  BSg((gr, RET_DV), lambda b, h, g: (fwd(b, h, g), vc + h)),
                  BSg((gr, RET_DV), lambda b, h, g: (fwd(b, h, g), gc + h)),
                  BSg((gr, RET_DK), lambda b, h, g: (g, 0)), BSg((gr, RET_DK), lambda b, h, g: (g, 0)),
                  st_spec(False), st_spec(False),
                  head_tab(CHUNK), head_tab(RET_DK), head_tab(RET_DK)],
        out_specs=BSg((gr, RET_DV), lambda b, h, g: (b * ng + g, h)),
        compiler_params=_params("parallel", "parallel", "parallel"), name="retention_out")(
            z, z, z, z, cos, sin, sf, sb, tabs["dfb"], tabs["qdf"], tabs["qdb"])


def _dispatch_tables(routed, counts):
    t = routed.shape[0]
    n_assign = t * MOE_TOPK
    flat_e = routed[:, 0:MOE_TOPK].astype(jnp.int32).reshape(-1)
    rank = routed[:, MOE_TOPK:2 * MOE_TOPK].astype(jnp.int32).reshape(-1)
    flat_t = jnp.repeat(jnp.arange(t, dtype=jnp.int32), MOE_TOPK)
    counts = counts[0, :MOE_EXPERTS].astype(jnp.int32)
    padded = (counts + MOE_ROWS - 1) // MOE_ROWS * MOE_ROWS
    p_end = jnp.cumsum(padded)
    dest = (p_end - padded)[flat_e] + rank
    n_blocks = (n_assign + MOE_EXPERTS * (MOE_ROWS - 1)) // MOE_ROWS
    row_tok = jnp.zeros((n_blocks * MOE_ROWS,), jnp.int32).at[dest].set(
        flat_t, unique_indices=True, mode="promise_in_bounds")
    blk_start = jnp.arange(n_blocks, dtype=jnp.int32) * MOE_ROWS
    block_e = jnp.minimum(jnp.sum((p_end[None, :] <= blk_start[:, None]).astype(jnp.int32), axis=1),
                          MOE_EXPERTS - 1)
    n_used = (p_end[-1] // MOE_ROWS).astype(jnp.int32).reshape(1)
    return block_e, n_used, row_tok.reshape(n_blocks, 1, MOE_ROWS), dest.reshape(t, MOE_TOPK).astype(jnp.int32)


def _start_row_gather(src_hbm, idx_ref, idx_of_row, dst, sem, n):
    def issue(r, carry):
        pltpu.make_async_copy(src_hbm.at[pl.ds(idx_ref[0, idx_of_row(r)], 1)], dst.at[pl.ds(r, 1)], sem).start()
        return carry
    lax.fori_loop(0, n, issue, 0, unroll=8)


def _wait_row_gather(src_hbm, dst, sem):
    pltpu.make_async_copy(src_hbm.at[pl.ds(0, dst.shape[0])], dst, sem).wait()


MOE_GATHER_AHEAD = 3


def _moe_kernel(be_ref, nu_ref, *refs):
    ahead = MOE_GATHER_AHEAD
    tok_refs = refs[:ahead + 1]
    h_hbm, w1_ref, w3_ref, w2_ref, y_ref, xb, sem = refs[ahead + 1:]
    i = pl.program_id(0)
    n_used = nu_ref[0]
    n_slots = ahead + 1
    slot = i % n_slots
    row = lambda r: r

    for a in range(ahead):
        @pl.when((i == 0) & (a < n_used))
        def _(a=a):
            _start_row_gather(h_hbm, tok_refs[a], row, xb.at[a], sem.at[a], MOE_ROWS)

    @pl.when(i + ahead < n_used)
    def _():
        nxt = (i + ahead) % n_slots
        _start_row_gather(h_hbm, tok_refs[ahead], row, xb.at[nxt], sem.at[nxt], MOE_ROWS)

    @pl.when(i < n_used)
    def _():
        _wait_row_gather(h_hbm, xb.at[slot], sem.at[slot])
        half = xb.shape[2]
        x_lo, x_hi = (v.astype(BF16) for v in _unpack_halves(xb[slot]))
        up = lambda w_ref: _dot(x_lo, w_ref[0:half, :]) + _dot(x_hi, w_ref[half:2 * half, :])
        h1 = up(w1_ref)
        h3 = up(w3_ref)
        hm = (h1 * jax.nn.sigmoid(h1) * h3).astype(BF16)
        y_ref[...] = _pack_halves(_dot(hm, w2_ref[...]))

    @pl.when(i >= n_used)
    def _():
        y_ref[...] = jnp.zeros_like(y_ref)


def moe_experts(h2, tables, w1, w3, w2):
    block_e, n_used, row_tok, _ = tables
    n_blocks = row_tok.shape[0]
    d, hid = w1.shape[1], w1.shape[2]
    assert h2.shape[1] == d // 2 and h2.dtype == jnp.uint32
    grid_spec = pltpu.PrefetchScalarGridSpec(
        num_scalar_prefetch=2, grid=(n_blocks,),
        in_specs=[BS((None, 1, MOE_ROWS), lambda i, be, nu, a=a: (jnp.minimum(i + a, n_blocks - 1), 0, 0),
                     memory_space=pltpu.SMEM) for a in range(MOE_GATHER_AHEAD + 1)] + [
                  BS(memory_space=pl.ANY),
                  BS((None, d, hid), lambda i, be, nu: (be[i], 0, 0)),
                  BS((None, d, hid), lambda i, be, nu: (be[i], 0, 0)),
                  BS((None, hid, d), lambda i, be, nu: (be[i], 0, 0))],
        out_specs=BS((MOE_ROWS, d // 2), lambda i, be, nu: (i, 0)),
        scratch_shapes=[pltpu.VMEM((MOE_GATHER_AHEAD + 1, MOE_ROWS, d // 2), jnp.uint32),
                        pltpu.SemaphoreType.DMA((MOE_GATHER_AHEAD + 1,))])
    return pl.pallas_call(
        _moe_kernel, out_shape=SDS((n_blocks * MOE_ROWS, d // 2), jnp.uint32), grid_spec=grid_spec,
        compiler_params=_params("arbitrary"), name="moe_experts")(
            block_e, n_used, *([row_tok] * (MOE_GATHER_AHEAD + 1)), h2, w1, w3, w2)


def _combine_kernel(pos_ref, pos_next_ref, y_hbm, x_ref, w_ref, g_ref, ng_ref, nsh_ref, nsc_ref, *refs,
                    keep_stream):
    outs, (yb, sem) = refs[:-2], refs[-2:]
    i = pl.program_id(0)
    tt = COMBINE_ROWS
    slot = i % 2
    idx = lambda r: r

    @pl.when(i == 0)
    def _():
        _start_row_gather(y_hbm, pos_ref, idx, yb.at[0], sem.at[0], MOE_TOPK * tt)

    @pl.when(i + 1 < pl.num_programs(0))
    def _():
        _start_row_gather(y_hbm, pos_next_ref, idx, yb.at[1 - slot], sem.at[1 - slot], MOE_TOPK * tt)

    _wait_row_gather(y_hbm, yb.at[slot], sem.at[slot])
    w = w_ref[...]
    w0, w1 = (w[:, 2 * MOE_TOPK + k:2 * MOE_TOPK + k + 1] for k in range(MOE_TOPK))
    half = yb.shape[2]
    halves = (slice(0, half), slice(half, 2 * half))
    ya = _unpack_halves(yb[slot, 0:tt, :])
    yb_ = _unpack_halves(yb[slot, tt:2 * tt, :])
    xn = [x_ref[:, s] + g_ref[:, s] * (a * w0 + b * w1) for s, a, b in zip(halves, ya, yb_)]
    if keep_stream:
        for s, v in zip(halves, xn):
            outs[0][:, s] = v
    h_ref = outs[-1]
    ms = sum(jnp.sum(v * v, axis=-1, keepdims=True) for v in xn) / (2 * half)
    inv = lax.rsqrt(ms + EPS)
    for s, v in zip(halves, xn):
        h_ref[:, s] = ((v * inv * ng_ref[:, s]) * (1.0 + nsc_ref[:, s]) + nsh_ref[:, s]).astype(h_ref.dtype)


def moe_combine(x1, y, pos, routed, gate, rows, seg_rows, norm_g, norm_shift, norm_scale, out_dtype, keep_stream):
    d = x1.shape[1]
    tt = COMBINE_ROWS
    nt = rows // tt
    seg = _seg_map(seg_rows, tt, gate.shape[0])
    nseg = _seg_map(seg_rows, tt, norm_shift.shape[0])
    pos3 = pos.reshape(nt, tt, MOE_TOPK).transpose(0, 2, 1).reshape(nt, 1, MOE_TOPK * tt)
    row_spec = BS((tt, d), lambda i: (i, 0))
    out_shape = ((SDS((rows, d), F32),) if keep_stream else ()) + (SDS((rows, d), out_dtype),)
    res = pl.pallas_call(
        functools.partial(_combine_kernel, keep_stream=keep_stream), out_shape=out_shape, grid=(nt,),
        in_specs=[BS((None, 1, MOE_TOPK * tt), lambda i: (i, 0, 0), memory_space=pltpu.SMEM),
                  BS((None, 1, MOE_TOPK * tt), lambda i: (jnp.minimum(i + 1, nt - 1), 0, 0),
                     memory_space=pltpu.SMEM),
                  BS(memory_space=pl.ANY), row_spec,
                  BS((tt, ROUTER_LANES), lambda i: (i, 0)), BS((None, 1, d), lambda i: (seg(i), 0, 0)),
                  BS((1, d), lambda i: (0, 0)), BS((None, 1, d), lambda i: (nseg(i), 0, 0)),
                  BS((None, 1, d), lambda i: (nseg(i), 0, 0))],
        out_specs=tuple(row_spec for _ in out_shape),
        scratch_shapes=[pltpu.VMEM((2, MOE_TOPK * tt, d // 2), jnp.uint32), pltpu.SemaphoreType.DMA((2,))],
        compiler_params=_params("arbitrary"), name="moe_combine")(
            pos3, pos3, y, x1, routed, gate, norm_g, norm_shift, norm_scale)
    return res if keep_stream else res[0]


def kernel(x, c, ctx, c_ctx, ada_w, ada_b, norm1_g, w_in, gm_norm_g, gm_ws, gm_bs, na_rpb, ret_decay_fwd,
           ret_decay_bwd, w_branch_a, w_branch_b, w_branch_c, w_out, norm2_g, moe_w_group, moe_w_expert,
           moe_w1, moe_w3, moe_w2, final_norm_g):
    batch, seq, d = x.shape
    n_ctx = ctx.shape[1]
    depth = ada_w.shape[0]
    t_lat, t_ctx = batch * seq, batch * n_ctx
    t_all = t_lat + t_ctx
    cols, in_total = _col_offsets(d)
    assert w_in.shape[2] == in_total and seq % ROW_TILE == 0 and t_ctx % ROW_TILE == 0
    assert 8 >= batch + 1

    cond = jnp.concatenate([c, c_ctx[None, :], jnp.zeros((8 - batch - 1, d), c.dtype)], axis=0)
    mod = ada_modulation(jax.nn.silu(cond).astype(BF16), ada_w, ada_b)[:, :batch + 1]
    mod = mod.reshape(depth, batch + 1, 6, 1, d)

    na_bias = _na_bias_tables(na_rpb, seq // GRID_W)
    stream = (x.reshape(t_lat, d), ctx.reshape(t_ctx, d))
    h = norm_mod(stream, norm1_g[0].reshape(1, d), mod[0, :, 0], mod[0, :, 1], seq, BF16)
    for l in range(depth):
        need_ctx = l < depth - 1
        rows = t_all if need_ctx else t_lat
        sh1, sc1, g1, sh2, sc2, g2 = (mod[l, :, k] for k in range(6))

        z, (w_a, w_b, w_c, w_o, w1, w3, w2) = in_proj(
            h, w_in, l, t_all, BF16, [w_branch_a, w_branch_b, w_branch_c, w_out, moe_w1, moe_w3, moe_w2])

        a = chunk_gmlp(z, cols, gm_norm_g[l], gm_ws[l], gm_bs[l], rows)
        bb = neighbourhood_attention(z, cols, na_bias, l, batch, seq, n_ctx)
        tabs = _decay_tables(ret_decay_fwd[l], ret_decay_bwd[l])
        s0f, s0b = context_state(z, cols, tabs, batch, seq, n_ctx)
        r = retention(z, cols, tabs, s0f, s0b, 0, seq, batch, False)
        if need_ctx:
            zeros = jnp.zeros_like(s0f)
            bb = (bb, context_attention(z, cols, batch, seq, n_ctx))
            r = (r, retention(z, cols, tabs, zeros, zeros, t_lat, n_ctx, batch, True))
        else:
            bb, r = (bb, bb), (r, r)

        y = merge_branches(z, a, bb, r, w_a, w_b, w_c, rows)
        x1 = matmul_residual(y, w_o, stream, g1, rows, seq)

        w_router = jnp.concatenate(
            [moe_w_group[l], moe_w_expert[l],
             jnp.zeros((d, ROUTER_LANES - MOE_GROUPS - MOE_EXPERTS), F32)], axis=1).astype(BF16)
        h2, routed, counts = norm_router(x1, norm2_g[l].reshape(1, d), sh2, sc2, w_router, rows, seq)
        tables = _dispatch_tables(routed, counts)
        y_moe = moe_experts(h2, tables, w1, w3, w2)
        if need_ctx:
            xs, h = moe_combine(x1, y_moe, tables[3], routed, g2, rows, seq, norm1_g[l + 1].reshape(1, d),
                                mod[l + 1, :, 0], mod[l + 1, :, 1], BF16, True)
            stream = (xs, xs)
        else:
            zero = jnp.zeros((1, 1, d), F32)
            out = moe_combine(x1, y_moe, tables[3], routed, g2, rows, seq, final_norm_g.reshape(1, d),
                              zero, zero, F32, False)
    return out.reshape(batch, seq, d)
```

```python
import functools

import jax
import jax.numpy as jnp
import numpy as np
from jax import lax
from jax.experimental import pallas as pl
from jax.experimental.pallas import tpu as pltpu

F32 = jnp.float32
BF16 = jnp.bfloat16
SDS = jax.ShapeDtypeStruct
BS = pl.BlockSpec

EPS = 1e-6
NEG_INF = -1e30
HEAD_DIM = 128
GRID_W = 64
NA_ROWS = 8
NA_COLS = 16
NA_QROWS = 16
NA_KROWS = NA_QROWS + NA_ROWS
NA_SUB_QROWS = 4
NA_SUB_KROWS = NA_SUB_QROWS + NA_ROWS
RET_DK = 128
RET_DV = 256
CHUNK = 128
RET_CHUNKS_PER_STEP = 32
ROPE_BASE = 10000.0
MOE_GROUPS = 4
MOE_EPG = 8
MOE_EXPERTS = MOE_GROUPS * MOE_EPG
MOE_TOPK = 2
ROUTER_LANES = 128
N_MIX_HEADS = 8
GM_WIDTH = N_MIX_HEADS * HEAD_DIM
NA_WIDTH = N_MIX_HEADS * HEAD_DIM
RET_QK_WIDTH = N_MIX_HEADS * RET_DK
RET_V_WIDTH = N_MIX_HEADS * RET_DV

V7X_VMEM_BYTES = 64 * 1024 * 1024
VMEM_LIMIT = V7X_VMEM_BYTES - 8 * 1024 * 1024

ROW_TILE = 512
COL_TILE = 1024
MOE_ROWS = 256
COMBINE_ROWS = 256


def _params(*sem):
    return pltpu.CompilerParams(dimension_semantics=sem, vmem_limit_bytes=VMEM_LIMIT)


def _col_offsets(d):
    off, out = 0, {}
    for name, width in (("gate_a", d), ("gate_b", d), ("gate_c", d), ("gm_u", GM_WIDTH), ("gm_v", GM_WIDTH),
                        ("na_q", NA_WIDTH), ("na_k", NA_WIDTH), ("na_v", NA_WIDTH),
                        ("ret_q", RET_QK_WIDTH), ("ret_k", RET_QK_WIDTH), ("ret_v", RET_V_WIDTH),
                        ("ret_g", RET_V_WIDTH)):
        out[name] = off
        off += width
    return out, off


def _dot(a, b):
    return jnp.dot(a, b, preferred_element_type=F32)


def _dot_nt(a, b):
    return lax.dot_general(a, b, (((1,), (1,)), ((), ())), preferred_element_type=F32)


_HI16 = 0xFFFF0000


def _pack_halves(v):
    n = v.shape[1] // 2
    bits = lax.bitcast_convert_type(v.astype(BF16).astype(F32), jnp.uint32)
    return lax.shift_right_logical(bits[:, :n], jnp.uint32(16)) | (bits[:, n:] & jnp.uint32(_HI16))


def _unpack_halves(u):
    lo = lax.bitcast_convert_type(lax.shift_left(u, jnp.uint32(16)), F32)
    hi = lax.bitcast_convert_type(u & jnp.uint32(_HI16), F32)
    return lo, hi


def _norm_mod(x_ref, g_ref, sh_ref, sc_ref):
    x = x_ref[...]
    ms = jnp.mean(x * x, axis=-1, keepdims=True)
    y = x * lax.rsqrt(ms + EPS) * g_ref[...]
    return y * (1.0 + sc_ref[...]) + sh_ref[...]


def _norm_kernel(x_ref, xt_ref, g_ref, sh_ref, sc_ref, o_ref, *, n_main):
    @pl.when(pl.program_id(0) < n_main)
    def _():
        o_ref[...] = _norm_mod(x_ref, g_ref, sh_ref, sc_ref).astype(o_ref.dtype)

    @pl.when(pl.program_id(0) >= n_main)
    def _():
        o_ref[...] = _norm_mod(xt_ref, g_ref, sh_ref, sc_ref).astype(o_ref.dtype)


def _first_lane(mask, lane):
    return jnp.min(jnp.where(mask, lane, float(ROUTER_LANES)), axis=-1, keepdims=True)


def _route(logits, run_ref):
    n = logits.shape[0]
    lane = lax.broadcasted_iota(jnp.int32, logits.shape, 1).astype(F32)
    neg = -jnp.inf
    is_g = lane < MOE_GROUPS
    gl = jnp.where(is_g, logits, neg)
    eg = jnp.exp(gl - jnp.max(gl, axis=-1, keepdims=True))
    pg = eg / jnp.sum(eg, axis=-1, keepdims=True)
    p_sel = jnp.max(pg, axis=-1, keepdims=True)
    grp = _first_lane((pg == p_sel) & is_g, lane)
    lo = MOE_GROUPS + MOE_EPG * grp
    is_e = (lane >= lo) & (lane < lo + MOE_EPG)
    el = jnp.where(is_e, logits, neg)
    ee = jnp.exp(el - jnp.max(el, axis=-1, keepdims=True))
    pe = ee / jnp.sum(ee, axis=-1, keepdims=True)
    t1 = jnp.max(pe, axis=-1, keepdims=True)
    i1 = _first_lane((pe == t1) & is_e, lane)
    rest = jnp.where(is_e & (lane != i1), pe, -1.0)
    t2 = jnp.max(rest, axis=-1, keepdims=True)
    i2 = _first_lane(rest == t2, lane)
    tsum = t1 + t2
    w1, w2 = p_sel * t1 / tsum, p_sel * t2 / tsum
    e1, e2 = i1 - MOE_GROUPS, i2 - MOE_GROUPS
    oh1, oh2 = lane == e1, lane == e2
    cnt = jnp.where(oh1 | oh2, 1.0, 0.0)
    r_i = lax.broadcasted_iota(jnp.int32, (n, n), 0)
    c_i = lax.broadcasted_iota(jnp.int32, (n, n), 1)
    tri = jnp.where(c_i < r_i, 1.0, 0.0).astype(BF16)
    before = _dot(tri, cnt.astype(BF16)) + run_ref[...]
    r1 = jnp.sum(jnp.where(oh1, before, 0.0), axis=-1, keepdims=True)
    r2 = jnp.sum(jnp.where(oh2, before, 0.0), axis=-1, keepdims=True)
    run_ref[...] = run_ref[...] + jnp.sum(cnt, axis=0, keepdims=True)
    slab = jnp.zeros(logits.shape, F32)
    for k, v in enumerate((e1, e2, r1, r2, w1, w2)):
        slab = jnp.where(lane == float(k), v, slab)
    return slab


def _norm_router_kernel(x_ref, g_ref, sh_ref, sc_ref, wr_ref, h_ref, rt_ref, cnt_ref, run_ref):
    @pl.when(pl.program_id(0) == 0)
    def _():
        run_ref[...] = jnp.zeros_like(run_ref)
    h = _norm_mod(x_ref, g_ref, sh_ref, sc_ref)
    h_ref[...] = _pack_halves(h)
    rt_ref[...] = _route(_dot(h.astype(BF16), wr_ref[...]), run_ref)
    cnt_ref[...] = run_ref[...]


def _seg_map(seg_rows, tile, nseg):
    per = seg_rows // tile
    return lambda i: jnp.minimum(i // per, nseg - 1)


def norm_mod(stream, g, shift, scale, seg_rows, out_dtype):
    main, tail = stream
    d = main.shape[1]
    tr = ROW_TILE // 2
    seg = _seg_map(seg_rows, tr, shift.shape[0])
    n_main = main.shape[0] // tr
    rows = main.shape[0] + tail.shape[0]
    return pl.pallas_call(
        functools.partial(_norm_kernel, n_main=n_main), out_shape=SDS((rows, d), out_dtype), grid=(rows // tr,),
        in_specs=[BS((tr, d), lambda i: (jnp.minimum(i, n_main - 1), 0)),
                  BS((tr, d), lambda i: (jnp.maximum(i - n_main, 0), 0)), BS((1, d), lambda i: (0, 0)),
                  BS((None, 1, d), lambda i: (seg(i), 0, 0)), BS((None, 1, d), lambda i: (seg(i), 0, 0))],
        out_specs=BS((tr, d), lambda i: (i, 0)),
        compiler_params=_params("parallel"), name="norm_mod")(main, tail, g, shift, scale)


def norm_router(x, g, shift, scale, w_router, rows, seg_rows):
    d = x.shape[1]
    tr = ROW_TILE // 2
    seg = _seg_map(seg_rows, tr, shift.shape[0])
    return pl.pallas_call(
        _norm_router_kernel,
        out_shape=(SDS((rows, d // 2), jnp.uint32), SDS((rows, ROUTER_LANES), F32),
                   SDS((1, ROUTER_LANES), F32)),
        grid=(rows // tr,),
        in_specs=[BS((tr, d), lambda i: (i, 0)), BS((1, d), lambda i: (0, 0)),
                  BS((None, 1, d), lambda i: (seg(i), 0, 0)), BS((None, 1, d), lambda i: (seg(i), 0, 0)),
                  BS((d, ROUTER_LANES), lambda i: (0, 0))],
        out_specs=(BS((tr, d // 2), lambda i: (i, 0)), BS((tr, ROUTER_LANES), lambda i: (i, 0)),
                   BS((1, ROUTER_LANES), lambda i: (0, 0))),
        scratch_shapes=[pltpu.VMEM((1, ROUTER_LANES), F32)],
        compiler_params=_params("arbitrary"), name="norm_router")(x, g, shift, scale, w_router)


def _ada_kernel(a_ref, w_ref, b_ref, o_ref):
    o_ref[...] = _dot(a_ref[...], w_ref[...].astype(BF16)) + b_ref[...]


def ada_modulation(act, ada_w, ada_b):
    depth, d, n = ada_w.shape
    tn = ROW_TILE
    return pl.pallas_call(
        _ada_kernel, out_shape=SDS((depth, act.shape[0], n), F32), grid=(depth, n // tn),
        in_specs=[BS(act.shape, lambda l, j: (0, 0)), BS((None, d, tn), lambda l, j: (l, 0, j)),
                  BS((None, 1, tn), lambda l, j: (l, 0, j))],
        out_specs=BS((None, act.shape[0], tn), lambda l, j: (l, 0, j)),
        compiler_params=_params("parallel", "parallel"), name="ada_modulation")(
            act, ada_w, ada_b.reshape(depth, 1, n))


def _in_proj_kernel(a_ref, b_ref, *refs, n_side):
    side_in, o_ref, side_out, bq_ref = refs[:n_side], refs[n_side], refs[n_side + 1:-1], refs[-1]
    @pl.when(pl.program_id(1) == 0)
    def _():
        bq_ref[...] = b_ref[...].astype(BF16)
    o_ref[...] = _dot(a_ref[...], bq_ref[...]).astype(o_ref.dtype)
    for s_in, s_out in zip(side_in, side_out):
        s_out[...] = s_in[...].astype(BF16)


BF16_SUBLANES = 16


def in_proj(a, w, layer, rows, out_dtype, side):
    _, k, n = w.shape
    tm, tn = ROW_TILE, COL_TILE
    nj, ni = n // tn, rows // tm
    n_steps = nj * ni
    in_specs, out_specs, out_shapes = [], [], []
    for s in side:
        r, c = s.shape[-2:]
        n_mat = int(np.prod(s.shape[1:-2]))
        rps = BF16_SUBLANES
        while n_mat * (r // rps) > n_steps:
            rps *= 2
        assert r % rps == 0
        bpm = r // rps
        blk = lambda j, i, last=n_mat * bpm - 1: jnp.minimum(j * ni + i, last)
        if s.ndim == 4:
            in_specs.append(BS((None, None, rps, c),
                               lambda j, i, blk=blk, bpm=bpm: (layer, blk(j, i) // bpm, blk(j, i) % bpm, 0)))
            out_specs.append(BS((None, rps, c), lambda j, i, blk=blk, bpm=bpm: (blk(j, i) // bpm, blk(j, i) % bpm, 0)))
        else:
            in_specs.append(BS((None, rps, c), lambda j, i, blk=blk: (layer, blk(j, i), 0)))
            out_specs.append(BS((rps, c), lambda j, i, blk=blk: (blk(j, i), 0)))
        out_shapes.append(SDS(s.shape[1:], BF16))
    res = pl.pallas_call(
        functools.partial(_in_proj_kernel, n_side=len(side)),
        out_shape=[SDS((rows, n), out_dtype)] + out_shapes, grid=(nj, ni),
        in_specs=[BS((tm, k), lambda j, i: (i, 0)),
                  BS((None, k, tn), lambda j, i: (layer, 0, j), pipeline_mode=pl.Buffered(1))] + in_specs,
        out_specs=[BS((tm, tn), lambda j, i: (i, j))] + out_specs,
        scratch_shapes=[pltpu.VMEM((k, tn), BF16)],
        compiler_params=_params("arbitrary", "arbitrary"), name="in_proj")(a, w, *side)
    return res[0], list(res[1:])


def _mm_res_kernel(a_ref, b_ref, x_ref, xt_ref, g_ref, o_ref, *, n_main):
    upd = g_ref[...] * _dot(a_ref[...], b_ref[...])

    @pl.when(pl.program_id(1) < n_main)
    def _():
        o_ref[...] = x_ref[...] + upd

    @pl.when(pl.program_id(1) >= n_main)
    def _():
        o_ref[...] = xt_ref[...] + upd


def matmul_residual(a, b, stream, gate, rows, seg_rows):
    k, n = b.shape
    tm, tn = ROW_TILE, COL_TILE
    seg = _seg_map(seg_rows, tm, gate.shape[0])
    main, tail = stream
    n_main = main.shape[0] // tm
    return pl.pallas_call(
        functools.partial(_mm_res_kernel, n_main=n_main), out_shape=SDS((rows, n), F32),
        grid=(n // tn, rows // tm),
        in_specs=[BS((tm, k), lambda j, i: (i, 0)), BS((k, tn), lambda j, i: (0, j)),
                  BS((tm, tn), lambda j, i: (jnp.minimum(i, n_main - 1), j)),
                  BS((tm, tn), lambda j, i: (jnp.maximum(i - n_main, 0), j)),
                  BS((None, 1, tn), lambda j, i: (seg(i), 0, j))],
        out_specs=BS((tm, tn), lambda j, i: (i, j)),
        compiler_params=_params("parallel", "parallel"), name="out_proj")(a, b, main, tail, gate)


def _merge_kernel(a_ref, b_ref, bt_ref, r_ref, rt_ref, ga_ref, gb_ref, gc_ref, wa_ref, wb_ref, wc_ref, o_ref, *,
                  n_main):
    is_tail = pl.program_id(1) >= n_main

    def branch(x, w_ref, g_ref):
        return jax.nn.sigmoid(g_ref[...].astype(F32)) * _dot(x, w_ref[...])
    y = (branch(a_ref[...], wa_ref, ga_ref)
         + branch(jnp.where(is_tail, bt_ref[...], b_ref[...]), wb_ref, gb_ref)
         + branch(jnp.where(is_tail, rt_ref[...], r_ref[...]), wc_ref, gc_ref))
    o_ref[...] = y.astype(o_ref.dtype)


def merge_branches(z, a, bb, r, w_a, w_b, w_c, rows):
    d = w_a.shape[1]
    tm, tn = ROW_TILE, COL_TILE
    nb = d // tn
    n_main = bb[0].shape[0] // tm
    row_spec = lambda arr: BS((tm, arr.shape[1]), lambda j, i: (i, 0))
    main_spec = lambda arr: BS((tm, arr.shape[1]), lambda j, i: (jnp.minimum(i, n_main - 1), 0))
    tail_spec = lambda arr: BS((tm, arr.shape[1]), lambda j, i: (jnp.maximum(i - n_main, 0), 0))
    w_spec = lambda arr: BS((arr.shape[0], tn), lambda j, i: (0, j))
    gate_spec = lambda g: BS((tm, tn), lambda j, i: (i, g * nb + j))
    return pl.pallas_call(
        functools.partial(_merge_kernel, n_main=n_main), out_shape=SDS((rows, d), BF16), grid=(nb, rows // tm),
        in_specs=[row_spec(a), main_spec(bb[0]), tail_spec(bb[1]), main_spec(r[0]), tail_spec(r[1]),
                  gate_spec(0), gate_spec(1), gate_spec(2), w_spec(w_a), w_spec(w_b), w_spec(w_c)],
        out_specs=BS((tm, tn), lambda j, i: (i, j)),
        compiler_params=_params("parallel", "parallel"), name="merge_branches")(
            a, bb[0], bb[1], r[0], r[1], z, z, z, w_a, w_b, w_c)


GM_CHUNKS_PER_STEP = 4


def _gmlp_kernel(u_ref, v_ref, ng_ref, ws_ref, bs_ref, o_ref):
    for c in range(GM_CHUNKS_PER_STEP):
        rows = slice(c * CHUNK, (c + 1) * CHUNK)
        v = jax.nn.gelu(v_ref[rows, :].astype(F32))
        mu = jnp.mean(v, axis=-1, keepdims=True)
        var = jnp.mean(jnp.square(v - mu), axis=-1, keepdims=True)
        vn = ((v - mu) * lax.rsqrt(var + EPS) * ng_ref[...]).astype(BF16)
        for g in range(N_MIX_HEADS):
            cols = slice(g * HEAD_DIM, (g + 1) * HEAD_DIM)
            mixed = _dot(ws_ref[g], vn[:, cols]) + bs_ref[g]
            u = jax.nn.gelu(u_ref[rows, cols].astype(F32))
            o_ref[rows, cols] = (u * mixed).astype(o_ref.dtype)


def chunk_gmlp(z, cols, norm_g, ws, bs, rows):
    tr = GM_CHUNKS_PER_STEP * CHUNK
    ub, vb = cols["gm_u"] // GM_WIDTH, cols["gm_v"] // GM_WIDTH
    bs_b = jnp.broadcast_to(bs[:, :, None], bs.shape + (HEAD_DIM,)).astype(F32)
    return pl.pallas_call(
        _gmlp_kernel, out_shape=SDS((rows, GM_WIDTH), BF16), grid=(rows // tr,),
        in_specs=[BS((tr, GM_WIDTH), lambda i: (i, ub)), BS((tr, GM_WIDTH), lambda i: (i, vb)),
                  BS((1, GM_WIDTH), lambda i: (0, 0)), BS(ws.shape, lambda i: (0, 0, 0)),
                  BS(bs_b.shape, lambda i: (0, 0, 0))],
        out_specs=BS((tr, GM_WIDTH), lambda i: (i, 0)),
        compiler_params=_params("parallel"), name="chunk_gmlp")(
            z, z, norm_g.reshape(1, GM_WIDTH), ws.astype(BF16), bs_b)


def _na_bias_tables(rpb, grid_rows):
    nq = grid_rows // NA_QROWS
    depth, n_heads = rpb.shape[:2]
    n_dr = 2 * NA_ROWS - 1
    i = np.arange(NA_QROWS)[:, None]
    m = np.arange(NA_KROWS)[None, :]
    c = np.arange(GRID_W)[:, None]
    w = np.arange(GRID_W)[None, :]
    c_start = np.clip(c - NA_COLS // 2, 0, GRID_W - NA_COLS)
    col_ok = (w >= c_start) & (w < c_start + NA_COLS)
    dc = np.clip(w - c + NA_COLS - 1, 0, 2 * NA_COLS - 2)
    col_sel = (dc.reshape(-1)[None, :] == np.arange(2 * NA_COLS - 1)[:, None]).astype(np.float32)
    band = jnp.einsum("lhab,bq->lhaq", rpb.astype(F32), col_sel, precision=lax.Precision.HIGHEST)
    band = jnp.where(col_ok[None, None, None], band.reshape(depth, n_heads, n_dr, GRID_W, GRID_W), NEG_INF)
    band = jnp.concatenate([band, jnp.full((depth, n_heads, 1, GRID_W, GRID_W), NEG_INF, F32)], axis=2)
    slots, offsets = [], []
    for j in (0, 1, nq - 1):
        base = int(np.clip(NA_QROWS * j - NA_ROWS // 2, 0, grid_rows - NA_KROWS))
        r = NA_QROWS * j + i
        kr = base + m
        r_start = np.clip(r - NA_ROWS // 2, 0, grid_rows - NA_ROWS)
        row_ok = (kr >= r_start) & (kr < r_start + NA_ROWS)
        slot = np.where(row_ok, kr - r + NA_ROWS - 1, n_dr)
        for s in range(NA_QROWS // NA_SUB_QROWS):
            sub = slice(s * NA_SUB_QROWS, (s + 1) * NA_SUB_QROWS)
            off = min(int(np.argmax(row_ok[sub].any(axis=0))), NA_KROWS - NA_SUB_KROWS)
            assert not row_ok[sub, :off].any() and not row_ok[sub, off + NA_SUB_KROWS:].any()
            slots.append(slot[sub, off:off + NA_SUB_KROWS])
            offsets.append(off)
    n_sub = NA_QROWS // NA_SUB_QROWS
    slots = jnp.asarray(np.stack(slots).reshape(-1), jnp.int32)
    out_block = (n_sub, NA_SUB_QROWS * GRID_W, NA_SUB_KROWS * GRID_W)
    grid_spec = pltpu.PrefetchScalarGridSpec(
        num_scalar_prefetch=1, grid=(depth, 3, n_heads),
        in_specs=[BS((None, None, n_dr + 1, GRID_W, GRID_W), lambda l, k, h, s: (l, h, 0, 0, 0))],
        out_specs=BS((None, None, None) + out_block, lambda l, k, h, s: (l, k, h, 0, 0, 0)))
    tables = pl.pallas_call(
        _na_bias_kernel, grid_spec=grid_spec, out_shape=SDS((depth, 3, n_heads) + out_block, F32),
        compiler_params=_params("parallel", "parallel", "parallel"), name="na_bias_tables")(slots, band)
    return tables, np.asarray(offsets).reshape(3, n_sub)


def _na_bias_kernel(slot_ref, band_ref, o_ref):
    kind = pl.program_id(1)
    for s in range(NA_QROWS // NA_SUB_QROWS):
        for i in range(NA_SUB_QROWS):
            row = ((kind * (NA_QROWS // NA_SUB_QROWS) + s) * NA_SUB_QROWS + i) * NA_SUB_KROWS
            tiles = [band_ref[slot_ref[row + m]] for m in range(NA_SUB_KROWS)]
            o_ref[s, i * GRID_W:(i + 1) * GRID_W, :] = jnp.concatenate(tiles, axis=1)


def _na_kernel(q_ref, k_ref, v_ref, kc_ref, vc_ref, bias_ref, o_ref, *, grid_rows, offsets):
    j = pl.program_id(2)
    nq = grid_rows // NA_QROWS
    kind = jnp.where(j == 0, 0, jnp.where(j == nq - 1, 2, 1))
    base_row = jnp.clip(NA_QROWS * j - NA_ROWS // 2, 0, grid_rows - NA_KROWS)
    nk = NA_SUB_KROWS * GRID_W
    nsq = NA_SUB_QROWS * GRID_W
    for s in range(NA_QROWS // NA_SUB_QROWS):
        off = jnp.where(j == 0, int(offsets[0, s]), jnp.where(j == nq - 1, int(offsets[2, s]), int(offsets[1, s])))
        start = pl.multiple_of((base_row + off) * GRID_W, (NA_ROWS // 2) * GRID_W)
        rows = slice(s * nsq, (s + 1) * nsq)
        q = (q_ref[rows, :].astype(F32) * (HEAD_DIM ** -0.5)).astype(BF16)
        s_loc = _dot_nt(q, k_ref[pl.ds(start, nk), :]) + bias_ref[kind, s]
        s_ctx = _dot_nt(q, kc_ref[...])
        mx = jnp.maximum(jnp.max(s_loc, axis=-1, keepdims=True), jnp.max(s_ctx, axis=-1, keepdims=True))
        e_loc = jnp.exp(s_loc - mx)
        e_ctx = jnp.exp(s_ctx - mx)
        den = jnp.sum(e_loc, axis=-1, keepdims=True) + jnp.sum(e_ctx, axis=-1, keepdims=True)
        o = _dot(e_loc.astype(BF16), v_ref[pl.ds(start, nk), :]) + _dot(e_ctx.astype(BF16), vc_ref[...])
        o_ref[rows, :] = (o / den).astype(o_ref.dtype)


def neighbourhood_attention(z, cols, bias_tables, layer, batch, seq, n_ctx):
    bias, offsets = bias_tables
    assert np.all(offsets % (NA_ROWS // 2) == 0)
    grid_rows = seq // GRID_W
    assert grid_rows % NA_QROWS == 0 and grid_rows >= 2 * NA_KROWS
    nq = grid_rows // NA_QROWS
    qb = NA_QROWS * GRID_W
    qc, kc, vc = (cols[n] // HEAD_DIM for n in ("na_q", "na_k", "na_v"))
    ctx_blk0 = batch * seq // n_ctx
    return pl.pallas_call(
        functools.partial(_na_kernel, grid_rows=grid_rows, offsets=offsets),
        out_shape=SDS((batch * seq, NA_WIDTH), BF16), grid=(N_MIX_HEADS, batch, nq),
        in_specs=[BS((qb, HEAD_DIM), lambda h, b, j: (b * nq + j, qc + h)),
                  BS((seq, HEAD_DIM), lambda h, b, j: (b, kc + h)),
                  BS((seq, HEAD_DIM), lambda h, b, j: (b, vc + h)),
                  BS((n_ctx, HEAD_DIM), lambda h, b, j: (ctx_blk0 + b, kc + h)),
                  BS((n_ctx, HEAD_DIM), lambda h, b, j: (ctx_blk0 + b, vc + h)),
                  BS((None, 3, None) + bias.shape[3:], lambda h, b, j: (layer, 0, h, 0, 0, 0))],
        out_specs=BS((qb, HEAD_DIM), lambda h, b, j: (b * nq + j, h)),
        compiler_params=_params("parallel", "parallel", "parallel"), name="neighbourhood_attention")(
            z, z, z, z, z, bias)


def _ctx_attn_kernel(q_ref, k_ref, v_ref, o_ref):
    q = (q_ref[...].astype(F32) * (HEAD_DIM ** -0.5)).astype(BF16)
    s = _dot_nt(q, k_ref[...])
    e = jnp.exp(s - jnp.max(s, axis=-1, keepdims=True))
    o = _dot(e.astype(BF16), v_ref[...]) / jnp.sum(e, axis=-1, keepdims=True)
    o_ref[...] = o.astype(o_ref.dtype)


def context_attention(z, cols, batch, seq, n_ctx):
    qc, kc, vc = (cols[n] // HEAD_DIM for n in ("na_q", "na_k", "na_v"))
    blk0 = batch * seq // n_ctx
    spec = lambda col: BS((n_ctx, HEAD_DIM), lambda b, h: (blk0 + b, col + h))
    return pl.pallas_call(
        _ctx_attn_kernel, out_shape=SDS((batch * n_ctx, NA_WIDTH), BF16), grid=(batch, N_MIX_HEADS),
        in_specs=[spec(qc), spec(kc), spec(vc)],
        out_specs=BS((n_ctx, HEAD_DIM), lambda b, h: (b, h)),
        compiler_params=_params("parallel", "parallel"), name="context_attention")(z, z, z)


def _rope(t_ref, cos_ref, sin_ref):
    t = t_ref[...].astype(F32)
    lane = lax.broadcasted_iota(jnp.int32, t.shape, 1)
    quarter = HEAD_DIM // 4
    partner = jnp.where((lane % (2 * quarter)) < quarter,
                        pltpu.roll(t, HEAD_DIM - quarter, 1), pltpu.roll(t, quarter, 1))
    return t * cos_ref[...] + partner * sin_ref[...]


def _rope_tables(n, identity):
    if identity:
        return jnp.ones((n, HEAD_DIM), F32), jnp.zeros((n, HEAD_DIM), F32)
    nf = HEAD_DIM // 4
    pos = jnp.arange(n)
    p_row = (pos // GRID_W).astype(F32)
    p_col = (pos % GRID_W).astype(F32)
    inv = ROPE_BASE ** (-jnp.arange(nf, dtype=F32) / nf)
    a_row = p_row[:, None] * inv[None, :]
    a_col = p_col[:, None] * inv[None, :]
    cos = jnp.concatenate([jnp.cos(a_row), jnp.cos(a_row), jnp.cos(a_col), jnp.cos(a_col)], axis=-1)
    sin = jnp.concatenate([-jnp.sin(a_row), jnp.sin(a_row), -jnp.sin(a_col), jnp.sin(a_col)], axis=-1)
    return cos, sin


def _decay_tables(dec_f, dec_b):
    log_gf = jnp.log1p(-jnp.exp2(dec_f.astype(F32)))
    log_gb = jnp.log1p(-jnp.exp2(dec_b.astype(F32)))
    idx = jnp.arange(CHUNK, dtype=F32)
    rel = idx[:, None] - idx[None, :]
    d_f = jnp.where(rel >= 0, jnp.exp(log_gf[:, None, None] * jnp.maximum(rel, 0.0)), 0.0)
    d_b = jnp.where(rel < 0, jnp.exp(log_gb[:, None, None] * jnp.maximum(-rel, 0.0)), 0.0)
    bcast = lambda v: jnp.broadcast_to(v[:, :, None], v.shape + (RET_DK,))
    return dict(
        log_gf=log_gf, log_gb=log_gb,
        dfb=d_f + d_b,
        qdf=bcast(jnp.exp(log_gf[:, None] * (idx + 1.0))),
        qdb=bcast(jnp.exp(log_gb[:, None] * (CHUNK - idx))),
        kdf=bcast(jnp.exp(log_gf[:, None] * (CHUNK - 1.0 - idx))),
        kdb=bcast(jnp.exp(log_gb[:, None] * idx)),
        cdf=jnp.broadcast_to(jnp.exp(log_gf * CHUNK)[:, None, None], (log_gf.shape[0], 1, RET_DV)),
        cdb=jnp.broadcast_to(jnp.exp(log_gb * CHUNK)[:, None, None], (log_gb.shape[0], 1, RET_DV)),
    )


def _ctx_state_kernel(k_ref, v_ref, wf_ref, wb_ref, sf_ref, sb_ref):
    k = k_ref[...].astype(F32) * (RET_DK ** -0.5)
    v = v_ref[...]
    sf_ref[...] = _dot((k * wf_ref[...]).T.astype(BF16), v)
    sb_ref[...] = _dot((k * wb_ref[...]).T.astype(BF16), v)


def context_state(z, cols, tabs, batch, seq, n_ctx):
    pos = jnp.arange(n_ctx, dtype=F32)
    bcast = lambda v: jnp.broadcast_to(v[:, :, None], v.shape + (RET_DK,))
    w_f = bcast(jnp.exp(tabs["log_gf"][:, None] * ((n_ctx - 1.0) - pos)))
    w_b = bcast(jnp.exp(tabs["log_gb"][:, None] * pos))
    kc, vc = cols["ret_k"] // RET_DK, cols["ret_v"] // RET_DV
    blk0 = batch * seq // n_ctx
    out = SDS((batch, N_MIX_HEADS, RET_DK, RET_DV), F32)
    tab_spec = BS((None, n_ctx, RET_DK), lambda b, h: (h, 0, 0))
    out_spec = BS((None, None, RET_DK, RET_DV), lambda b, h: (b, h, 0, 0))
    return pl.pallas_call(
        _ctx_state_kernel, out_shape=(out, out), grid=(batch, N_MIX_HEADS),
        in_specs=[BS((n_ctx, RET_DK), lambda b, h: (blk0 + b, kc + h)),
                  BS((n_ctx, RET_DV), lambda b, h: (blk0 + b, vc + h)), tab_spec, tab_spec],
        out_specs=(out_spec, out_spec),
        compiler_params=_params("parallel", "parallel"), name="context_state")(z, z, w_f, w_b)


def _ret_state_kernel(kf_ref, vf_ref, cosf_ref, sinf_ref, kb_ref, vb_ref, cosb_ref, sinb_ref,
                      kdf_ref, kdb_ref, cdf_ref, cdb_ref, s0f_ref, s0b_ref, sf_ref, sb_ref, st_ref, *, cpg):
    seq_id = pl.program_id(1) * pl.num_programs(2) + pl.program_id(2)

    @pl.when(pl.program_id(0) == 0)
    def _():
        st_ref[2 * seq_id] = s0f_ref[...]
        st_ref[2 * seq_id + 1] = s0b_ref[...]

    def scan(k_ref, v_ref, cos_ref, sin_ref, kd_ref, cd_ref, out_ref, direction, order):
        kr = _rope(k_ref, cos_ref, sin_ref) * (RET_DK ** -0.5)
        slot = 2 * seq_id + direction
        s = st_ref[slot]
        for c in order:
            rows = slice(c * CHUNK, (c + 1) * CHUNK)
            out_ref[c] = s.astype(out_ref.dtype)
            kc = (kr[rows, :] * kd_ref[...]).T.astype(BF16)
            s = s * cd_ref[...] + _dot(kc, v_ref[rows, :])
        st_ref[slot] = s

    scan(kf_ref, vf_ref, cosf_ref, sinf_ref, kdf_ref, cdf_ref, sf_ref, 0, range(cpg))
    scan(kb_ref, vb_ref, cosb_ref, sinb_ref, kdb_ref, cdb_ref, sb_ref, 1, reversed(range(cpg)))


def _ret_out_kernel(q_ref, k_ref, v_ref, g_ref, cos_ref, sin_ref, sf_ref, sb_ref, dfb_ref, qdf_ref, qdb_ref,
                    o_ref, *, cpg):
    qr = _rope(q_ref, cos_ref, sin_ref)
    kr = (_rope(k_ref, cos_ref, sin_ref) * (RET_DK ** -0.5)).astype(BF16)
    for c in range(cpg):
        rows = slice(c * CHUNK, (c + 1) * CHUNK)
        qc = qr[rows, :]
        p = (_dot_nt(qc.astype(BF16), kr[rows, :]) * dfb_ref[...]).astype(BF16)
        o = (_dot(p, v_ref[rows, :]) + _dot((qc * qdf_ref[...]).astype(BF16), sf_ref[c])
             + _dot((qc * qdb_ref[...]).astype(BF16), sb_ref[c]))
        mu = jnp.mean(o, axis=-1, keepdims=True)
        var = jnp.mean(jnp.square(o - mu), axis=-1, keepdims=True)
        on = (o - mu) * lax.rsqrt(var + EPS)
        g = g_ref[rows, :].astype(F32)
        o_ref[rows, :] = (on * (g * jax.nn.sigmoid(g))).astype(o_ref.dtype)


def retention(z, cols, tabs, s0f, s0b, row0, n, batch, rope_identity):
    nc = n // CHUNK
    cpg = min(RET_CHUNKS_PER_STEP, nc)
    ng = nc // cpg
    gr = cpg * CHUNK
    blk0 = row0 // gr
    qc, kc = cols["ret_q"] // RET_DK, cols["ret_k"] // RET_DK
    vc, gc = cols["ret_v"] // RET_DV, cols["ret_g"] // RET_DV
    cos, sin = _rope_tables(n, rope_identity)
    BSg = lambda shape, f: BS(shape, lambda g, b, h: f(b, h, g))
    fwd = lambda b, h, g: blk0 + b * ng + g
    bwd = lambda b, h, g: blk0 + b * ng + (ng - 1 - g)
    head_tab = lambda w: BSg((None, CHUNK, w), lambda b, h, g: (h, 0, 0))
    c_tab = BSg((None, 1, RET_DV), lambda b, h, g: (h, 0, 0))
    s0_spec = BSg((None, None, RET_DK, RET_DV), lambda b, h, g: (b, h, 0, 0))
    st_shape = SDS((batch, N_MIX_HEADS, nc, RET_DK, RET_DV), BF16)
    st_spec = lambda rev: BSg((None, None, cpg, RET_DK, RET_DV),
                              lambda b, h, g: (b, h, (ng - 1 - g) if rev else g, 0, 0))
    sf, sb = pl.pallas_call(
        functools.partial(_ret_state_kernel, cpg=cpg), out_shape=(st_shape, st_shape),
        grid=(ng, batch, N_MIX_HEADS),
        in_specs=[BSg((gr, RET_DK), lambda b, h, g: (fwd(b, h, g), kc + h)),
                  BSg((gr, RET_DV), lambda b, h, g: (fwd(b, h, g), vc + h)),
                  BSg((gr, RET_DK), lambda b, h, g: (g, 0)), BSg((gr, RET_DK), lambda b, h, g: (g, 0)),
                  BSg((gr, RET_DK), lambda b, h, g: (bwd(b, h, g), kc + h)),
                  BSg((gr, RET_DV), lambda b, h, g: (bwd(b, h, g), vc + h)),
                  BSg((gr, RET_DK), lambda b, h, g: (ng - 1 - g, 0)),
                  BSg((gr, RET_DK), lambda b, h, g: (ng - 1 - g, 0)),
                  head_tab(RET_DK), head_tab(RET_DK), c_tab, c_tab, s0_spec, s0_spec],
        out_specs=(st_spec(False), st_spec(True)),
        scratch_shapes=[pltpu.VMEM((2 * batch * N_MIX_HEADS, RET_DK, RET_DV), F32)],
        compiler_params=_params("arbitrary", "arbitrary", "arbitrary"), name="retention_state")(
            z, z, cos, sin, z, z, cos, sin, tabs["kdf"], tabs["kdb"], tabs["cdf"], tabs["cdb"], s0f, s0b)
    return pl.pallas_call(
        functools.partial(_ret_out_kernel, cpg=cpg), out_shape=SDS((batch * n, RET_V_WIDTH), BF16),
        grid=(ng, batch, N_MIX_HEADS),
        in_specs=[BSg((gr, RET_DK), lambda b, h, g: (fwd(b, h, g), qc + h)),
                  BSg((gr, RET_DK), lambda b, h, g: (fwd(b, h, g), kc + h)),
                  BSg((gr, RET_DV), lambda b, h, g: (fwd(b, h, g), vc + h)),
                  BSg((gr, RET_DV), lambda b, h, g: (fwd(b, h, g), gc + h)),
                  BSg((gr, RET_DK), lambda b, h, g: (g, 0)), BSg((gr, RET_DK), lambda b, h, g: (g, 0)),
                  st_spec(False), st_spec(False),
                  head_tab(CHUNK), head_tab(RET_DK), head_tab(RET_DK)],
        out_specs=BSg((gr, RET_DV), lambda b, h, g: (b * ng + g, h)),
        compiler_params=_params("parallel", "parallel", "parallel"), name="retention_out")(
            z, z, z, z, cos, sin, sf, sb, tabs["dfb"], tabs["qdf"], tabs["qdb"])


def _dispatch_tables(routed, counts):
    t = routed.shape[0]
    n_assign = t * MOE_TOPK
    flat_e = routed[:, 0:MOE_TOPK].astype(jnp.int32).reshape(-1)
    rank = routed[:, MOE_TOPK:2 * MOE_TOPK].astype(jnp.int32).reshape(-1)
    flat_t = jnp.repeat(jnp.arange(t, dtype=jnp.int32), MOE_TOPK)
    counts = counts[0, :MOE_EXPERTS].astype(jnp.int32)
    padded = (counts + MOE_ROWS - 1) // MOE_ROWS * MOE_ROWS
    p_end = jnp.cumsum(padded)
    dest = (p_end - padded)[flat_e] + rank
    n_blocks = (n_assign + MOE_EXPERTS * (MOE_ROWS - 1)) // MOE_ROWS
    row_tok = jnp.zeros((n_blocks * MOE_ROWS,), jnp.int32).at[dest].set(
        flat_t, unique_indices=True, mode="promise_in_bounds")
    blk_start = jnp.arange(n_blocks, dtype=jnp.int32) * MOE_ROWS
    block_e = jnp.minimum(jnp.sum((p_end[None, :] <= blk_start[:, None]).astype(jnp.int32), axis=1),
                          MOE_EXPERTS - 1)
    n_used = (p_end[-1] // MOE_ROWS).astype(jnp.int32).reshape(1)
    return block_e, n_used, row_tok.reshape(n_blocks, 1, MOE_ROWS), dest.reshape(t, MOE_TOPK).astype(jnp.int32)


def _start_row_gather(src_hbm, idx_ref, idx_of_row, dst, sem, n):
    def issue(r, carry):
        pltpu.make_async_copy(src_hbm.at[pl.ds(idx_ref[0, idx_of_row(r)], 1)], dst.at[pl.ds(r, 1)], sem).start()
        return carry
    lax.fori_loop(0, n, issue, 0, unroll=8)


def _wait_row_gather(src_hbm, dst, sem):
    pltpu.make_async_copy(src_hbm.at[pl.ds(0, dst.shape[0])], dst, sem).wait()


MOE_GATHER_AHEAD = 3


def _moe_kernel(be_ref, nu_ref, *refs):
    ahead = MOE_GATHER_AHEAD
    tok_refs = refs[:ahead + 1]
    h_hbm, w1_ref, w3_ref, w2_ref, y_ref, xb, sem = refs[ahead + 1:]
    i = pl.program_id(0)
    n_used = nu_ref[0]
    n_slots = ahead + 1
    slot = i % n_slots
    row = lambda r: r

    for a in range(ahead):
        @pl.when((i == 0) & (a < n_used))
        def _(a=a):
            _start_row_gather(h_hbm, tok_refs[a], row, xb.at[a], sem.at[a], MOE_ROWS)

    @pl.when(i + ahead < n_used)
    def _():
        nxt = (i + ahead) % n_slots
        _start_row_gather(h_hbm, tok_refs[ahead], row, xb.at[nxt], sem.at[nxt], MOE_ROWS)

    @pl.when(i < n_used)
    def _():
        _wait_row_gather(h_hbm, xb.at[slot], sem.at[slot])
        half = xb.shape[2]
        x_lo, x_hi = (v.astype(BF16) for v in _unpack_halves(xb[slot]))
        up = lambda w_ref: _dot(x_lo, w_ref[0:half, :]) + _dot(x_hi, w_ref[half:2 * half, :])
        h1 = up(w1_ref)
        h3 = up(w3_ref)
        hm = (h1 * jax.nn.sigmoid(h1) * h3).astype(BF16)
        y_ref[...] = _pack_halves(_dot(hm, w2_ref[...]))

    @pl.when(i >= n_used)
    def _():
        y_ref[...] = jnp.zeros_like(y_ref)


def moe_experts(h2, tables, w1, w3, w2):
    block_e, n_used, row_tok, _ = tables
    n_blocks = row_tok.shape[0]
    d, hid = w1.shape[1], w1.shape[2]
    assert h2.shape[1] == d // 2 and h2.dtype == jnp.uint32
    grid_spec = pltpu.PrefetchScalarGridSpec(
        num_scalar_prefetch=2, grid=(n_blocks,),
        in_specs=[BS((None, 1, MOE_ROWS), lambda i, be, nu, a=a: (jnp.minimum(i + a, n_blocks - 1), 0, 0),
                     memory_space=pltpu.SMEM) for a in range(MOE_GATHER_AHEAD + 1)] + [
                  BS(memory_space=pl.ANY),
                  BS((None, d, hid), lambda i, be, nu: (be[i], 0, 0)),
                  BS((None, d, hid), lambda i, be, nu: (be[i], 0, 0)),
                  BS((None, hid, d), lambda i, be, nu: (be[i], 0, 0))],
        out_specs=BS((MOE_ROWS, d // 2), lambda i, be, nu: (i, 0)),
        scratch_shapes=[pltpu.VMEM((MOE_GATHER_AHEAD + 1, MOE_ROWS, d // 2), jnp.uint32),
                        pltpu.SemaphoreType.DMA((MOE_GATHER_AHEAD + 1,))])
    return pl.pallas_call(
        _moe_kernel, out_shape=SDS((n_blocks * MOE_ROWS, d // 2), jnp.uint32), grid_spec=grid_spec,
        compiler_params=_params("arbitrary"), name="moe_experts")(
            block_e, n_used, *([row_tok] * (MOE_GATHER_AHEAD + 1)), h2, w1, w3, w2)


def _combine_kernel(pos_ref, pos_next_ref, y_hbm, x_ref, w_ref, g_ref, ng_ref, nsh_ref, nsc_ref, *refs,
                    keep_stream):
    outs, (yb, sem) = refs[:-2], refs[-2:]
    i = pl.program_id(0)
    tt = COMBINE_ROWS
    slot = i % 2
    idx = lambda r: r

    @pl.when(i == 0)
    def _():
        _start_row_gather(y_hbm, pos_ref, idx, yb.at[0], sem.at[0], MOE_TOPK * tt)

    @pl.when(i + 1 < pl.num_programs(0))
    def _():
        _start_row_gather(y_hbm, pos_next_ref, idx, yb.at[1 - slot], sem.at[1 - slot], MOE_TOPK * tt)

    _wait_row_gather(y_hbm, yb.at[slot], sem.at[slot])
    w = w_ref[...]
    w0, w1 = (w[:, 2 * MOE_TOPK + k:2 * MOE_TOPK + k + 1] for k in range(MOE_TOPK))
    half = yb.shape[2]
    halves = (slice(0, half), slice(half, 2 * half))
    ya = _unpack_halves(yb[slot, 0:tt, :])
    yb_ = _unpack_halves(yb[slot, tt:2 * tt, :])
    xn = [x_ref[:, s] + g_ref[:, s] * (a * w0 + b * w1) for s, a, b in zip(halves, ya, yb_)]
    if keep_stream:
        for s, v in zip(halves, xn):
            outs[0][:, s] = v
    h_ref = outs[-1]
    ms = sum(jnp.sum(v * v, axis=-1, keepdims=True) for v in xn) / (2 * half)
    inv = lax.rsqrt(ms + EPS)
    for s, v in zip(halves, xn):
        h_ref[:, s] = ((v * inv * ng_ref[:, s]) * (1.0 + nsc_ref[:, s]) + nsh_ref[:, s]).astype(h_ref.dtype)


def moe_combine(x1, y, pos, routed, gate, rows, seg_rows, norm_g, norm_shift, norm_scale, out_dtype, keep_stream):
    d = x1.shape[1]
    tt = COMBINE_ROWS
    nt = rows // tt
    seg = _seg_map(seg_rows, tt, gate.shape[0])
    nseg = _seg_map(seg_rows, tt, norm_shift.shape[0])
    pos3 = pos.reshape(nt, tt, MOE_TOPK).transpose(0, 2, 1).reshape(nt, 1, MOE_TOPK * tt)
    row_spec = BS((tt, d), lambda i: (i, 0))
    out_shape = ((SDS((rows, d), F32),) if keep_stream else ()) + (SDS((rows, d), out_dtype),)
    res = pl.pallas_call(
        functools.partial(_combine_kernel, keep_stream=keep_stream), out_shape=out_shape, grid=(nt,),
        in_specs=[BS((None, 1, MOE_TOPK * tt), lambda i: (i, 0, 0), memory_space=pltpu.SMEM),
                  BS((None, 1, MOE_TOPK * tt), lambda i: (jnp.minimum(i + 1, nt - 1), 0, 0),
                     memory_space=pltpu.SMEM),
                  BS(memory_space=pl.ANY), row_spec,
                  BS((tt, ROUTER_LANES), lambda i: (i, 0)), BS((None, 1, d), lambda i: (seg(i), 0, 0)),
                  BS((1, d), lambda i: (0, 0)), BS((None, 1, d), lambda i: (nseg(i), 0, 0)),
                  BS((None, 1, d), lambda i: (nseg(i), 0, 0))],
        out_specs=tuple(row_spec for _ in out_shape),
        scratch_shapes=[pltpu.VMEM((2, MOE_TOPK * tt, d // 2), jnp.uint32), pltpu.SemaphoreType.DMA((2,))],
        compiler_params=_params("arbitrary"), name="moe_combine")(
            pos3, pos3, y, x1, routed, gate, norm_g, norm_shift, norm_scale)
    return res if keep_stream else res[0]


def kernel(x, c, ctx, c_ctx, ada_w, ada_b, norm1_g, w_in, gm_norm_g, gm_ws, gm_bs, na_rpb, ret_decay_fwd,
           ret_decay_bwd, w_branch_a, w_branch_b, w_branch_c, w_out, norm2_g, moe_w_group, moe_w_expert,
           moe_w1, moe_w3, moe_w2, final_norm_g):
    batch, seq, d = x.shape
    n_ctx = ctx.shape[1]
    depth = ada_w.shape[0]
    t_lat, t_ctx = batch * seq, batch * n_ctx
    t_all = t_lat + t_ctx
    cols, in_total = _col_offsets(d)
    assert w_in.shape[2] == in_total and seq % ROW_TILE == 0 and t_ctx % ROW_TILE == 0
    assert 8 >= batch + 1

    cond = jnp.concatenate([c, c_ctx[None, :], jnp.zeros((8 - batch - 1, d), c.dtype)], axis=0)
    mod = ada_modulation(jax.nn.silu(cond).astype(BF16), ada_w, ada_b)[:, :batch + 1]
    mod = mod.reshape(depth, batch + 1, 6, 1, d)

    na_bias = _na_bias_tables(na_rpb, seq // GRID_W)
    stream = (x.reshape(t_lat, d), ctx.reshape(t_ctx, d))
    h = norm_mod(stream, norm1_g[0].reshape(1, d), mod[0, :, 0], mod[0, :, 1], seq, BF16)
    for l in range(depth):
        need_ctx = l < depth - 1
        rows = t_all if need_ctx else t_lat
        sh1, sc1, g1, sh2, sc2, g2 = (mod[l, :, k] for k in range(6))

        z, (w_a, w_b, w_c, w_o, w1, w3, w2) = in_proj(
            h, w_in, l, t_all, BF16, [w_branch_a, w_branch_b, w_branch_c, w_out, moe_w1, moe_w3, moe_w2])

        a = chunk_gmlp(z, cols, gm_norm_g[l], gm_ws[l], gm_bs[l], rows)
        bb = neighbourhood_attention(z, cols, na_bias, l, batch, seq, n_ctx)
        tabs = _decay_tables(ret_decay_fwd[l], ret_decay_bwd[l])
        s0f, s0b = context_state(z, cols, tabs, batch, seq, n_ctx)
        r = retention(z, cols, tabs, s0f, s0b, 0, seq, batch, False)
        if need_ctx:
            zeros = jnp.zeros_like(s0f)
            bb = (bb, context_attention(z, cols, batch, seq, n_ctx))
            r = (r, retention(z, cols, tabs, zeros, zeros, t_lat, n_ctx, batch, True))
        else:
            bb, r = (bb, bb), (r, r)

        y = merge_branches(z, a, bb, r, w_a, w_b, w_c, rows)
        x1 = matmul_residual(y, w_o, stream, g1, rows, seq)

        w_router = jnp.concatenate(
            [moe_w_group[l], moe_w_expert[l],
             jnp.zeros((d, ROUTER_LANES - MOE_GROUPS - MOE_EXPERTS), F32)], axis=1).astype(BF16)
        h2, routed, counts = norm_router(x1, norm2_g[l].reshape(1, d), sh2, sc2, w_router, rows, seq)
        tables = _dispatch_tables(routed, counts)
        y_moe = moe_experts(h2, tables, w1, w3, w2)
        if need_ctx:
            xs, h = moe_combine(x1, y_moe, tables[3], routed, g2, rows, seq, norm1_g[l + 1].reshape(1, d),
                                mod[l + 1, :, 0], mod[l + 1, :, 1], BF16, True)
            stream = (xs, xs)
        else:
            zero = jnp.zeros((1, 1, d), F32)
            out = moe_combine(x1, y_moe, tables[3], routed, g2, rows, seq, final_norm_g.reshape(1, d),
                              zero, zero, F32, False)
    return out.reshape(batch, seq, d)
```

```python
import functools

import jax
import jax.numpy as jnp
import numpy as np
from jax import lax
from jax.experimental import pallas as pl
from jax.experimental.pallas import tpu as pltpu

F32 = jnp.float32
BF16 = jnp.bfloat16
SDS = jax.ShapeDtypeStruct
BS = pl.BlockSpec

EPS = 1e-6
NEG_INF = -1e30
HEAD_DIM = 128
GRID_W = 64
NA_ROWS = 8
NA_COLS = 16
NA_QROWS = 16
NA_KROWS = NA_QROWS + NA_ROWS
NA_SUB_QROWS = 4
NA_SUB_KROWS = NA_SUB_QROWS + NA_ROWS
RET_DK = 128
RET_DV = 256
CHUNK = 128
RET_CHUNKS_PER_STEP = 32
ROPE_BASE = 10000.0
MOE_GROUPS = 4
MOE_EPG = 8
MOE_EXPERTS = MOE_GROUPS * MOE_EPG
MOE_TOPK = 2
ROUTER_LANES = 128
N_MIX_HEADS = 8
GM_WIDTH = N_MIX_HEADS * HEAD_DIM
NA_WIDTH = N_MIX_HEADS * HEAD_DIM
RET_QK_WIDTH = N_MIX_HEADS * RET_DK
RET_V_WIDTH = N_MIX_HEADS * RET_DV

V7X_VMEM_BYTES = 64 * 1024 * 1024
VMEM_LIMIT = V7X_VMEM_BYTES - 8 * 1024 * 1024

ROW_TILE = 512
COL_TILE = 1024
MOE_ROWS = 256
COMBINE_ROWS = 256


def _params(*sem):
    return pltpu.CompilerParams(dimension_semantics=sem, vmem_limit_bytes=VMEM_LIMIT)


def _col_offsets(d):
    off, out = 0, {}
    for name, width in (("gate_a", d), ("gate_b", d), ("gate_c", d), ("gm_u", GM_WIDTH), ("gm_v", GM_WIDTH),
                        ("na_q", NA_WIDTH), ("na_k", NA_WIDTH), ("na_v", NA_WIDTH),
                        ("ret_q", RET_QK_WIDTH), ("ret_k", RET_QK_WIDTH), ("ret_v", RET_V_WIDTH),
                        ("ret_g", RET_V_WIDTH)):
        out[name] = off
        off += width
    return out, off


def _dot(a, b):
    return jnp.dot(a, b, preferred_element_type=F32)


def _dot_nt(a, b):
    return lax.dot_general(a, b, (((1,), (1,)), ((), ())), preferred_element_type=F32)


_HI16 = 0xFFFF0000


def _pack_halves(v):
    n = v.shape[1] // 2
    bits = lax.bitcast_convert_type(v.astype(BF16).astype(F32), jnp.uint32)
    return lax.shift_right_logical(bits[:, :n], jnp.uint32(16)) | (bits[:, n:] & jnp.uint32(_HI16))


def _unpack_halves(u):
    lo = lax.bitcast_convert_type(lax.shift_left(u, jnp.uint32(16)), F32)
    hi = lax.bitcast_convert_type(u & jnp.uint32(_HI16), F32)
    return lo, hi


def _norm_mod(x_ref, g_ref, sh_ref, sc_ref):
    x = x_ref[...]
    ms = jnp.mean(x * x, axis=-1, keepdims=True)
    y = x * lax.rsqrt(ms + EPS) * g_ref[...]
    return y * (1.0 + sc_ref[...]) + sh_ref[...]


def _norm_kernel(x_ref, xt_ref, g_ref, sh_ref, sc_ref, o_ref, *, n_main):
    @pl.when(pl.program_id(0) < n_main)
    def _():
        o_ref[...] = _norm_mod(x_ref, g_ref, sh_ref, sc_ref).astype(o_ref.dtype)

    @pl.when(pl.program_id(0) >= n_main)
    def _():
        o_ref[...] = _norm_mod(xt_ref, g_ref, sh_ref, sc_ref).astype(o_ref.dtype)


def _first_lane(mask, lane):
    return jnp.min(jnp.where(mask, lane, float(ROUTER_LANES)), axis=-1, keepdims=True)


def _route(logits, run_ref):
    n = logits.shape[0]
    lane = lax.broadcasted_iota(jnp.int32, logits.shape, 1).astype(F32)
    neg = -jnp.inf
    is_g = lane < MOE_GROUPS
    gl = jnp.where(is_g, logits, neg)
    eg = jnp.exp(gl - jnp.max(gl, axis=-1, keepdims=True))
    pg = eg / jnp.sum(eg, axis=-1, keepdims=True)
    p_sel = jnp.max(pg, axis=-1, keepdims=True)
    grp = _first_lane((pg == p_sel) & is_g, lane)
    lo = MOE_GROUPS + MOE_EPG * grp
    is_e = (lane >= lo) & (lane < lo + MOE_EPG)
    el = jnp.where(is_e, logits, neg)
    ee = jnp.exp(el - jnp.max(el, axis=-1, keepdims=True))
    pe = ee / jnp.sum(ee, axis=-1, keepdims=True)
    t1 = jnp.max(pe, axis=-1, keepdims=True)
    i1 = _first_lane((pe == t1) & is_e, lane)
    rest = jnp.where(is_e & (lane != i1), pe, -1.0)
    t2 = jnp.max(rest, axis=-1, keepdims=True)
    i2 = _first_lane(rest == t2, lane)
    tsum = t1 + t2
    w1, w2 = p_sel * t1 / tsum, p_sel * t2 / tsum
    e1, e2 = i1 - MOE_GROUPS, i2 - MOE_GROUPS
    oh1, oh2 = lane == e1, lane == e2
    cnt = jnp.where(oh1 | oh2, 1.0, 0.0)
    r_i = lax.broadcasted_iota(jnp.int32, (n, n), 0)
    c_i = lax.broadcasted_iota(jnp.int32, (n, n), 1)
    tri = jnp.where(c_i < r_i, 1.0, 0.0).astype(BF16)
    before = _dot(tri, cnt.astype(BF16)) + run_ref[...]
    r1 = jnp.sum(jnp.where(oh1, before, 0.0), axis=-1, keepdims=True)
    r2 = jnp.sum(jnp.where(oh2, before, 0.0), axis=-1, keepdims=True)
    run_ref[...] = run_ref[...] + jnp.sum(cnt, axis=0, keepdims=True)
    slab = jnp.zeros(logits.shape, F32)
    for k, v in enumerate((e1, e2, r1, r2, w1, w2)):
        slab = jnp.where(lane == float(k), v, slab)
    return slab


def _norm_router_kernel(x_ref, g_ref, sh_ref, sc_ref, wr_ref, h_ref, rt_ref, cnt_ref, run_ref):
    @pl.when(pl.program_id(0) == 0)
    def _():
        run_ref[...] = jnp.zeros_like(run_ref)
    h = _norm_mod(x_ref, g_ref, sh_ref, sc_ref)
    h_ref[...] = _pack_halves(h)
    rt_ref[...] = _route(_dot(h.astype(BF16), wr_ref[...]), run_ref)
    cnt_ref[...] = run_ref[...]


def _seg_map(seg_rows, tile, nseg):
    per = seg_rows // tile
    return lambda i: jnp.minimum(i // per, nseg - 1)


def norm_mod(stream, g, shift, scale, seg_rows, out_dtype):
    main, tail = stream
    d = main.shape[1]
    tr = ROW_TILE
    seg = _seg_map(seg_rows, tr, shift.shape[0])
    n_main = main.shape[0] // tr
    rows = main.shape[0] + tail.shape[0]
    return pl.pallas_call(
        functools.partial(_norm_kernel, n_main=n_main), out_shape=SDS((rows, d), out_dtype), grid=(rows // tr,),
        in_specs=[BS((tr, d), lambda i: (jnp.minimum(i, n_main - 1), 0)),
                  BS((tr, d), lambda i: (jnp.maximum(i - n_main, 0), 0)), BS((1, d), lambda i: (0, 0)),
                  BS((None, 1, d), lambda i: (seg(i), 0, 0)), BS((None, 1, d), lambda i: (seg(i), 0, 0))],
        out_specs=BS((tr, d), lambda i: (i, 0)),
        compiler_params=_params("parallel"), name="norm_mod")(main, tail, g, shift, scale)


def norm_router(x, g, shift, scale, w_router, rows, seg_rows):
    d = x.shape[1]
    tr = ROW_TILE
    seg = _seg_map(seg_rows, tr, shift.shape[0])
    return pl.pallas_call(
        _norm_router_kernel,
        out_shape=(SDS((rows, d // 2), jnp.uint32), SDS((rows, ROUTER_LANES), F32),
                   SDS((1, ROUTER_LANES), F32)),
        grid=(rows // tr,),
        in_specs=[BS((tr, d), lambda i: (i, 0)), BS((1, d), lambda i: (0, 0)),
                  BS((None, 1, d), lambda i: (seg(i), 0, 0)), BS((None, 1, d), lambda i: (seg(i), 0, 0)),
                  BS((d, ROUTER_LANES), lambda i: (0, 0))],
        out_specs=(BS((tr, d // 2), lambda i: (i, 0)), BS((tr, ROUTER_LANES), lambda i: (i, 0)),
                   BS((1, ROUTER_LANES), lambda i: (0, 0))),
        scratch_shapes=[pltpu.VMEM((1, ROUTER_LANES), F32)],
        compiler_params=_params("arbitrary"), name="norm_router")(x, g, shift, scale, w_router)


def _ada_kernel(a_ref, w_ref, b_ref, o_ref):
    o_ref[...] = _dot(a_ref[...], w_ref[...].astype(BF16)) + b_ref[...]


def ada_modulation(act, ada_w, ada_b):
    depth, d, n = ada_w.shape
    tn = ROW_TILE
    return pl.pallas_call(
        _ada_kernel, out_shape=SDS((depth, act.shape[0], n), F32), grid=(depth, n // tn),
        in_specs=[BS(act.shape, lambda l, j: (0, 0)), BS((None, d, tn), lambda l, j: (l, 0, j)),
                  BS((None, 1, tn), lambda l, j: (l, 0, j))],
        out_specs=BS((None, act.shape[0], tn), lambda l, j: (l, 0, j)),
        compiler_params=_params("parallel", "parallel"), name="ada_modulation")(
            act, ada_w, ada_b.reshape(depth, 1, n))


def _in_proj_kernel(a_ref, b_ref, *refs, n_side):
    side_in, o_ref, side_out, bq_ref = refs[:n_side], refs[n_side], refs[n_side + 1:-1], refs[-1]
    @pl.when(pl.program_id(1) == 0)
    def _():
        bq_ref[...] = b_ref[...].astype(BF16)
    o_ref[...] = _dot(a_ref[...], bq_ref[...]).astype(o_ref.dtype)
    for s_in, s_out in zip(side_in, side_out):
        s_out[...] = s_in[...].astype(BF16)


BF16_SUBLANES = 16


def in_proj(a, w, layer, rows, out_dtype, side):
    _, k, n = w.shape
    tm, tn = ROW_TILE, COL_TILE
    nj, ni = n // tn, rows // tm
    n_steps = nj * ni
    in_specs, out_specs, out_shapes = [], [], []
    for s in side:
        r, c = s.shape[-2:]
        n_mat = int(np.prod(s.shape[1:-2]))
        rps = BF16_SUBLANES
        while n_mat * (r // rps) > n_steps:
            rps *= 2
        assert r % rps == 0
        bpm = r // rps
        blk = lambda j, i, last=n_mat * bpm - 1: jnp.minimum(j * ni + i, last)
        if s.ndim == 4:
            in_specs.append(BS((None, None, rps, c),
                               lambda j, i, blk=blk, bpm=bpm: (layer, blk(j, i) // bpm, blk(j, i) % bpm, 0)))
            out_specs.append(BS((None, rps, c), lambda j, i, blk=blk, bpm=bpm: (blk(j, i) // bpm, blk(j, i) % bpm, 0)))
        else:
            in_specs.append(BS((None, rps, c), lambda j, i, blk=blk: (layer, blk(j, i), 0)))
            out_specs.append(BS((rps, c), lambda j, i, blk=blk: (blk(j, i), 0)))
        out_shapes.append(SDS(s.shape[1:], BF16))
    res = pl.pallas_call(
        functools.partial(_in_proj_kernel, n_side=len(side)),
        out_shape=[SDS((rows, n), out_dtype)] + out_shapes, grid=(nj, ni),
        in_specs=[BS((tm, k), lambda j, i: (i, 0)),
                  BS((None, k, tn), lambda j, i: (layer, 0, j), pipeline_mode=pl.Buffered(1))] + in_specs,
        out_specs=[BS((tm, tn), lambda j, i: (i, j))] + out_specs,
        scratch_shapes=[pltpu.VMEM((k, tn), BF16)],
        compiler_params=_params("arbitrary", "arbitrary"), name="in_proj")(a, w, *side)
    return res[0], list(res[1:])


def _mm_res_kernel(a_ref, b_ref, x_ref, xt_ref, g_ref, o_ref, *, n_main):
    upd = g_ref[...] * _dot(a_ref[...], b_ref[...])

    @pl.when(pl.program_id(1) < n_main)
    def _():
        o_ref[...] = x_ref[...] + upd

    @pl.when(pl.program_id(1) >= n_main)
    def _():
        o_ref[...] = xt_ref[...] + upd


def matmul_residual(a, b, stream, gate, rows, seg_rows):
    k, n = b.shape
    tm, tn = ROW_TILE, COL_TILE
    seg = _seg_map(seg_rows, tm, gate.shape[0])
    main, tail = stream
    n_main = main.shape[0] // tm
    return pl.pallas_call(
        functools.partial(_mm_res_kernel, n_main=n_main), out_shape=SDS((rows, n), F32),
        grid=(n // tn, rows // tm),
        in_specs=[BS((tm, k), lambda j, i: (i, 0)), BS((k, tn), lambda j, i: (0, j)),
                  BS((tm, tn), lambda j, i: (jnp.minimum(i, n_main - 1), j)),
                  BS((tm, tn), lambda j, i: (jnp.maximum(i - n_main, 0), j)),
                  BS((None, 1, tn), lambda j, i: (seg(i), 0, j))],
        out_specs=BS((tm, tn), lambda j, i: (i, j)),
        compiler_params=_params("parallel", "parallel"), name="out_proj")(a, b, main, tail, gate)


def _merge_kernel(a_ref, b_ref, bt_ref, r_ref, rt_ref, ga_ref, gb_ref, gc_ref, wa_ref, wb_ref, wc_ref, o_ref, *,
                  n_main):
    is_tail = pl.program_id(1) >= n_main

    def branch(x, w_ref, g_ref):
        return jax.nn.sigmoid(g_ref[...].astype(F32)) * _dot(x, w_ref[...])
    y = (branch(a_ref[...], wa_ref, ga_ref)
         + branch(jnp.where(is_tail, bt_ref[...], b_ref[...]), wb_ref, gb_ref)
         + branch(jnp.where(is_tail, rt_ref[...], r_ref[...]), wc_ref, gc_ref))
    o_ref[...] = y.astype(o_ref.dtype)


def merge_branches(z, a, bb, r, w_a, w_b, w_c, rows):
    d = w_a.shape[1]
    tm, tn = ROW_TILE, COL_TILE
    nb = d // tn
    n_main = bb[0].shape[0] // tm
    row_spec = lambda arr: BS((tm, arr.shape[1]), lambda j, i: (i, 0))
    main_spec = lambda arr: BS((tm, arr.shape[1]), lambda j, i: (jnp.minimum(i, n_main - 1), 0))
    tail_spec = lambda arr: BS((tm, arr.shape[1]), lambda j, i: (jnp.maximum(i - n_main, 0), 0))
    w_spec = lambda arr: BS((arr.shape[0], tn), lambda j, i: (0, j))
    gate_spec = lambda g: BS((tm, tn), lambda j, i: (i, g * nb + j))
    return pl.pallas_call(
        functools.partial(_merge_kernel, n_main=n_main), out_shape=SDS((rows, d), BF16), grid=(nb, rows // tm),
        in_specs=[row_spec(a), main_spec(bb[0]), tail_spec(bb[1]), main_spec(r[0]), tail_spec(r[1]),
                  gate_spec(0), gate_spec(1), gate_spec(2), w_spec(w_a), w_spec(w_b), w_spec(w_c)],
        out_specs=BS((tm, tn), lambda j, i: (i, j)),
        compiler_params=_params("parallel", "parallel"), name="merge_branches")(
            a, bb[0], bb[1], r[0], r[1], z, z, z, w_a, w_b, w_c)


GM_CHUNKS_PER_STEP = 4


def _gmlp_kernel(u_ref, v_ref, ng_ref, ws_ref, bs_ref, o_ref):
    for c in range(GM_CHUNKS_PER_STEP):
        rows = slice(c * CHUNK, (c + 1) * CHUNK)
        v = jax.nn.gelu(v_ref[rows, :].astype(F32))
        mu = jnp.mean(v, axis=-1, keepdims=True)
        var = jnp.mean(jnp.square(v - mu), axis=-1, keepdims=True)
        vn = ((v - mu) * lax.rsqrt(var + EPS) * ng_ref[...]).astype(BF16)
        for g in range(N_MIX_HEADS):
            cols = slice(g * HEAD_DIM, (g + 1) * HEAD_DIM)
            mixed = _dot(ws_ref[g], vn[:, cols]) + bs_ref[g]
            u = jax.nn.gelu(u_ref[rows, cols].astype(F32))
            o_ref[rows, cols] = (u * mixed).astype(o_ref.dtype)


def chunk_gmlp(z, cols, norm_g, ws, bs, rows):
    tr = GM_CHUNKS_PER_STEP * CHUNK
    ub, vb = cols["gm_u"] // GM_WIDTH, cols["gm_v"] // GM_WIDTH
    bs_b = jnp.broadcast_to(bs[:, :, None], bs.shape + (HEAD_DIM,)).astype(F32)
    return pl.pallas_call(
        _gmlp_kernel, out_shape=SDS((rows, GM_WIDTH), BF16), grid=(rows // tr,),
        in_specs=[BS((tr, GM_WIDTH), lambda i: (i, ub)), BS((tr, GM_WIDTH), lambda i: (i, vb)),
                  BS((1, GM_WIDTH), lambda i: (0, 0)), BS(ws.shape, lambda i: (0, 0, 0)),
                  BS(bs_b.shape, lambda i: (0, 0, 0))],
        out_specs=BS((tr, GM_WIDTH), lambda i: (i, 0)),
        compiler_params=_params("parallel"), name="chunk_gmlp")(
            z, z, norm_g.reshape(1, GM_WIDTH), ws.astype(BF16), bs_b)


def _na_bias_tables(rpb, grid_rows):
    nq = grid_rows // NA_QROWS
    depth, n_heads = rpb.shape[:2]
    n_dr = 2 * NA_ROWS - 1
    i = np.arange(NA_QROWS)[:, None]
    m = np.arange(NA_KROWS)[None, :]
    c = np.arange(GRID_W)[:, None]
    w = np.arange(GRID_W)[None, :]
    c_start = np.clip(c - NA_COLS // 2, 0, GRID_W - NA_COLS)
    col_ok = (w >= c_start) & (w < c_start + NA_COLS)
    dc = np.clip(w - c + NA_COLS - 1, 0, 2 * NA_COLS - 2)
    col_sel = (dc.reshape(-1)[None, :] == np.arange(2 * NA_COLS - 1)[:, None]).astype(np.float32)
    band = jnp.einsum("lhab,bq->lhaq", rpb.astype(F32), col_sel, precision=lax.Precision.HIGHEST)
    band = jnp.where(col_ok[None, None, None], band.reshape(depth, n_heads, n_dr, GRID_W, GRID_W), NEG_INF)
    band = jnp.concatenate([band, jnp.full((depth, n_heads, 1, GRID_W, GRID_W), NEG_INF, F32)], axis=2)
    slots, offsets = [], []
    for j in (0, 1, nq - 1):
        base = int(np.clip(NA_QROWS * j - NA_ROWS // 2, 0, grid_rows - NA_KROWS))
        r = NA_QROWS * j + i
        kr = base + m
        r_start = np.clip(r - NA_ROWS // 2, 0, grid_rows - NA_ROWS)
        row_ok = (kr >= r_start) & (kr < r_start + NA_ROWS)
        slot = np.where(row_ok, kr - r + NA_ROWS - 1, n_dr)
        for s in range(NA_QROWS // NA_SUB_QROWS):
            sub = slice(s * NA_SUB_QROWS, (s + 1) * NA_SUB_QROWS)
            off = min(int(np.argmax(row_ok[sub].any(axis=0))), NA_KROWS - NA_SUB_KROWS)
            assert not row_ok[sub, :off].any() and not row_ok[sub, off + NA_SUB_KROWS:].any()
            slots.append(slot[sub, off:off + NA_SUB_KROWS])
            offsets.append(off)
    n_sub = NA_QROWS // NA_SUB_QROWS
    slots = jnp.asarray(np.stack(slots).reshape(-1), jnp.int32)
    out_block = (n_sub, NA_SUB_QROWS * GRID_W, NA_SUB_KROWS * GRID_W)
    grid_spec = pltpu.PrefetchScalarGridSpec(
        num_scalar_prefetch=1, grid=(depth, 3, n_heads),
        in_specs=[BS((None, None, n_dr + 1, GRID_W, GRID_W), lambda l, k, h, s: (l, h, 0, 0, 0))],
        out_specs=BS((None, None, None) + out_block, lambda l, k, h, s: (l, k, h, 0, 0, 0)))
    tables = pl.pallas_call(
        _na_bias_kernel, grid_spec=grid_spec, out_shape=SDS((depth, 3, n_heads) + out_block, F32),
        compiler_params=_params("parallel", "parallel", "parallel"), name="na_bias_tables")(slots, band)
    return tables, np.asarray(offsets).reshape(3, n_sub)


def _na_bias_kernel(slot_ref, band_ref, o_ref):
    kind = pl.program_id(1)
    for s in range(NA_QROWS // NA_SUB_QROWS):
        for i in range(NA_SUB_QROWS):
            row = ((kind * (NA_QROWS // NA_SUB_QROWS) + s) * NA_SUB_QROWS + i) * NA_SUB_KROWS
            tiles = [band_ref[slot_ref[row + m]] for m in range(NA_SUB_KROWS)]
            o_ref[s, i * GRID_W:(i + 1) * GRID_W, :] = jnp.concatenate(tiles, axis=1)


def _na_kernel(q_ref, k_ref, v_ref, kc_ref, vc_ref, bias_ref, o_ref, *, grid_rows, offsets):
    j = pl.program_id(2)
    nq = grid_rows // NA_QROWS
    kind = jnp.where(j == 0, 0, jnp.where(j == nq - 1, 2, 1))
    base_row = jnp.clip(NA_QROWS * j - NA_ROWS // 2, 0, grid_rows - NA_KROWS)
    nk = NA_SUB_KROWS * GRID_W
    nsq = NA_SUB_QROWS * GRID_W
    for s in range(NA_QROWS // NA_SUB_QROWS):
        off = jnp.where(j == 0, int(offsets[0, s]), jnp.where(j == nq - 1, int(offsets[2, s]), int(offsets[1, s])))
        start = pl.multiple_of((base_row + off) * GRID_W, (NA_ROWS // 2) * GRID_W)
        rows = slice(s * nsq, (s + 1) * nsq)
        q = (q_ref[rows, :].astype(F32) * (HEAD_DIM ** -0.5)).astype(BF16)
        s_loc = _dot_nt(q, k_ref[pl.ds(start, nk), :]) + bias_ref[kind, s]
        s_ctx = _dot_nt(q, kc_ref[...])
        mx = jnp.maximum(jnp.max(s_loc, axis=-1, keepdims=True), jnp.max(s_ctx, axis=-1, keepdims=True))
        e_loc = jnp.exp(s_loc - mx)
        e_ctx = jnp.exp(s_ctx - mx)
        den = jnp.sum(e_loc, axis=-1, keepdims=True) + jnp.sum(e_ctx, axis=-1, keepdims=True)
        o = _dot(e_loc.astype(BF16), v_ref[pl.ds(start, nk), :]) + _dot(e_ctx.astype(BF16), vc_ref[...])
        o_ref[rows, :] = (o / den).astype(o_ref.dtype)


def neighbourhood_attention(z, cols, bias_tables, layer, batch, seq, n_ctx):
    bias, offsets = bias_tables
    assert np.all(offsets % (NA_ROWS // 2) == 0)
    grid_rows = seq // GRID_W
    assert grid_rows % NA_QROWS == 0 and grid_rows >= 2 * NA_KROWS
    nq = grid_rows // NA_QROWS
    qb = NA_QROWS * GRID_W
    qc, kc, vc = (cols[n] // HEAD_DIM for n in ("na_q", "na_k", "na_v"))
    ctx_blk0 = batch * seq // n_ctx
    return pl.pallas_call(
        functools.partial(_na_kernel, grid_rows=grid_rows, offsets=offsets),
        out_shape=SDS((batch * seq, NA_WIDTH), BF16), grid=(N_MIX_HEADS, batch, nq),
        in_specs=[BS((qb, HEAD_DIM), lambda h, b, j: (b * nq + j, qc + h)),
                  BS((seq, HEAD_DIM), lambda h, b, j: (b, kc + h)),
                  BS((seq, HEAD_DIM), lambda h, b, j: (b, vc + h)),
                  BS((n_ctx, HEAD_DIM), lambda h, b, j: (ctx_blk0 + b, kc + h)),
                  BS((n_ctx, HEAD_DIM), lambda h, b, j: (ctx_blk0 + b, vc + h)),
                  BS((None, 3, None) + bias.shape[3:], lambda h, b, j: (layer, 0, h, 0, 0, 0))],
        out_specs=BS((qb, HEAD_DIM), lambda h, b, j: (b * nq + j, h)),
        compiler_params=_params("parallel", "parallel", "parallel"), name="neighbourhood_attention")(
            z, z, z, z, z, bias)


def _ctx_attn_kernel(q_ref, k_ref, v_ref, o_ref):
    q = (q_ref[...].astype(F32) * (HEAD_DIM ** -0.5)).astype(BF16)
    s = _dot_nt(q, k_ref[...])
    e = jnp.exp(s - jnp.max(s, axis=-1, keepdims=True))
    o = _dot(e.astype(BF16), v_ref[...]) / jnp.sum(e, axis=-1, keepdims=True)
    o_ref[...] = o.astype(o_ref.dtype)


def context_attention(z, cols, batch, seq, n_ctx):
    qc, kc, vc = (cols[n] // HEAD_DIM for n in ("na_q", "na_k", "na_v"))
    blk0 = batch * seq // n_ctx
    spec = lambda col: BS((n_ctx, HEAD_DIM), lambda b, h: (blk0 + b, col + h))
    return pl.pallas_call(
        _ctx_attn_kernel, out_shape=SDS((batch * n_ctx, NA_WIDTH), BF16), grid=(batch, N_MIX_HEADS),
        in_specs=[spec(qc), spec(kc), spec(vc)],
        out_specs=BS((n_ctx, HEAD_DIM), lambda b, h: (b, h)),
        compiler_params=_params("parallel", "parallel"), name="context_attention")(z, z, z)


def _rope(t_ref, cos_ref, sin_ref):
    t = t_ref[...].astype(F32)
    lane = lax.broadcasted_iota(jnp.int32, t.shape, 1)
    quarter = HEAD_DIM // 4
    partner = jnp.where((lane % (2 * quarter)) < quarter,
                        pltpu.roll(t, HEAD_DIM - quarter, 1), pltpu.roll(t, quarter, 1))
    return t * cos_ref[...] + partner * sin_ref[...]


def _rope_tables(n, identity):
    if identity:
        return jnp.ones((n, HEAD_DIM), F32), jnp.zeros((n, HEAD_DIM), F32)
    nf = HEAD_DIM // 4
    pos = jnp.arange(n)
    p_row = (pos // GRID_W).astype(F32)
    p_col = (pos % GRID_W).astype(F32)
    inv = ROPE_BASE ** (-jnp.arange(nf, dtype=F32) / nf)
    a_row = p_row[:, None] * inv[None, :]
    a_col = p_col[:, None] * inv[None, :]
    cos = jnp.concatenate([jnp.cos(a_row), jnp.cos(a_row), jnp.cos(a_col), jnp.cos(a_col)], axis=-1)
    sin = jnp.concatenate([-jnp.sin(a_row), jnp.sin(a_row), -jnp.sin(a_col), jnp.sin(a_col)], axis=-1)
    return cos, sin


def _decay_tables(dec_f, dec_b):
    log_gf = jnp.log1p(-jnp.exp2(dec_f.astype(F32)))
    log_gb = jnp.log1p(-jnp.exp2(dec_b.astype(F32)))
    idx = jnp.arange(CHUNK, dtype=F32)
    rel = idx[:, None] - idx[None, :]
    d_f = jnp.where(rel >= 0, jnp.exp(log_gf[:, None, None] * jnp.maximum(rel, 0.0)), 0.0)
    d_b = jnp.where(rel < 0, jnp.exp(log_gb[:, None, None] * jnp.maximum(-rel, 0.0)), 0.0)
    bcast = lambda v: jnp.broadcast_to(v[:, :, None], v.shape + (RET_DK,))
    return dict(
        log_gf=log_gf, log_gb=log_gb,
        dfb=d_f + d_b,
        qdf=bcast(jnp.exp(log_gf[:, None] * (idx + 1.0))),
        qdb=bcast(jnp.exp(log_gb[:, None] * (CHUNK - idx))),
        kdf=bcast(jnp.exp(log_gf[:, None] * (CHUNK - 1.0 - idx))),
        kdb=bcast(jnp.exp(log_gb[:, None] * idx)),
        cdf=jnp.broadcast_to(jnp.exp(log_gf * CHUNK)[:, None, None], (log_gf.shape[0], 1, RET_DV)),
        cdb=jnp.broadcast_to(jnp.exp(log_gb * CHUNK)[:, None, None], (log_gb.shape[0], 1, RET_DV)),
    )


def _ctx_state_kernel(k_ref, v_ref, wf_ref, wb_ref, sf_ref, sb_ref):
    k = k_ref[...].astype(F32) * (RET_DK ** -0.5)
    v = v_ref[...]
    sf_ref[...] = _dot((k * wf_ref[...]).T.astype(BF16), v)
    sb_ref[...] = _dot((k * wb_ref[...]).T.astype(BF16), v)


def context_state(z, cols, tabs, batch, seq, n_ctx):
    pos = jnp.arange(n_ctx, dtype=F32)
    bcast = lambda v: jnp.broadcast_to(v[:, :, None], v.shape + (RET_DK,))
    w_f = bcast(jnp.exp(tabs["log_gf"][:, None] * ((n_ctx - 1.0) - pos)))
    w_b = bcast(jnp.exp(tabs["log_gb"][:, None] * pos))
    kc, vc = cols["ret_k"] // RET_DK, cols["ret_v"] // RET_DV
    blk0 = batch * seq // n_ctx
    out = SDS((batch, N_MIX_HEADS, RET_DK, RET_DV), F32)
    tab_spec = BS((None, n_ctx, RET_DK), lambda b, h: (h, 0, 0))
    out_spec = BS((None, None, RET_DK, RET_DV), lambda b, h: (b, h, 0, 0))
    return pl.pallas_call(
        _ctx_state_kernel, out_shape=(out, out), grid=(batch, N_MIX_HEADS),
        in_specs=[BS((n_ctx, RET_DK), lambda b, h: (blk0 + b, kc + h)),
                  BS((n_ctx, RET_DV), lambda b, h: (blk0 + b, vc + h)), tab_spec, tab_spec],
        out_specs=(out_spec, out_spec),
        compiler_params=_params("parallel", "parallel"), name="context_state")(z, z, w_f, w_b)


def _ret_state_kernel(kf_ref, vf_ref, cosf_ref, sinf_ref, kb_ref, vb_ref, cosb_ref, sinb_ref,
                      kdf_ref, kdb_ref, cdf_ref, cdb_ref, s0f_ref, s0b_ref, sf_ref, sb_ref, st_ref, *, cpg):
    seq_id = pl.program_id(1) * pl.num_programs(2) + pl.program_id(2)

    @pl.when(pl.program_id(0) == 0)
    def _():
        st_ref[2 * seq_id] = s0f_ref[...]
        st_ref[2 * seq_id + 1] = s0b_ref[...]

    def scan(k_ref, v_ref, cos_ref, sin_ref, kd_ref, cd_ref, out_ref, direction, order):
        kr = _rope(k_ref, cos_ref, sin_ref) * (RET_DK ** -0.5)
        slot = 2 * seq_id + direction
        s = st_ref[slot]
        for c in order:
            rows = slice(c * CHUNK, (c + 1) * CHUNK)
            out_ref[c] = s.astype(out_ref.dtype)
            kc = (kr[rows, :] * kd_ref[...]).T.astype(BF16)
            s = s * cd_ref[...] + _dot(kc, v_ref[rows, :])
        st_ref[slot] = s

    scan(kf_ref, vf_ref, cosf_ref, sinf_ref, kdf_ref, cdf_ref, sf_ref, 0, range(cpg))
    scan(kb_ref, vb_ref, cosb_ref, sinb_ref, kdb_ref, cdb_ref, sb_ref, 1, reversed(range(cpg)))


def _ret_out_kernel(q_ref, k_ref, v_ref, g_ref, cos_ref, sin_ref, sf_ref, sb_ref, dfb_ref, qdf_ref, qdb_ref,
                    o_ref, *, cpg):
    qr = _rope(q_ref, cos_ref, sin_ref)
    kr = (_rope(k_ref, cos_ref, sin_ref) * (RET_DK ** -0.5)).astype(BF16)
    for c in range(cpg):
        rows = slice(c * CHUNK, (c + 1) * CHUNK)
        qc = qr[rows, :]
        p = (_dot_nt(qc.astype(BF16), kr[rows, :]) * dfb_ref[...]).astype(BF16)
        o = (_dot(p, v_ref[rows, :]) + _dot((qc * qdf_ref[...]).astype(BF16), sf_ref[c])
             + _dot((qc * qdb_ref[...]).astype(BF16), sb_ref[c]))
        mu = jnp.mean(o, axis=-1, keepdims=True)
        var = jnp.mean(jnp.square(o - mu), axis=-1, keepdims=True)
        on = (o - mu) * lax.rsqrt(var + EPS)
        g = g_ref[rows, :].astype(F32)
        o_ref[rows, :] = (on * (g * jax.nn.sigmoid(g))).astype(o_ref.dtype)


def retention(z, cols, tabs, s0f, s0b, row0, n, batch, rope_identity):
    nc = n // CHUNK
    cpg = min(RET_CHUNKS_PER_STEP, nc)
    ng = nc // cpg
    gr = cpg * CHUNK
    blk0 = row0 // gr
    qc, kc = cols["ret_q"] // RET_DK, cols["ret_k"] // RET_DK
    vc, gc = cols["ret_v"] // RET_DV, cols["ret_g"] // RET_DV
    cos, sin = _rope_tables(n, rope_identity)
    BSg = lambda shape, f: BS(shape, lambda g, b, h: f(b, h, g))
    fwd = lambda b, h, g: blk0 + b * ng + g
    bwd = lambda b, h, g: blk0 + b * ng + (ng - 1 - g)
    head_tab = lambda w: BSg((None, CHUNK, w), lambda b, h, g: (h, 0, 0))
    c_tab = BSg((None, 1, RET_DV), lambda b, h, g: (h, 0, 0))
    s0_spec = BSg((None, None, RET_DK, RET_DV), lambda b, h, g: (b, h, 0, 0))
    st_shape = SDS((batch, N_MIX_HEADS, nc, RET_DK, RET_DV), BF16)
    st_spec = lambda rev: BSg((None, None, cpg, RET_DK, RET_DV),
                              lambda b, h, g: (b, h, (ng - 1 - g) if rev else g, 0, 0))
    sf, sb = pl.pallas_call(
        functools.partial(_ret_state_kernel, cpg=cpg), out_shape=(st_shape, st_shape),
        grid=(ng, batch, N_MIX_HEADS),
        in_specs=[BSg((gr, RET_DK), lambda b, h, g: (fwd(b, h, g), kc + h)),
                  BSg((gr, RET_DV), lambda b, h, g: (fwd(b, h, g), vc + h)),
                  BSg((gr, RET_DK), lambda b, h, g: (g, 0)), BSg((gr, RET_DK), lambda b, h, g: (g, 0)),
                  BSg((gr, RET_DK), lambda b, h, g: (bwd(b, h, g), kc + h)),
                  BSg((gr, RET_DV), lambda b, h, g: (bwd(b, h, g), vc + h)),
                  BSg((gr, RET_DK), lambda b, h, g: (ng - 1 - g, 0)),
                  BSg((gr, RET_DK), lambda b, h, g: (ng - 1 - g, 0)),
                  head_tab(RET_DK), head_tab(RET_DK), c_tab, c_tab, s0_spec, s0_spec],
        out_specs=(st_spec(False), st_spec(True)),
        scratch_shapes=[pltpu.VMEM((2 * batch * N_MIX_HEADS, RET_DK, RET_DV), F32)],
        compiler_params=_params("arbitrary", "arbitrary", "arbitrary"), name="retention_state")(
            z, z, cos, sin, z, z, cos, sin, tabs["kdf"], tabs["kdb"], tabs["cdf"], tabs["cdb"], s0f, s0b)
    return pl.pallas_call(
        functools.partial(_ret_out_kernel, cpg=cpg), out_shape=SDS((batch * n, RET_V_WIDTH), BF16),
        grid=(ng, batch, N_MIX_HEADS),
        in_specs=[BSg((gr, RET_DK), lambda b, h, g: (fwd(b, h, g), qc + h)),
                  BSg((gr, RET_DK), lambda b, h, g: (fwd(b, h, g), kc + h)),
                  BSg((gr, RET_DV), lambda b, h, g: (fwd(b, h, g), vc + h)),
                  BSg((gr, RET_DV), lambda b, h, g: (fwd(b, h, g), gc + h)),
                  BSg((gr, RET_DK), lambda b, h, g: (g, 0)), BSg((gr, RET_DK), lambda b, h, g: (g, 0)),
                  st_spec(False), st_spec(False),
                  head_tab(CHUNK), head_tab(RET_DK), head_tab(RET_DK)],
        out_specs=BSg((gr, RET_DV), lambda b, h, g: (b * ng + g, h)),
        compiler_params=_params("parallel", "parallel", "parallel"), name="retention_out")(
            z, z, z, z, cos, sin, sf, sb, tabs["dfb"], tabs["qdf"], tabs["qdb"])


def _dispatch_tables(routed, counts):
    t = routed.shape[0]
    n_assign = t * MOE_TOPK
    flat_e = routed[:, 0:MOE_TOPK].astype(jnp.int32).reshape(-1)
    rank = routed[:, MOE_TOPK:2 * MOE_TOPK].astype(jnp.int32).reshape(-1)
    flat_t = jnp.repeat(jnp.arange(t, dtype=jnp.int32), MOE_TOPK)
    counts = counts[0, :MOE_EXPERTS].astype(jnp.int32)
    padded = (counts + MOE_ROWS - 1) // MOE_ROWS * MOE_ROWS
    p_end = jnp.cumsum(padded)
    dest = (p_end - padded)[flat_e] + rank
    n_blocks = (n_assign + MOE_EXPERTS * (MOE_ROWS - 1)) // MOE_ROWS
    row_tok = jnp.zeros((n_blocks * MOE_ROWS,), jnp.int32).at[dest].set(
        flat_t, unique_indices=True, mode="promise_in_bounds")
    blk_start = jnp.arange(n_blocks, dtype=jnp.int32) * MOE_ROWS
    block_e = jnp.minimum(jnp.sum((p_end[None, :] <= blk_start[:, None]).astype(jnp.int32), axis=1),
                          MOE_EXPERTS - 1)
    n_used = (p_end[-1] // MOE_ROWS).astype(jnp.int32).reshape(1)
    return block_e, n_used, row_tok.reshape(n_blocks, 1, MOE_ROWS), dest.reshape(t, MOE_TOPK).astype(jnp.int32)


def _start_row_gather(src_hbm, idx_ref, idx_of_row, dst, sem, n):
    def issue(r, carry):
        pltpu.make_async_copy(src_hbm.at[pl.ds(idx_ref[0, idx_of_row(r)], 1)], dst.at[pl.ds(r, 1)], sem).start()
        return carry
    lax.fori_loop(0, n, issue, 0, unroll=8)


def _wait_row_gather(src_hbm, dst, sem):
    pltpu.make_async_copy(src_hbm.at[pl.ds(0, dst.shape[0])], dst, sem).wait()


MOE_GATHER_AHEAD = 3


def _moe_kernel(be_ref, nu_ref, *refs):
    ahead = MOE_GATHER_AHEAD
    tok_refs = refs[:ahead + 1]
    h_hbm, w1_ref, w3_ref, w2_ref, y_ref, xb, sem = refs[ahead + 1:]
    i = pl.program_id(0)
    n_used = nu_ref[0]
    n_slots = ahead + 1
    slot = i % n_slots
    row = lambda r: r

    for a in range(ahead):
        @pl.when((i == 0) & (a < n_used))
        def _(a=a):
            _start_row_gather(h_hbm, tok_refs[a], row, xb.at[a], sem.at[a], MOE_ROWS)

    @pl.when(i + ahead < n_used)
    def _():
        nxt = (i + ahead) % n_slots
        _start_row_gather(h_hbm, tok_refs[ahead], row, xb.at[nxt], sem.at[nxt], MOE_ROWS)

    @pl.when(i < n_used)
    def _():
        _wait_row_gather(h_hbm, xb.at[slot], sem.at[slot])
        half = xb.shape[2]
        x_lo, x_hi = (v.astype(BF16) for v in _unpack_halves(xb[slot]))
        up = lambda w_ref: _dot(x_lo, w_ref[0:half, :]) + _dot(x_hi, w_ref[half:2 * half, :])
        h1 = up(w1_ref)
        h3 = up(w3_ref)
        hm = (h1 * jax.nn.sigmoid(h1) * h3).astype(BF16)
        y_ref[...] = _pack_halves(_dot(hm, w2_ref[...]))

    @pl.when(i >= n_used)
    def _():
        y_ref[...] = jnp.zeros_like(y_ref)


def moe_experts(h2, tables, w1, w3, w2):
    block_e, n_used, row_tok, _ = tables
    n_blocks = row_tok.shape[0]
    d, hid = w1.shape[1], w1.shape[2]
    assert h2.shape[1] == d // 2 and h2.dtype == jnp.uint32
    grid_spec = pltpu.PrefetchScalarGridSpec(
        num_scalar_prefetch=2, grid=(n_blocks,),
        in_specs=[BS((None, 1, MOE_ROWS), lambda i, be, nu, a=a: (jnp.minimum(i + a, n_blocks - 1), 0, 0),
                     memory_space=pltpu.SMEM) for a in range(MOE_GATHER_AHEAD + 1)] + [
                  BS(memory_space=pl.ANY),
                  BS((None, d, hid), lambda i, be, nu: (be[i], 0, 0)),
                  BS((None, d, hid), lambda i, be, nu: (be[i], 0, 0)),
                  BS((None, hid, d), lambda i, be, nu: (be[i], 0, 0))],
        out_specs=BS((MOE_ROWS, d // 2), lambda i, be, nu: (i, 0)),
        scratch_shapes=[pltpu.VMEM((MOE_GATHER_AHEAD + 1, MOE_ROWS, d // 2), jnp.uint32),
                        pltpu.SemaphoreType.DMA((MOE_GATHER_AHEAD + 1,))])
    return pl.pallas_call(
        _moe_kernel, out_shape=SDS((n_blocks * MOE_ROWS, d // 2), jnp.uint32), grid_spec=grid_spec,
        compiler_params=_params("arbitrary"), name="moe_experts")(
            block_e, n_used, *([row_tok] * (MOE_GATHER_AHEAD + 1)), h2, w1, w3, w2)


def _combine_kernel(pos_ref, pos_next_ref, y_hbm, x_ref, w_ref, g_ref, ng_ref, nsh_ref, nsc_ref, *refs,
                    keep_stream):
    outs, (yb, sem) = refs[:-2], refs[-2:]
    i = pl.program_id(0)
    tt = COMBINE_ROWS
    slot = i % 2
    idx = lambda r: r

    @pl.when(i == 0)
    def _():
        _start_row_gather(y_hbm, pos_ref, idx, yb.at[0], sem.at[0], MOE_TOPK * tt)

    @pl.when(i + 1 < pl.num_programs(0))
    def _():
        _start_row_gather(y_hbm, pos_next_ref, idx, yb.at[1 - slot], sem.at[1 - slot], MOE_TOPK * tt)

    _wait_row_gather(y_hbm, yb.at[slot], sem.at[slot])
    w = w_ref[...]
    w0, w1 = (w[:, 2 * MOE_TOPK + k:2 * MOE_TOPK + k + 1] for k in range(MOE_TOPK))
    half = yb.shape[2]
    halves = (slice(0, half), slice(half, 2 * half))
    ya = _unpack_halves(yb[slot, 0:tt, :])
    yb_ = _unpack_halves(yb[slot, tt:2 * tt, :])
    xn = [x_ref[:, s] + g_ref[:, s] * (a * w0 + b * w1) for s, a, b in zip(halves, ya, yb_)]
    if keep_stream:
        for s, v in zip(halves, xn):
            outs[0][:, s] = v
    h_ref = outs[-1]
    ms = sum(jnp.sum(v * v, axis=-1, keepdims=True) for v in xn) / (2 * half)
    inv = lax.rsqrt(ms + EPS)
    for s, v in zip(halves, xn):
        h_ref[:, s] = ((v * inv * ng_ref[:, s]) * (1.0 + nsc_ref[:, s]) + nsh_ref[:, s]).astype(h_ref.dtype)


def moe_combine(x1, y, pos, routed, gate, rows, seg_rows, norm_g, norm_shift, norm_scale, out_dtype, keep_stream):
    d = x1.shape[1]
    tt = COMBINE_ROWS
    nt = rows // tt
    seg = _seg_map(seg_rows, tt, gate.shape[0])
    nseg = _seg_map(seg_rows, tt, norm_shift.shape[0])
    pos3 = pos.reshape(nt, tt, MOE_TOPK).transpose(0, 2, 1).reshape(nt, 1, MOE_TOPK * tt)
    row_spec = BS((tt, d), lambda i: (i, 0))
    out_shape = ((SDS((rows, d), F32),) if keep_stream else ()) + (SDS((rows, d), out_dtype),)
    res = pl.pallas_call(
        functools.partial(_combine_kernel, keep_stream=keep_stream), out_shape=out_shape, grid=(nt,),
        in_specs=[BS((None, 1, MOE_TOPK * tt), lambda i: (i, 0, 0), memory_space=pltpu.SMEM),
                  BS((None, 1, MOE_TOPK * tt), lambda i: (jnp.minimum(i + 1, nt - 1), 0, 0),
                     memory_space=pltpu.SMEM),
                  BS(memory_space=pl.ANY), row_spec,
                  BS((tt, ROUTER_LANES), lambda i: (i, 0)), BS((None, 1, d), lambda i: (seg(i), 0, 0)),
                  BS((1, d), lambda i: (0, 0)), BS((None, 1, d), lambda i: (nseg(i), 0, 0)),
                  BS((None, 1, d), lambda i: (nseg(i), 0, 0))],
        out_specs=tuple(row_spec for _ in out_shape),
        scratch_shapes=[pltpu.VMEM((2, MOE_TOPK * tt, d // 2), jnp.uint32), pltpu.SemaphoreType.DMA((2,))],
        compiler_params=_params("arbitrary"), name="moe_combine")(
            pos3, pos3, y, x1, routed, gate, norm_g, norm_shift, norm_scale)
    return res if keep_stream else res[0]


def kernel(x, c, ctx, c_ctx, ada_w, ada_b, norm1_g, w_in, gm_norm_g, gm_ws, gm_bs, na_rpb, ret_decay_fwd,
           ret_decay_bwd, w_branch_a, w_branch_b, w_branch_c, w_out, norm2_g, moe_w_group, moe_w_expert,
           moe_w1, moe_w3, moe_w2, final_norm_g):
    batch, seq, d = x.shape
    n_ctx = ctx.shape[1]
    depth = ada_w.shape[0]
    t_lat, t_ctx = batch * seq, batch * n_ctx
    t_all = t_lat + t_ctx
    cols, in_total = _col_offsets(d)
    assert w_in.shape[2] == in_total and seq % ROW_TILE == 0 and t_ctx % ROW_TILE == 0
    assert 8 >= batch + 1

    cond = jnp.concatenate([c, c_ctx[None, :], jnp.zeros((8 - batch - 1, d), c.dtype)], axis=0)
    mod = ada_modulation(jax.nn.silu(cond).astype(BF16), ada_w, ada_b)[:, :batch + 1]
    mod = mod.reshape(depth, batch + 1, 6, 1, d)

    na_bias = _na_bias_tables(na_rpb, seq // GRID_W)
    stream = (x.reshape(t_lat, d), ctx.reshape(t_ctx, d))
    h = norm_mod(stream, norm1_g[0].reshape(1, d), mod[0, :, 0], mod[0, :, 1], seq, BF16)
    for l in range(depth):
        need_ctx = l < depth - 1
        rows = t_all if need_ctx else t_lat
        sh1, sc1, g1, sh2, sc2, g2 = (mod[l, :, k] for k in range(6))

        z, (w_a, w_b, w_c, w_o, w1, w3, w2) = in_proj(
            h, w_in, l, t_all, BF16, [w_branch_a, w_branch_b, w_branch_c, w_out, moe_w1, moe_w3, moe_w2])

        a = chunk_gmlp(z, cols, gm_norm_g[l], gm_ws[l], gm_bs[l], rows)
        bb = neighbourhood_attention(z, cols, na_bias, l, batch, seq, n_ctx)
        tabs = _decay_tables(ret_decay_fwd[l], ret_decay_bwd[l])
        s0f, s0b = context_state(z, cols, tabs, batch, seq, n_ctx)
        r = retention(z, cols, tabs, s0f, s0b, 0, seq, batch, False)
        if need_ctx:
            zeros = jnp.zeros_like(s0f)
            bb = (bb, context_attention(z, cols, batch, seq, n_ctx))
            r = (r, retention(z, cols, tabs, zeros, zeros, t_lat, n_ctx, batch, True))
        else:
            bb, r = (bb, bb), (r, r)

        y = merge_branches(z, a, bb, r, w_a, w_b, w_c, rows)
        x1 = matmul_residual(y, w_o, stream, g1, rows, seq)

        w_router = jnp.concatenate(
            [moe_w_group[l], moe_w_expert[l],
             jnp.zeros((d, ROUTER_LANES - MOE_GROUPS - MOE_EXPERTS), F32)], axis=1).astype(BF16)
        h2, routed, counts = norm_router(x1, norm2_g[l].reshape(1, d), sh2, sc2, w_router, rows, seq)
        tables = _dispatch_tables(routed, counts)
        y_moe = moe_experts(h2, tables, w1, w3, w2)
        if need_ctx:
            xs, h = moe_combine(x1, y_moe, tables[3], routed, g2, rows, seq, norm1_g[l + 1].reshape(1, d),
                                mod[l + 1, :, 0], mod[l + 1, :, 1], BF16, True)
            stream = (xs, xs)
        else:
            zero = jnp.zeros((1, 1, d), F32)
            out = moe_combine(x1, y_moe, tables[3], routed, g2, rows, seq, final_norm_g.reshape(1, d),
                              zero, zero, F32, False)
    return out.reshape(batch, seq, d)
```
